```python
import math
import jax
import jax.numpy as jnp
from jax import lax
import numpy as np

D_MODEL = 1024
BATCH = 8
SEQ = 4096
DEPTH = 1
DEC_BATCH = 32
DEC_SEQ = 2048
PAST_LEN = 128

A_HEADS = 4
A_HD = 64
A_VD = 2 * A_HD
A_ROT = A_HD // 4
A_THETA = 500000.0
Q_BLOCK = 128
R_HEADS = 4
R_DK = 64
R_DV = 2 * R_DK
R_CHUNK = 128
R_THETA = 10000.0
N_GROUPS = 4
EXPERTS_PER_GROUP = 8
N_EXPERTS = N_GROUPS * EXPERTS_PER_GROUP
TOP_K = 2
D_EXPERT = 512
EXPERT_BLOCK = 256
EPS = 1e-6

A_QK_W = A_HEADS * 2 * A_HD
A_V_W = A_HEADS * A_VD
R_QK_W = R_HEADS * R_DK
R_V_W = R_HEADS * R_DV
IN_WIDTHS = (A_QK_W, A_QK_W, A_V_W, R_QK_W, R_QK_W, R_V_W, R_V_W, D_MODEL, D_MODEL)
D_IN = sum(IN_WIDTHS)

kernel_name = 'hybrid_diffattn_retention_hmoe'


def rmsnorm(x, g=None):
    xf = x.astype(jnp.float32)
    y = xf * lax.rsqrt(jnp.mean(xf * xf, axis=-1, keepdims=True) + EPS)
    if g is not None:
        y = y * g.astype(jnp.float32)
    return y.astype(x.dtype)


def partial_rotary(x, pos, rot_dim, theta):
    inv = theta ** (-jnp.arange(0, rot_dim, 2, dtype=jnp.float32) / rot_dim)
    ang = pos[:, None] * inv[None, :]
    cos = jnp.cos(ang)
    sin = jnp.sin(ang)
    xf = x.astype(jnp.float32)
    half = rot_dim // 2
    x1 = xf[..., :half]
    x2 = xf[..., half:rot_dim]
    out = jnp.concatenate([x1 * cos - x2 * sin, x2 * cos + x1 * sin, xf[..., rot_dim:]], axis=-1)
    return out.astype(x.dtype)


def diff_attention(q, k, v, lam, subln_g, lam_init):
    B, S = q.shape[0], q.shape[1]
    pos = jnp.arange(S, dtype=jnp.float32)
    q = partial_rotary(jnp.transpose(q, (0, 2, 3, 1, 4)), pos, A_ROT, A_THETA)
    k = partial_rotary(jnp.transpose(k, (0, 2, 3, 1, 4)), pos, A_ROT, A_THETA)
    v = jnp.transpose(v, (0, 2, 1, 3))
    nb = S // Q_BLOCK
    qb = jnp.moveaxis(q.reshape(B, A_HEADS, 2, nb, Q_BLOCK, A_HD), 3, 0)
    scale = A_HD ** -0.5

    def one_block(qblk):
        s = jnp.einsum('bhcqd,bhckd->bhcqk', qblk, k, preferred_element_type=jnp.float32) * scale
        p = jax.nn.softmax(s, axis=-1)
        a = p[:, :, 0] - lam * p[:, :, 1]
        return jnp.einsum('bhqk,bhkv->bhqv', a.astype(v.dtype), v)

    o = lax.map(one_block, qb)
    o = jnp.transpose(o, (1, 0, 3, 2, 4)).reshape(B, S, A_HEADS, A_VD)
    o = rmsnorm(o, subln_g) * (1.0 - lam_init)
    return o.reshape(B, S, A_V_W)


def _decay_scan(u, decay, reverse):
    def step(state, u_i):
        return decay * state + u_i, state
    _, states = lax.scan(step, jnp.zeros_like(u[:, :, 0]), jnp.moveaxis(u, 2, 0), reverse=reverse)
    return jnp.moveaxis(states, 0, 2)


def retention(q, k, v, decay_f, decay_b):
    B, S = q.shape[0], q.shape[1]
    C = R_CHUNK
    n = S // C
    f32 = jnp.float32
    pos = jnp.arange(S, dtype=f32)
    qf = partial_rotary(jnp.transpose(q, (0, 2, 1, 3)).astype(f32), pos, R_DK, R_THETA)
    kf = partial_rotary(jnp.transpose(k, (0, 2, 1, 3)).astype(f32), pos, R_DK, R_THETA) * (R_DK ** -0.5)
    vf = jnp.transpose(v, (0, 2, 1, 3)).astype(f32)
    qc = qf.reshape(B, R_HEADS, n, C, R_DK)
    kc = kf.reshape(B, R_HEADS, n, C, R_DK)
    vc = vf.reshape(B, R_HEADS, n, C, R_DV)
    lgf = -jnp.exp(decay_f.astype(f32))[:, None]
    lgb = -jnp.exp(decay_b.astype(f32))[:, None]
    idx = jnp.arange(C, dtype=f32)
    rel = idx[:, None] - idx[None, :]
    mask = (jnp.where(rel >= 0, jnp.exp(lgf[:, :, None] * jnp.maximum(rel, 0.0)), 0.0)
            + jnp.where(rel < 0, jnp.exp(lgb[:, :, None] * jnp.maximum(-rel, 0.0)), 0.0))
    s = jnp.einsum('bhncd,bhnmd->bhncm', qc, kc) * mask[None, :, None]
    y = jnp.einsum('bhncm,bhnmv->bhncv', s, vc)
    uf = jnp.einsum('bhncd,bhncv->bhndv', kc * jnp.exp(lgf * (C - 1 - idx))[None, :, None, :, None], vc)
    sf = _decay_scan(uf, jnp.exp(lgf * C)[:, :, None], reverse=False)
    y = y + jnp.einsum('bhncd,bhndv->bhncv', qc * jnp.exp(lgf * (idx + 1))[None, :, None, :, None], sf)
    ub = jnp.einsum('bhncd,bhncv->bhndv', kc * jnp.exp(lgb * idx)[None, :, None, :, None], vc)
    sb = _decay_scan(ub, jnp.exp(lgb * C)[:, :, None], reverse=True)
    y = y + jnp.einsum('bhncd,bhndv->bhncv', qc * jnp.exp(lgb * (C - idx))[None, :, None, :, None], sb)
    y = rmsnorm(y.reshape(B, R_HEADS, S, R_DV))
    return jnp.transpose(y, (0, 2, 1, 3)).reshape(B, S, R_V_W).astype(v.dtype)


def token_mixer(xn, w_in, lam_q1, lam_k1, lam_q2, lam_k2, subln_g, ret_decay_f, ret_decay_b,
                w_pa, w_pb, w_out, lam_init):
    B, S, _ = xn.shape
    f32 = jnp.float32
    z = jnp.einsum('bsd,de->bse', xn, w_in)
    qa, ka, va, qr, kr, vr, gr, ga, gb = jnp.split(z, np.cumsum(IN_WIDTHS)[:-1].tolist(), axis=-1)
    lam = (jnp.exp(jnp.sum(lam_q1.astype(f32) * lam_k1.astype(f32)))
           - jnp.exp(jnp.sum(lam_q2.astype(f32) * lam_k2.astype(f32))) + lam_init)
    oa = diff_attention(qa.reshape(B, S, A_HEADS, 2, A_HD), ka.reshape(B, S, A_HEADS, 2, A_HD),
                        va.reshape(B, S, A_HEADS, A_VD), lam, subln_g, lam_init)
    ob = retention(qr.reshape(B, S, R_HEADS, R_DK), kr.reshape(B, S, R_HEADS, R_DK),
                   vr.reshape(B, S, R_HEADS, R_DV), ret_decay_f, ret_decay_b)
    ob = jax.nn.silu(gr) * ob
    merged = jax.nn.sigmoid(ga) * (oa @ w_pa) + jax.nn.sigmoid(gb) * (ob @ w_pb)
    return merged @ w_out


def hier_moe(xn, w_rg, b_rg, w_re, b_re, w1, w3, w2):
    B, S, D = xn.shape
    f32 = jnp.float32
    xt = xn.reshape(-1, D)
    T = xt.shape[0]
    lg = jnp.einsum('td,dg->tg', xt, w_rg, preferred_element_type=f32) + b_rg.astype(f32)
    pg_top, g_sel = lax.top_k(jax.nn.softmax(lg, axis=-1), 1)
    le = jnp.einsum('td,gde->tge', xt, w_re, preferred_element_type=f32) + b_re.astype(f32)
    le = jnp.take_along_axis(le, g_sel[:, :, None], axis=1)[:, 0]
    pe_top, e_sel = lax.top_k(jax.nn.softmax(le, axis=-1), TOP_K)
    gate = pg_top * pe_top / jnp.sum(pe_top, axis=-1, keepdims=True)
    expert_id = (g_sel * EXPERTS_PER_GROUP + e_sel).reshape(-1)
    A = T * TOP_K
    tok = jnp.repeat(jnp.arange(T, dtype=jnp.int32), TOP_K)
    g_flat = gate.reshape(-1)
    order = jnp.argsort(expert_id)
    e_s = expert_id[order]
    tok_s = tok[order]
    g_s = g_flat[order]
    counts = jnp.bincount(expert_id, length=N_EXPERTS)
    start = jnp.cumsum(counts) - counts
    padded = (counts + EXPERT_BLOCK - 1) // EXPERT_BLOCK * EXPERT_BLOCK
    pend = jnp.cumsum(padded)
    pstart = pend - padded
    dest = pstart[e_s] + (jnp.arange(A, dtype=jnp.int32) - start[e_s])
    n_blocks = -(-A // EXPERT_BLOCK) + N_EXPERTS
    P = n_blocks * EXPERT_BLOCK
    slot_tok = jnp.zeros((P,), jnp.int32).at[dest].set(tok_s)
    slot_gate = jnp.zeros((P,), f32).at[dest].set(g_s)
    block_start = jnp.arange(n_blocks, dtype=pend.dtype) * EXPERT_BLOCK
    block_expert = jnp.minimum(jnp.searchsorted(pend, block_start, side='right'), N_EXPERTS - 1)

    def run_block(args):
        toks, gts, e = args
        xb = xt[toks]
        h = jax.nn.silu(xb @ w1[e]) * (xb @ w3[e])
        return (h @ w2[e]) * gts[:, None].astype(xt.dtype)

    yb = lax.map(run_block, (slot_tok.reshape(n_blocks, EXPERT_BLOCK),
                             slot_gate.reshape(n_blocks, EXPERT_BLOCK), block_expert))
    out = jnp.zeros_like(xt).at[slot_tok].add(yb.reshape(P, D))
    return out.reshape(B, S, D)


def trunk(x, attn_norm_g, w_in, lam_q1, lam_k1, lam_q2, lam_k2, subln_g, ret_decay_f, ret_decay_b,
          w_pa, w_pb, w_out, ffn_norm_g, w_rg, b_rg, w_re, b_re, w1, w3, w2, final_norm_g):
    for l in range(DEPTH):
        lam_init = 0.8 - 0.6 * math.exp(-0.3 * l)
        x = x + token_mixer(rmsnorm(x, attn_norm_g[l]), w_in[l], lam_q1[l], lam_k1[l], lam_q2[l],
                            lam_k2[l], subln_g[l], ret_decay_f[l], ret_decay_b[l], w_pa[l], w_pb[l],
                            w_out[l], lam_init)
        x = x + hier_moe(rmsnorm(x, ffn_norm_g[l]), w_rg[l], b_rg[l], w_re[l], b_re[l],
                         w1[l], w3[l], w2[l])
    return rmsnorm(x, final_norm_g)


def setup_inputs(seed: int = 0) -> dict:
    key = jax.random.key(seed)
    ks = jax.random.split(key, 24)
    f32 = jnp.float32

    def nrm(k, shape, scale):
        return jax.random.normal(k, shape, f32) * scale

    base_decay = jnp.log(-jnp.log1p(-(2.0 ** (-5.0 - jnp.arange(R_HEADS, dtype=f32)))))
    return {
        'x_prompt': nrm(ks[0], (BATCH, SEQ, D_MODEL), 1.0),
        'x_sample': nrm(ks[1], (DEC_BATCH, DEC_SEQ, D_MODEL), 1.0),
        'attn_norm_g': 1.0 + nrm(ks[2], (DEPTH, D_MODEL), 0.02),
        'w_in': nrm(ks[3], (DEPTH, D_MODEL, D_IN), D_MODEL ** -0.5),
        'lam_q1': nrm(ks[4], (DEPTH, A_HD), 0.1),
        'lam_k1': nrm(ks[5], (DEPTH, A_HD), 0.1),
        'lam_q2': nrm(ks[6], (DEPTH, A_HD), 0.1),
        'lam_k2': nrm(ks[7], (DEPTH, A_HD), 0.1),
        'subln_g': 1.0 + nrm(ks[8], (DEPTH, A_VD), 0.02),
        'ret_decay_f': base_decay[None, :] + nrm(ks[9], (DEPTH, R_HEADS), 0.05),
        'ret_decay_b': base_decay[None, :] + nrm(ks[10], (DEPTH, R_HEADS), 0.05),
        'w_pa': nrm(ks[11], (DEPTH, A_V_W, D_MODEL), A_V_W ** -0.5),
        'w_pb': nrm(ks[12], (DEPTH, R_V_W, D_MODEL), R_V_W ** -0.5),
        'w_out': nrm(ks[13], (DEPTH, D_MODEL, D_MODEL), D_MODEL ** -0.5),
        'ffn_norm_g': 1.0 + nrm(ks[14], (DEPTH, D_MODEL), 0.02),
        'w_rg': nrm(ks[15], (DEPTH, D_MODEL, N_GROUPS), D_MODEL ** -0.5),
        'b_rg': nrm(ks[16], (DEPTH, N_GROUPS), 0.01),
        'w_re': nrm(ks[17], (DEPTH, N_GROUPS, D_MODEL, EXPERTS_PER_GROUP), D_MODEL ** -0.5),
        'b_re': nrm(ks[18], (DEPTH, N_GROUPS, EXPERTS_PER_GROUP), 0.01),
        'w1': nrm(ks[19], (DEPTH, N_EXPERTS, D_MODEL, D_EXPERT), D_MODEL ** -0.5),
        'w3': nrm(ks[20], (DEPTH, N_EXPERTS, D_MODEL, D_EXPERT), D_MODEL ** -0.5),
        'w2': nrm(ks[21], (DEPTH, N_EXPERTS, D_EXPERT, D_MODEL), D_EXPERT ** -0.5),
        'final_norm_g': 1.0 + nrm(ks[22], (D_MODEL,), 0.02),
    }


def reference(x_prompt, x_sample, attn_norm_g, w_in, lam_q1, lam_k1, lam_q2, lam_k2, subln_g,
              ret_decay_f, ret_decay_b, w_pa, w_pb, w_out, ffn_norm_g, w_rg, b_rg, w_re, b_re,
              w1, w3, w2, final_norm_g):
    y_prompt = trunk(x_prompt, attn_norm_g, w_in, lam_q1, lam_k1, lam_q2, lam_k2, subln_g,
                     ret_decay_f, ret_decay_b, w_pa, w_pb, w_out, ffn_norm_g, w_rg, b_rg, w_re,
                     b_re, w1, w3, w2, final_norm_g)
    y_sample = trunk(x_sample, attn_norm_g, w_in, lam_q1, lam_k1, lam_q2, lam_k2, subln_g,
                     ret_decay_f, ret_decay_b, w_pa, w_pb, w_out, ffn_norm_g, w_rg, b_rg, w_re,
                     b_re, w1, w3, w2, final_norm_g)
    return (y_prompt, y_sample)
```

```python
import functools
import math

import numpy as np
import jax
import jax.numpy as jnp
from jax import lax
from jax.experimental import pallas as pl
from jax.experimental.pallas import tpu as pltpu

f32 = jnp.float32
bf16 = jnp.bfloat16

D_MODEL = 1024
A_HEADS = 4
A_HD = 64
A_VD = 128
A_ROT = 16
A_THETA = 500000.0
R_HEADS = 4
R_DK = 64
R_DV = 128
R_CHUNK = 128
R_THETA = 10000.0
N_GROUPS = 4
EXPERTS_PER_GROUP = 8
N_EXPERTS = 32
D_EXPERT = 512
EPS = 1e-6
LAM_INIT = 0.8 - 0.6 * math.exp(-0.3 * 0)

A_QK_W = 512
A_V_W = 512
R_QK_W = 256
R_V_W = 512
D_IN = 5120
OFF_QA, OFF_KA, OFF_VA, OFF_QR, OFF_KR, OFF_VR, OFF_GR, OFF_GA, OFF_GB = (
    0, 512, 1024, 1536, 1792, 2048, 2560, 3072, 4096)

LANES = 128
N_PAIRS = 28
N_CLASSES = N_GROUPS * N_PAIRS
ROW_W = D_MODEL + LANES
MOE_BLK = 128
VMEM_LIMIT = 56 * 1024 * 1024


def _cparams(sem, **kw):
    return pltpu.CompilerParams(dimension_semantics=sem, vmem_limit_bytes=VMEM_LIMIT, **kw)


def _lam_kernel(q1_ref, k1_ref, q2_ref, k2_ref, o_ref):
    a = jnp.sum(q1_ref[...] * k1_ref[...], axis=-1, keepdims=True)
    b = jnp.sum(q2_ref[...] * k2_ref[...], axis=-1, keepdims=True)
    lam = jnp.exp(a) - jnp.exp(b) + LAM_INIT
    o_ref[...] = jnp.broadcast_to(lam, o_ref.shape)


def _lam_call(q1, k1, q2, k2):
    return pl.pallas_call(
        _lam_kernel,
        out_shape=jax.ShapeDtypeStruct((8, LANES), f32),
        name="lam",
    )(q1, k1, q2, k2)


def _rotary(z, c, s1, s2, shift):
    outs = []
    for j in range(z.shape[1] // LANES):
        zz = z[:, j * LANES:(j + 1) * LANES]
        outs.append(zz * c + pltpu.roll(zz, LANES - shift, 1) * s1 + pltpu.roll(zz, shift, 1) * s2)
    return jnp.concatenate(outs, axis=1)


def _inproj_kernel(x_ref, g_ref, w_ref, ca_ref, sa1_ref, sa2_ref, cr_ref, sr1_ref, sr2_ref,
                   qa_ref, ka_ref, va_ref, qr_ref, kr_ref, vr_ref, gr_ref, ga_ref, gb_ref):
    x = x_ref[...]
    xn = x * lax.rsqrt(jnp.mean(x * x, axis=-1, keepdims=True) + EPS)
    xn = (xn * g_ref[...]).astype(bf16)

    def proj(off, width):
        return jnp.dot(xn, w_ref[:, off:off + width], preferred_element_type=f32)

    ca, sa1, sa2 = ca_ref[...], sa1_ref[...], sa2_ref[...]
    cr, sr1, sr2 = cr_ref[...], sr1_ref[...], sr2_ref[...]
    qa_ref[...] = (_rotary(proj(OFF_QA, A_QK_W), ca, sa1, sa2, A_ROT // 2) * (A_HD ** -0.5)).astype(bf16)
    ka_ref[...] = _rotary(proj(OFF_KA, A_QK_W), ca, sa1, sa2, A_ROT // 2).astype(bf16)
    va_ref[...] = proj(OFF_VA, A_V_W).astype(bf16)
    qr_ref[...] = _rotary(proj(OFF_QR, R_QK_W), cr, sr1, sr2, R_DK // 2)
    kr_ref[...] = _rotary(proj(OFF_KR, R_QK_W), cr, sr1, sr2, R_DK // 2) * (R_DK ** -0.5)
    vr_ref[...] = proj(OFF_VR, R_V_W).astype(bf16)
    gr = proj(OFF_GR, R_V_W)
    gr_ref[...] = (gr * jax.nn.sigmoid(gr)).astype(bf16)
    ga_ref[...] = jax.nn.sigmoid(proj(OFF_GA, D_MODEL)).astype(bf16)
    gb_ref[...] = jax.nn.sigmoid(proj(OFF_GB, D_MODEL)).astype(bf16)


def _rot_tables(S, rot_dim, theta, head_dim):
    inv = theta ** (-jnp.arange(0, rot_dim, 2, dtype=f32) / rot_dim)
    pos = jnp.arange(S, dtype=f32)
    ang = pos[:, None] * inv[None, :]
    cos, sin = jnp.cos(ang), jnp.sin(ang)
    half = rot_dim // 2
    pad = head_dim - rot_dim
    c = jnp.concatenate([cos, cos, jnp.ones((S, pad), f32)], axis=1)
    s1 = jnp.concatenate([-sin, jnp.zeros((S, half + pad), f32)], axis=1)
    s2 = jnp.concatenate([jnp.zeros((S, half), f32), sin, jnp.zeros((S, pad), f32)], axis=1)
    rep = LANES // head_dim
    return tuple(jnp.tile(t, (1, rep)) for t in (c, s1, s2))


def _inproj_call(x2d, S, g, w_in_bf, tabs_a, tabs_r, tm):
    T = x2d.shape[0]
    nps = S // tm
    row = lambda i: (i, 0)
    posmap = lambda i: (i % nps, 0)
    const = lambda i: (0, 0)
    tab_spec = pl.BlockSpec((tm, LANES), posmap)
    outs = [
        (A_QK_W, bf16), (A_QK_W, bf16), (A_V_W, bf16),
        (R_QK_W, f32), (R_QK_W, f32), (R_V_W, bf16), (R_V_W, bf16),
        (D_MODEL, bf16), (D_MODEL, bf16),
    ]
    return pl.pallas_call(
        _inproj_kernel,
        grid=(T // tm,),
        in_specs=[pl.BlockSpec((tm, D_MODEL), row),
                  pl.BlockSpec((1, D_MODEL), const),
                  pl.BlockSpec((D_MODEL, D_IN), const)] + [tab_spec] * 6,
        out_specs=[pl.BlockSpec((tm, w), row) for w, _ in outs],
        out_shape=[jax.ShapeDtypeStruct((T, w), dt) for w, dt in outs],
        compiler_params=_cparams(("arbitrary",)),
        name="inproj",
    )(x2d, g, w_in_bf, *tabs_a, *tabs_r)


def _attn_kernel(lam_ref, q_ref, k_ref, v_ref, g_ref, o_ref, vaug_ref, *, tq):
    qi = pl.program_id(2)
    S = v_ref.shape[0]

    @pl.when(qi == 0)
    def _():
        lane = lax.broadcasted_iota(jnp.int32, (S, LANES), 1)
        vaug_ref[:, :LANES] = v_ref[...]
        vaug_ref[:, LANES:] = jnp.where(lane == 0, 1.0, 0.0).astype(bf16)

    q = q_ref[...]
    lane = lax.broadcasted_iota(jnp.int32, q.shape, 1)
    zero = jnp.zeros_like(q)
    qs = jnp.concatenate([jnp.where(lane < A_HD, q, zero), jnp.where(lane >= A_HD, q, zero)], axis=0)
    s = lax.dot_general(qs, k_ref[...], (((1,), (1,)), ((), ())), preferred_element_type=f32)
    m = jnp.max(s, axis=-1, keepdims=True)
    e = jnp.exp(s - m).astype(bf16)
    oa = jnp.dot(e, vaug_ref[...], preferred_element_type=f32)
    o = oa[:, :LANES] / oa[:, LANES:LANES + 1]
    a = o[:tq] - lam_ref[0] * o[tq:]
    y = a * lax.rsqrt(jnp.mean(a * a, axis=-1, keepdims=True) + EPS)
    o_ref[...] = ((y * g_ref[...]) * (1.0 - LAM_INIT)).astype(o_ref.dtype)


def _attn_call(lam, qa, ka, va, subln_g, B, S, tq):
    T = qa.shape[0]
    nq = S // tq
    return pl.pallas_call(
        functools.partial(_attn_kernel, tq=tq),
        grid=(B, A_HEADS, nq),
        in_specs=[pl.BlockSpec(memory_space=pltpu.SMEM),
                  pl.BlockSpec((tq, LANES), lambda b, h, i: (b * nq + i, h)),
                  pl.BlockSpec((S, LANES), lambda b, h, i: (b, h)),
                  pl.BlockSpec((S, LANES), lambda b, h, i: (b, h)),
                  pl.BlockSpec((1, LANES), lambda b, h, i: (0, 0))],
        out_specs=pl.BlockSpec((tq, LANES), lambda b, h, i: (b * nq + i, h)),
        out_shape=jax.ShapeDtypeStruct((T, A_V_W), bf16),
        scratch_shapes=[pltpu.VMEM((S, 2 * LANES), bf16)],
        compiler_params=_cparams(("arbitrary", "arbitrary", "arbitrary")),
        name="attn",
    )(lam, qa, ka, va, subln_g)


def _ret_kernel(df_ref, db_ref, q_ref, k_ref, v_ref, gr_ref, o_ref, uf_ref, ub_ref):
    p = pl.program_id(1)
    C = R_CHUNK
    S = q_ref.shape[0]
    n = S // C
    lane = lax.broadcasted_iota(jnp.int32, (C, LANES), 1)
    rowi = lax.broadcasted_iota(jnp.int32, (C, LANES), 0)
    first = lane < R_DK
    ones = jnp.ones((C, LANES), f32)
    lgf_a, lgf_b = -jnp.exp(ones * df_ref[2 * p]), -jnp.exp(ones * df_ref[2 * p + 1])
    lgb_a, lgb_b = -jnp.exp(ones * db_ref[2 * p]), -jnp.exp(ones * db_ref[2 * p + 1])
    lgf = jnp.where(first, lgf_a, lgf_b)
    lgb = jnp.where(first, lgb_a, lgb_b)
    idx = rowi.astype(f32)
    k_f = jnp.exp(lgf * (C - 1 - idx))
    k_b = jnp.exp(lgb * idx)
    q_f = jnp.exp(lgf * (idx + 1))
    q_b = jnp.exp(lgb * (C - idx))
    top = rowi < R_DK
    g_f = jnp.where(top, jnp.exp(lgf_a * C), jnp.exp(lgf_b * C))
    g_b = jnp.where(top, jnp.exp(lgb_a * C), jnp.exp(lgb_b * C))
    g_f = jnp.concatenate([g_f, g_f], axis=1)
    g_b = jnp.concatenate([g_b, g_b], axis=1)
    rel = (rowi - lane).astype(f32)

    def decay_mask(lf, lb):
        return jnp.where(rel >= 0, jnp.exp(lf * jnp.maximum(rel, 0.0)), jnp.exp(lb * jnp.maximum(-rel, 0.0)))

    mask = jnp.concatenate([decay_mask(lgf_a, lgb_a), decay_mask(lgf_b, lgb_b)], axis=0)

    def summaries(c, carry):
        r0 = pl.multiple_of(c * C, C)
        kc = k_ref[pl.ds(r0, C), :]
        vc = v_ref[pl.ds(r0, C), :]
        kt = jnp.concatenate([(kc * k_f).T, (kc * k_b).T], axis=0).astype(bf16)
        uu = jnp.dot(kt, vc, preferred_element_type=f32)
        uf_ref[c] = uu[:LANES]
        ub_ref[c] = uu[LANES:]
        return carry

    lax.fori_loop(0, n, summaries, 0)

    def scan_f(c, st):
        u = uf_ref[c]
        uf_ref[c] = st
        return g_f * st + u

    lax.fori_loop(0, n, scan_f, jnp.zeros((LANES, 2 * LANES), f32))

    def scan_b(j, st):
        c = n - 1 - j
        u = ub_ref[c]
        ub_ref[c] = st
        return g_b * st + u

    lax.fori_loop(0, n, scan_b, jnp.zeros((LANES, 2 * LANES), f32))

    def outputs(c, carry):
        r0 = pl.multiple_of(c * C, C)
        qc = q_ref[pl.ds(r0, C), :]
        kc = k_ref[pl.ds(r0, C), :]
        vc = v_ref[pl.ds(r0, C), :]
        zero = jnp.zeros_like(qc)
        q2 = jnp.concatenate([jnp.where(first, qc, zero), jnp.where(first, zero, qc)], axis=0)
        s = lax.dot_general(q2.astype(bf16), kc.astype(bf16), (((1,), (1,)), ((), ())),
                            preferred_element_type=f32)
        qf2 = jnp.concatenate([q_f, q_f], axis=0)
        qb2 = jnp.concatenate([q_b, q_b], axis=0)
        lhs = jnp.concatenate([s * mask, q2 * qf2, q2 * qb2], axis=1).astype(bf16)
        rhs = jnp.concatenate([vc, uf_ref[c].astype(bf16), ub_ref[c].astype(bf16)], axis=0)
        yy = jnp.dot(lhs, rhs, preferred_element_type=f32)
        ya = yy[:C, :LANES]
        yb = yy[C:, LANES:]
        ya = ya * lax.rsqrt(jnp.mean(ya * ya, axis=-1, keepdims=True) + EPS)
        yb = yb * lax.rsqrt(jnp.mean(yb * yb, axis=-1, keepdims=True) + EPS)
        y = jnp.concatenate([ya, yb], axis=1)
        o_ref[pl.ds(r0, C), :] = (gr_ref[pl.ds(r0, C), :].astype(f32) * y).astype(o_ref.dtype)
        return carry

    lax.fori_loop(0, n, outputs, 0)


def _ret_call(decay_f, decay_b, qr, kr, vr, gr, B, S):
    T = qr.shape[0]
    n = S // R_CHUNK
    blk = lambda w: pl.BlockSpec((S, w), lambda b, p: (b, p))
    return pl.pallas_call(
        _ret_kernel,
        grid=(B, R_HEADS // 2),
        in_specs=[pl.BlockSpec(memory_space=pltpu.SMEM), pl.BlockSpec(memory_space=pltpu.SMEM),
                  blk(LANES), blk(LANES), blk(2 * LANES), blk(2 * LANES)],
        out_specs=blk(2 * LANES),
        out_shape=jax.ShapeDtypeStruct((T, R_V_W), bf16),
        scratch_shapes=[pltpu.VMEM((n, LANES, 2 * LANES), f32), pltpu.VMEM((n, LANES, 2 * LANES), f32)],
        compiler_params=_cparams(("arbitrary", "arbitrary")),
        name="ret",
    )(decay_f, decay_b, qr, kr, vr, gr)


def _outproj_kernel(oa_ref, ob_ref, ga_ref, gb_ref, x_ref, wpa_ref, wpb_ref, wo_ref, g_ref,
                    wrh_ref, wrl_ref, br_ref,
                    x1_ref, row_ref, meta_ref, cnt_ref, tri_ref, carry_ref, *, tm):
    i = pl.program_id(0)

    @pl.when(i == 0)
    def _():
        r = lax.broadcasted_iota(jnp.int32, (tm, tm), 0)
        c = lax.broadcasted_iota(jnp.int32, (tm, tm), 1)
        tri_ref[...] = jnp.where(c < r, 1.0, 0.0).astype(bf16)
        carry_ref[...] = jnp.zeros_like(carry_ref)

    pa = jnp.dot(oa_ref[...], wpa_ref[...], preferred_element_type=f32)
    pb = jnp.dot(ob_ref[...], wpb_ref[...], preferred_element_type=f32)
    merged = ga_ref[...].astype(f32) * pa + gb_ref[...].astype(f32) * pb
    x1 = x_ref[...] + jnp.dot(merged.astype(bf16), wo_ref[...], preferred_element_type=f32)
    x1_ref[...] = x1
    xn = x1 * lax.rsqrt(jnp.mean(x1 * x1, axis=-1, keepdims=True) + EPS)
    xn = xn * g_ref[...]
    row_ref[:, :D_MODEL] = xn

    xh = xn.astype(bf16)
    xl = (xn - xh.astype(f32)).astype(bf16)
    wh = wrh_ref[...]
    logits = (jnp.dot(xh, wh, preferred_element_type=f32) + jnp.dot(xl, wh, preferred_element_type=f32)
              + jnp.dot(xh, wrl_ref[...], preferred_element_type=f32) + br_ref[...])
    lane = lax.broadcasted_iota(jnp.int32, (tm, LANES), 1)
    ninf = jnp.full((tm, LANES), -jnp.inf, f32)
    lane_f = lane.astype(f32)
    big = jnp.full((tm, LANES), float(LANES - 1), f32)

    def first_lane(hit):
        return jnp.min(jnp.where(hit, lane_f, big), axis=-1, keepdims=True).astype(jnp.int32)

    isg = lane < N_GROUPS
    lg = jnp.where(isg, logits, ninf)
    mg = jnp.max(lg, axis=-1, keepdims=True)
    gsel = first_lane(lg == mg)
    pg_top = 1.0 / jnp.sum(jnp.where(isg, jnp.exp(logits - mg), 0.0), axis=-1, keepdims=True)
    lo = N_GROUPS + EXPERTS_PER_GROUP * gsel
    le = jnp.where((lane >= lo) & (lane < lo + EXPERTS_PER_GROUP), logits, ninf)
    m1 = jnp.max(le, axis=-1, keepdims=True)
    i1 = first_lane(le == m1)
    le2 = jnp.where(lane == i1, ninf, le)
    m2 = jnp.max(le2, axis=-1, keepdims=True)
    i2 = first_lane(le2 == m2)
    t2 = jnp.exp(m2 - m1)
    gate1 = pg_top / (1.0 + t2)
    gate2 = pg_top * t2 / (1.0 + t2)
    e1 = i1 - lo
    e2 = i2 - lo
    ea = jnp.minimum(e1, e2)
    eb = jnp.maximum(e1, e2)
    gate_a = jnp.where(e1 < e2, gate1, gate2)
    gate_b = jnp.where(e1 < e2, gate2, gate1)
    pair = lax.shift_right_logical(ea * (2 * EXPERTS_PER_GROUP - 1 - ea), 1) + (eb - ea - 1)
    cls = gsel * N_PAIRS + pair
    row_ref[:, D_MODEL:] = jnp.where(lane == 0, gate_a, jnp.where(lane == 1, gate_b, 0.0))

    hit = lane == cls
    onehot = jnp.where(hit, 1.0, 0.0)
    rank = jnp.dot(tri_ref[...], onehot.astype(bf16), preferred_element_type=f32) + carry_ref[...]
    pos = jnp.sum(jnp.where(hit, rank, 0.0), axis=-1, keepdims=True)
    carry_ref[...] = carry_ref[...] + jnp.sum(onehot, axis=0, keepdims=True)
    cnt_ref[...] = jnp.broadcast_to(carry_ref[...], cnt_ref.shape)
    meta_ref[...] = jnp.where(lane == 0, cls.astype(f32), jnp.where(lane == 1, pos, 0.0))


def _outproj_call(oa, ob, ga, gb, x2d, wpa, wpb, wo, g, wrh, wrl, br, tm):
    T = x2d.shape[0]
    row = lambda i: (i, 0)
    const = lambda i: (0, 0)
    return pl.pallas_call(
        functools.partial(_outproj_kernel, tm=tm),
        grid=(T // tm,),
        in_specs=[pl.BlockSpec((tm, A_V_W), row), pl.BlockSpec((tm, R_V_W), row),
                  pl.BlockSpec((tm, D_MODEL), row), pl.BlockSpec((tm, D_MODEL), row),
                  pl.BlockSpec((tm, D_MODEL), row),
                  pl.BlockSpec((A_V_W, D_MODEL), const), pl.BlockSpec((R_V_W, D_MODEL), const),
                  pl.BlockSpec((D_MODEL, D_MODEL), const), pl.BlockSpec((1, D_MODEL), const),
                  pl.BlockSpec((D_MODEL, LANES), const), pl.BlockSpec((D_MODEL, LANES), const),
                  pl.BlockSpec((1, LANES), const)],
        out_specs=[pl.BlockSpec((tm, D_MODEL), row), pl.BlockSpec((tm, ROW_W), row),
                   pl.BlockSpec((tm, LANES), row), pl.BlockSpec((8, LANES), const)],
        out_shape=[jax.ShapeDtypeStruct((T, D_MODEL), f32), jax.ShapeDtypeStruct((T, ROW_W), f32),
                   jax.ShapeDtypeStruct((T, LANES), f32), jax.ShapeDtypeStruct((8, LANES), f32)],
        scratch_shapes=[pltpu.VMEM((tm, tm), bf16), pltpu.VMEM((1, LANES), f32)],
        compiler_params=_cparams(("arbitrary",)),
        name="outproj",
    )(oa, ob, ga, gb, x2d, wpa, wpb, wo, g, wrh, wrl, br)


def _dispatch_kernel(dest_ref, row_ref, xs_in_ref, xs_ref, sem, *, tm):
    del xs_in_ref

    def issue(r, carry):
        d = dest_ref[0, 0, r]
        pltpu.make_async_copy(row_ref.at[pl.ds(r, 1), :], xs_ref.at[pl.ds(d, 1), :], sem).start()
        return carry

    lax.fori_loop(0, tm, issue, 0)
    pltpu.make_async_copy(row_ref, xs_ref.at[pl.ds(0, tm), :], sem).wait()


def _dispatch_call(dest, rows, xs_zero, tm):
    T = rows.shape[0]
    P = xs_zero.shape[0]
    return pl.pallas_call(
        functools.partial(_dispatch_kernel, tm=tm),
        grid=(T // tm,),
        in_specs=[pl.BlockSpec((1, 1, tm), lambda i: (i, 0, 0), memory_space=pltpu.SMEM),
                  pl.BlockSpec((tm, ROW_W), lambda i: (i, 0)),
                  pl.BlockSpec(memory_space=pl.ANY)],
        out_specs=pl.BlockSpec(memory_space=pl.ANY),
        scratch_shapes=[pltpu.SemaphoreType.DMA(())],
        out_shape=jax.ShapeDtypeStruct((P, ROW_W), f32),
        input_output_aliases={2: 0},
        compiler_params=_cparams(("arbitrary",), has_side_effects=True),
        name="dispatch",
    )(dest.reshape(T // tm, 1, tm), rows, xs_zero)


def _expert_kernel(ea_ref, eb_ref, nb_ref, xs_ref, w1a_ref, w3a_ref, w2a_ref, w1b_ref, w3b_ref, w2b_ref,
                   y_ref):
    j = pl.program_id(0)

    @pl.when(j < nb_ref[0])
    def _():
        x = xs_ref[:, :D_MODEL].astype(bf16)
        gates = xs_ref[:, D_MODEL:]
        ga = gates[:, 0:1]
        gb = gates[:, 1:2]

        def ffn(w1_ref, w3_ref, w2_ref):
            a = jnp.dot(x, w1_ref[0], preferred_element_type=f32)
            b = jnp.dot(x, w3_ref[0], preferred_element_type=f32)
            h = (a * jax.nn.sigmoid(a)) * b
            return jnp.dot(h.astype(bf16), w2_ref[0], preferred_element_type=f32)

        y_ref[...] = ffn(w1a_ref, w3a_ref, w2a_ref) * ga + ffn(w1b_ref, w3b_ref, w2b_ref) * gb

    @pl.when(j >= nb_ref[0])
    def _():
        y_ref[...] = jnp.zeros_like(y_ref)


def _expert_call(blk_ea, blk_eb, n_used, xs, w1, w3, w2):
    P = xs.shape[0]
    nb = P // MOE_BLK
    wa = lambda j, ea, eb, n: (ea[j], 0, 0)
    wb = lambda j, ea, eb, n: (eb[j], 0, 0)
    up = (1, D_MODEL, D_EXPERT)
    down = (1, D_EXPERT, D_MODEL)
    return pl.pallas_call(
        _expert_kernel,
        grid_spec=pltpu.PrefetchScalarGridSpec(
            num_scalar_prefetch=3,
            grid=(nb,),
            in_specs=[pl.BlockSpec((MOE_BLK, ROW_W), lambda j, ea, eb, n: (jnp.minimum(j, n[0] - 1), 0)),
                      pl.BlockSpec(up, wa), pl.BlockSpec(up, wa), pl.BlockSpec(down, wa),
                      pl.BlockSpec(up, wb), pl.BlockSpec(up, wb), pl.BlockSpec(down, wb)],
            out_specs=pl.BlockSpec((MOE_BLK, D_MODEL), lambda j, ea, eb, n: (j, 0)),
        ),
        out_shape=jax.ShapeDtypeStruct((P, D_MODEL), f32),
        compiler_params=_cparams(("arbitrary",)),
        name="experts",
    )(blk_ea, blk_eb, n_used, xs, w1, w3, w2, w1, w3, w2)


def _combine_kernel(dest_ref, x1_ref, g_ref, y_ref, o_ref, buf_ref, sem, *, tm):
    def issue(r, carry):
        d = dest_ref[0, 0, r]
        pltpu.make_async_copy(y_ref.at[pl.ds(d, 1), :], buf_ref.at[pl.ds(r, 1), :], sem).start()
        return carry

    lax.fori_loop(0, tm, issue, 0)
    pltpu.make_async_copy(y_ref.at[pl.ds(0, tm), :], buf_ref, sem).wait()
    x = x1_ref[...] + buf_ref[...]
    y = x * lax.rsqrt(jnp.mean(x * x, axis=-1, keepdims=True) + EPS)
    o_ref[...] = y * g_ref[...]


def _combine_call(dest, x1, g, y, tm):
    T = x1.shape[0]
    return pl.pallas_call(
        functools.partial(_combine_kernel, tm=tm),
        grid=(T // tm,),
        in_specs=[pl.BlockSpec((1, 1, tm), lambda i: (i, 0, 0), memory_space=pltpu.SMEM),
                  pl.BlockSpec((tm, D_MODEL), lambda i: (i, 0)),
                  pl.BlockSpec((1, D_MODEL), lambda i: (0, 0)),
                  pl.BlockSpec(memory_space=pl.ANY)],
        out_specs=pl.BlockSpec((tm, D_MODEL), lambda i: (i, 0)),
        scratch_shapes=[pltpu.VMEM((tm, D_MODEL), f32), pltpu.SemaphoreType.DMA(())],
        out_shape=jax.ShapeDtypeStruct((T, D_MODEL), f32),
        compiler_params=_cparams(("arbitrary",)),
        name="combine",
    )(dest.reshape(T // tm, 1, tm), x1, g, y)


def _pair_table():
    pa, pb = [], []
    for a in range(EXPERTS_PER_GROUP):
        for b in range(a + 1, EXPERTS_PER_GROUP):
            pa.append(a)
            pb.append(b)
    return np.asarray(pa, np.int32), np.asarray(pb, np.int32)


def _route_plan(meta, counts, T):
    nb = T // MOE_BLK + N_CLASSES
    cnt = counts[0, :N_CLASSES].astype(jnp.int32)
    nblk = (cnt + MOE_BLK - 1) // MOE_BLK
    cum = jnp.cumsum(nblk)
    pstart = (cum - nblk) * MOE_BLK
    cls = meta[:, 0].astype(jnp.int32)
    pos = meta[:, 1].astype(jnp.int32)
    dest = pstart[cls] + pos
    blk = jnp.minimum(jnp.arange(nb, dtype=jnp.int32), cum[-1] - 1)
    blk_cls = jnp.minimum(jnp.searchsorted(cum, blk, side='right'), N_CLASSES - 1).astype(jnp.int32)
    pa, pb = _pair_table()
    grp = blk_cls // N_PAIRS
    pr = blk_cls % N_PAIRS
    blk_ea = grp * EXPERTS_PER_GROUP + jnp.asarray(pa)[pr]
    blk_eb = grp * EXPERTS_PER_GROUP + jnp.asarray(pb)[pr]
    n_used = cum[-1:].astype(jnp.int32)
    return dest, blk_ea.astype(jnp.int32), blk_eb.astype(jnp.int32), n_used, nb * MOE_BLK


def _trunk(x, lam, wts, tabs, tm=512, tq=128):
    B, S, _ = x.shape
    T = B * S
    x2d = x.reshape(T, D_MODEL)
    tabs_a, tabs_r = tabs
    qa, ka, va, qr, kr, vr, gr, ga, gb = _inproj_call(
        x2d, S, wts['attn_norm_g'], wts['w_in'], [t[:S] for t in tabs_a], [t[:S] for t in tabs_r], tm)
    oa = _attn_call(lam, qa, ka, va, wts['subln_g'], B, S, tq)
    ob = _ret_call(wts['ret_decay_f'], wts['ret_decay_b'], qr, kr, vr, gr, B, S)
    x1, rows, meta, counts = _outproj_call(
        oa, ob, ga, gb, x2d, wts['w_pa'], wts['w_pb'], wts['w_out'], wts['ffn_norm_g'],
        wts['wr_hi'], wts['wr_lo'], wts['b_r'], tm)
    dest, blk_ea, blk_eb, n_used, P = _route_plan(meta, counts, T)
    xs = _dispatch_call(dest, rows, jnp.zeros((P, ROW_W), f32), 256)
    y = _expert_call(blk_ea, blk_eb, n_used, xs, wts['w1'], wts['w3'], wts['w2'])
    out = _combine_call(dest, x1, wts['final_norm_g'], y, 256)
    return out.reshape(B, S, D_MODEL)


def kernel(x_prompt, x_sample, attn_norm_g, w_in, lam_q1, lam_k1, lam_q2, lam_k2, subln_g,
           ret_decay_f, ret_decay_b, w_pa, w_pb, w_out, ffn_norm_g, w_rg, b_rg, w_re, b_re,
           w1, w3, w2, final_norm_g):
    wr = jnp.concatenate([w_rg[0], jnp.transpose(w_re[0], (1, 0, 2)).reshape(D_MODEL, N_EXPERTS)], axis=1)
    wr = jnp.pad(wr, ((0, 0), (0, LANES - wr.shape[1])))
    wr_hi = wr.astype(bf16)
    b_r = jnp.pad(jnp.concatenate([b_rg[0], b_re[0].reshape(-1)]), (0, LANES - N_GROUPS - N_EXPERTS))
    wts = dict(
        attn_norm_g=attn_norm_g[0][None, :], w_in=w_in[0].astype(bf16), subln_g=subln_g[0][None, :],
        ret_decay_f=ret_decay_f[0], ret_decay_b=ret_decay_b[0],
        w_pa=w_pa[0].astype(bf16), w_pb=w_pb[0].astype(bf16), w_out=w_out[0].astype(bf16),
        ffn_norm_g=ffn_norm_g[0][None, :], wr_hi=wr_hi, wr_lo=(wr - wr_hi.astype(f32)).astype(bf16),
        b_r=b_r[None, :], w1=w1[0].astype(bf16), w3=w3[0].astype(bf16), w2=w2[0].astype(bf16),
        final_norm_g=final_norm_g[None, :])
    lam = _lam_call(lam_q1, lam_k1, lam_q2, lam_k2)[0, :1]
    s_max = max(x_prompt.shape[1], x_sample.shape[1])
    tabs = (_rot_tables(s_max, A_ROT, A_THETA, A_HD), _rot_tables(s_max, R_DK, R_THETA, R_DK))
    return (_trunk(x_prompt, lam, wts, tabs), _trunk(x_sample, lam, wts, tabs))
```

```python
import functools
import math

import numpy as np
import jax
import jax.numpy as jnp
from jax import lax
from jax.experimental import pallas as pl
from jax.experimental.pallas import tpu as pltpu

f32 = jnp.float32
bf16 = jnp.bfloat16

D_MODEL = 1024
A_HEADS = 4
A_HD = 64
A_VD = 128
A_ROT = 16
A_THETA = 500000.0
R_HEADS = 4
R_DK = 64
R_DV = 128
R_CHUNK = 128
R_THETA = 10000.0
N_GROUPS = 4
EXPERTS_PER_GROUP = 8
N_EXPERTS = 32
D_EXPERT = 512
EPS = 1e-6
LAM_INIT = 0.8 - 0.6 * math.exp(-0.3 * 0)

A_QK_W = 512
A_V_W = 512
R_QK_W = 256
R_V_W = 512
D_IN = 5120
OFF_QA, OFF_KA, OFF_VA, OFF_QR, OFF_KR, OFF_VR, OFF_GR, OFF_GA, OFF_GB = (
    0, 512, 1024, 1536, 1792, 2048, 2560, 3072, 4096)

LANES = 128
N_PAIRS = 28
N_CLASSES = N_GROUPS * N_PAIRS
ROW_W = D_MODEL + LANES
MOE_BLK = 128
IDX_TILE = 256
RET_UNROLL = 4
VMEM_LIMIT = 56 * 1024 * 1024


def _cparams(sem, **kw):
    return pltpu.CompilerParams(dimension_semantics=sem, vmem_limit_bytes=VMEM_LIMIT, **kw)


def _lam_kernel(q1_ref, k1_ref, q2_ref, k2_ref, o_ref):
    a = jnp.sum(q1_ref[...] * k1_ref[...], axis=-1, keepdims=True)
    b = jnp.sum(q2_ref[...] * k2_ref[...], axis=-1, keepdims=True)
    lam = jnp.exp(a) - jnp.exp(b) + LAM_INIT
    o_ref[...] = jnp.broadcast_to(lam, o_ref.shape)


def _lam_call(q1, k1, q2, k2):
    return pl.pallas_call(
        _lam_kernel,
        out_shape=jax.ShapeDtypeStruct((8, LANES), f32),
        name="lam",
    )(q1, k1, q2, k2)


def _rotary(z, c, s1, s2, shift):
    outs = []
    for j in range(z.shape[1] // LANES):
        zz = z[:, j * LANES:(j + 1) * LANES]
        outs.append(zz * c + pltpu.roll(zz, LANES - shift, 1) * s1 + pltpu.roll(zz, shift, 1) * s2)
    return jnp.concatenate(outs, axis=1)


def _inproj_kernel(x_ref, g_ref, w_ref, ca_ref, sa1_ref, sa2_ref, cr_ref, sr1_ref, sr2_ref,
                   qa_ref, ka_ref, va_ref, qr_ref, kr_ref, vr_ref, gr_ref, ga_ref, gb_ref,
                   qn_ref, kn_ref, *, nps):
    x = x_ref[...]
    xn = x * lax.rsqrt(jnp.mean(x * x, axis=-1, keepdims=True) + EPS)
    xn = (xn * g_ref[...]).astype(bf16)

    def proj(off, width):
        return jnp.dot(xn, w_ref[:, off:off + width], preferred_element_type=f32)

    ca, sa1, sa2 = ca_ref[...], sa1_ref[...], sa2_ref[...]
    cr, sr1, sr2 = cr_ref[...], sr1_ref[...], sr2_ref[...]
    qa = (_rotary(proj(OFF_QA, A_QK_W), ca, sa1, sa2, A_ROT // 2) * (A_HD ** -0.5)).astype(bf16)
    ka = _rotary(proj(OFF_KA, A_QK_W), ca, sa1, sa2, A_ROT // 2).astype(bf16)
    qa_ref[...] = qa
    ka_ref[...] = ka

    seg = (lax.broadcasted_iota(jnp.int32, (A_QK_W, LANES), 0) // A_HD
           == lax.broadcasted_iota(jnp.int32, (A_QK_W, LANES), 1))
    ind = jnp.where(seg, 1.0, 0.0).astype(bf16)

    def norm2max(z):
        zf = z.astype(f32)
        n2 = jnp.dot((zf * zf).astype(bf16), ind, preferred_element_type=f32)
        return jnp.max(n2, axis=0, keepdims=True)

    qmax, kmax = norm2max(qa), norm2max(ka)
    first = pl.program_id(0) % nps == 0

    @pl.when(first)
    def _():
        qn_ref[0] = qmax
        kn_ref[0] = kmax

    @pl.when(jnp.logical_not(first))
    def _():
        qn_ref[0] = jnp.maximum(qn_ref[0], qmax)
        kn_ref[0] = jnp.maximum(kn_ref[0], kmax)

    va_ref[...] = proj(OFF_VA, A_V_W).astype(bf16)
    qr_ref[...] = _rotary(proj(OFF_QR, R_QK_W), cr, sr1, sr2, R_DK // 2)
    kr_ref[...] = _rotary(proj(OFF_KR, R_QK_W), cr, sr1, sr2, R_DK // 2) * (R_DK ** -0.5)
    vr_ref[...] = proj(OFF_VR, R_V_W).astype(bf16)
    gr = proj(OFF_GR, R_V_W)
    gr_ref[...] = (gr * jax.nn.sigmoid(gr)).astype(bf16)
    ga_ref[...] = jax.nn.sigmoid(proj(OFF_GA, D_MODEL)).astype(bf16)
    gb_ref[...] = jax.nn.sigmoid(proj(OFF_GB, D_MODEL)).astype(bf16)


def _rot_tables(S, rot_dim, theta, head_dim):
    inv = theta ** (-jnp.arange(0, rot_dim, 2, dtype=f32) / rot_dim)
    pos = jnp.arange(S, dtype=f32)
    ang = pos[:, None] * inv[None, :]
    cos, sin = jnp.cos(ang), jnp.sin(ang)
    half = rot_dim // 2
    pad = head_dim - rot_dim
    c = jnp.concatenate([cos, cos, jnp.ones((S, pad), f32)], axis=1)
    s1 = jnp.concatenate([-sin, jnp.zeros((S, half + pad), f32)], axis=1)
    s2 = jnp.concatenate([jnp.zeros((S, half), f32), sin, jnp.zeros((S, pad), f32)], axis=1)
    rep = LANES // head_dim
    return tuple(jnp.tile(t, (1, rep)) for t in (c, s1, s2))


def _inproj_call(x2d, S, g, w_in_bf, tabs_a, tabs_r, tm):
    T = x2d.shape[0]
    nps = S // tm
    row = lambda i: (i, 0)
    posmap = lambda i: (i % nps, 0)
    const = lambda i: (0, 0)
    tab_spec = pl.BlockSpec((tm, LANES), posmap)
    outs = [
        (A_QK_W, bf16), (A_QK_W, bf16), (A_V_W, bf16),
        (R_QK_W, f32), (R_QK_W, f32), (R_V_W, bf16), (R_V_W, bf16),
        (D_MODEL, bf16), (D_MODEL, bf16),
    ]
    nrm_spec = pl.BlockSpec((1, 1, LANES), lambda i: (i // nps, 0, 0))
    nrm_shape = jax.ShapeDtypeStruct((T // S, 1, LANES), f32)
    return pl.pallas_call(
        functools.partial(_inproj_kernel, nps=nps),
        grid=(T // tm,),
        in_specs=[pl.BlockSpec((tm, D_MODEL), row),
                  pl.BlockSpec((1, D_MODEL), const),
                  pl.BlockSpec((D_MODEL, D_IN), const)] + [tab_spec] * 6,
        out_specs=[pl.BlockSpec((tm, w), row) for w, _ in outs] + [nrm_spec, nrm_spec],
        out_shape=[jax.ShapeDtypeStruct((T, w), dt) for w, dt in outs] + [nrm_shape, nrm_shape],
        compiler_params=_cparams(("arbitrary",)),
        name="inproj",
    )(x2d, g, w_in_bf, *tabs_a, *tabs_r)


def _attn_kernel(kmax_ref, safe_ref, lam_ref, q_ref, k_ref, v_ref, g_ref, o_ref, vaug_ref, *, tq):
    b, h, qi = pl.program_id(0), pl.program_id(1), pl.program_id(2)
    S = v_ref.shape[0]

    @pl.when(qi == 0)
    def _():
        lane = lax.broadcasted_iota(jnp.int32, (S, LANES), 1)
        vaug_ref[:, :LANES] = v_ref[...]
        vaug_ref[:, LANES:] = jnp.where(lane == 0, 1.0, 0.0).astype(bf16)

    def stacked_q():
        q = q_ref[...]
        lane = lax.broadcasted_iota(jnp.int32, q.shape, 1)
        zero = jnp.zeros_like(q)
        return jnp.concatenate([jnp.where(lane < A_HD, q, zero), jnp.where(lane >= A_HD, q, zero)], axis=0)

    def scores(qs):
        return lax.dot_general(qs, k_ref[...], (((1,), (1,)), ((), ())), preferred_element_type=f32)

    def finish(e):
        oa = jnp.dot(e.astype(bf16), vaug_ref[...], preferred_element_type=f32)
        o = oa[:, :LANES] / oa[:, LANES:LANES + 1]
        a = o[:tq] - lam_ref[0] * o[tq:]
        y = a * lax.rsqrt(jnp.mean(a * a, axis=-1, keepdims=True) + EPS)
        o_ref[...] = ((y * g_ref[...]) * (1.0 - LAM_INIT)).astype(o_ref.dtype)

    safe = safe_ref[b * A_HEADS + h] == 1

    @pl.when(safe)
    def _():
        qs = stacked_q()
        qf = qs.astype(f32)
        qn = jnp.sqrt(jnp.sum(qf * qf, axis=-1, keepdims=True))
        row = lax.broadcasted_iota(jnp.int32, qn.shape, 0)
        kbase = (b * A_HEADS + h) * 2
        shift = qn * jnp.where(row < tq, kmax_ref[kbase], kmax_ref[kbase + 1])
        finish(jnp.exp(scores(qs) - shift))

    @pl.when(jnp.logical_not(safe))
    def _():
        s = scores(stacked_q())
        finish(jnp.exp(s - jnp.max(s, axis=-1, keepdims=True)))


ATTN_SAFE_BOUND = 40.0


def _attn_call(lam, qa, ka, va, qn2, kn2, subln_g, B, S, tq):
    T = qa.shape[0]
    nq = S // tq
    nsub = 2 * A_HEADS
    qmax = jnp.sqrt(qn2[:, 0, :nsub])
    kmax = jnp.sqrt(kn2[:, 0, :nsub])
    safe = jnp.all((qmax * kmax).reshape(B, A_HEADS, 2) < ATTN_SAFE_BOUND, axis=-1)
    return pl.pallas_call(
        functools.partial(_attn_kernel, tq=tq),
        grid_spec=pltpu.PrefetchScalarGridSpec(
            num_scalar_prefetch=2,
            grid=(B, A_HEADS, nq),
            in_specs=[pl.BlockSpec(memory_space=pltpu.SMEM),
                      pl.BlockSpec((tq, LANES), lambda b, h, i, km, sf: (b * nq + i, h)),
                      pl.BlockSpec((S, LANES), lambda b, h, i, km, sf: (b, h)),
                      pl.BlockSpec((S, LANES), lambda b, h, i, km, sf: (b, h)),
                      pl.BlockSpec((1, LANES), lambda b, h, i, km, sf: (0, 0))],
            out_specs=pl.BlockSpec((tq, LANES), lambda b, h, i, km, sf: (b * nq + i, h)),
            scratch_shapes=[pltpu.VMEM((S, 2 * LANES), bf16)],
        ),
        out_shape=jax.ShapeDtypeStruct((T, A_V_W), bf16),
        compiler_params=_cparams(("arbitrary", "arbitrary", "arbitrary")),
        name="attn",
    )(kmax.reshape(-1), safe.reshape(-1).astype(jnp.int32), lam, qa, ka, va, subln_g)


def _ret_kernel(df_ref, db_ref, q_ref, k_ref, v_ref, gr_ref, o_ref, uf_ref, ub_ref):
    p = pl.program_id(1)
    C = R_CHUNK
    S = q_ref.shape[0]
    n = S // C
    lane = lax.broadcasted_iota(jnp.int32, (C, LANES), 1)
    rowi = lax.broadcasted_iota(jnp.int32, (C, LANES), 0)
    first = lane < R_DK
    ones = jnp.ones((C, LANES), f32)
    lgf_a, lgf_b = -jnp.exp(ones * df_ref[2 * p]), -jnp.exp(ones * df_ref[2 * p + 1])
    lgb_a, lgb_b = -jnp.exp(ones * db_ref[2 * p]), -jnp.exp(ones * db_ref[2 * p + 1])
    lgf = jnp.where(first, lgf_a, lgf_b)
    lgb = jnp.where(first, lgb_a, lgb_b)
    idx = rowi.astype(f32)
    k_f = jnp.exp(lgf * (C - 1 - idx))
    k_b = jnp.exp(lgb * idx)
    q_f = jnp.exp(lgf * (idx + 1))
    q_b = jnp.exp(lgb * (C - idx))
    top = rowi < R_DK
    g_f = jnp.where(top, jnp.exp(lgf_a * C), jnp.exp(lgf_b * C))
    g_b = jnp.where(top, jnp.exp(lgb_a * C), jnp.exp(lgb_b * C))
    g_f = jnp.concatenate([g_f, g_f], axis=1)
    g_b = jnp.concatenate([g_b, g_b], axis=1)
    rel = (rowi - lane).astype(f32)

    def decay_mask(lf, lb):
        return jnp.where(rel >= 0, jnp.exp(lf * jnp.maximum(rel, 0.0)), jnp.exp(lb * jnp.maximum(-rel, 0.0)))

    mask = jnp.concatenate([decay_mask(lgf_a, lgb_a), decay_mask(lgf_b, lgb_b)], axis=0)

    def summaries(c, carry):
        r0 = pl.multiple_of(c * C, C)
        kc = k_ref[pl.ds(r0, C), :]
        vc = v_ref[pl.ds(r0, C), :]
        kt = jnp.concatenate([(kc * k_f).T, (kc * k_b).T], axis=0).astype(bf16)
        uu = jnp.dot(kt, vc, preferred_element_type=f32)
        uf_ref[c] = uu[:LANES]
        ub_ref[c] = uu[LANES:]
        return carry

    lax.fori_loop(0, n, summaries, 0, unroll=RET_UNROLL)

    def scan_f(c, st):
        u = uf_ref[c]
        uf_ref[c] = st
        return g_f * st + u

    lax.fori_loop(0, n, scan_f, jnp.zeros((LANES, 2 * LANES), f32))

    def scan_b(j, st):
        c = n - 1 - j
        u = ub_ref[c]
        ub_ref[c] = st
        return g_b * st + u

    lax.fori_loop(0, n, scan_b, jnp.zeros((LANES, 2 * LANES), f32))

    def outputs(c, carry):
        r0 = pl.multiple_of(c * C, C)
        qc = q_ref[pl.ds(r0, C), :]
        kc = k_ref[pl.ds(r0, C), :]
        vc = v_ref[pl.ds(r0, C), :]
        zero = jnp.zeros_like(qc)
        q2 = jnp.concatenate([jnp.where(first, qc, zero), jnp.where(first, zero, qc)], axis=0)
        s = lax.dot_general(q2.astype(bf16), kc.astype(bf16), (((1,), (1,)), ((), ())),
                            preferred_element_type=f32)
        qf2 = jnp.concatenate([q_f, q_f], axis=0)
        qb2 = jnp.concatenate([q_b, q_b], axis=0)
        lhs = jnp.concatenate([s * mask, q2 * qf2, q2 * qb2], axis=1).astype(bf16)
        rhs = jnp.concatenate([vc, uf_ref[c].astype(bf16), ub_ref[c].astype(bf16)], axis=0)
        yy = jnp.dot(lhs, rhs, preferred_element_type=f32)
        ya = yy[:C, :LANES]
        yb = yy[C:, LANES:]
        ya = ya * lax.rsqrt(jnp.mean(ya * ya, axis=-1, keepdims=True) + EPS)
        yb = yb * lax.rsqrt(jnp.mean(yb * yb, axis=-1, keepdims=True) + EPS)
        y = jnp.concatenate([ya, yb], axis=1)
        o_ref[pl.ds(r0, C), :] = (gr_ref[pl.ds(r0, C), :].astype(f32) * y).astype(o_ref.dtype)
        return carry

    lax.fori_loop(0, n, outputs, 0, unroll=RET_UNROLL)


def _ret_call(decay_f, decay_b, qr, kr, vr, gr, B, S):
    T = qr.shape[0]
    n = S // R_CHUNK
    blk = lambda w: pl.BlockSpec((S, w), lambda b, p: (b, p))
    return pl.pallas_call(
        _ret_kernel,
        grid=(B, R_HEADS // 2),
        in_specs=[pl.BlockSpec(memory_space=pltpu.SMEM), pl.BlockSpec(memory_space=pltpu.SMEM),
                  blk(LANES), blk(LANES), blk(2 * LANES), blk(2 * LANES)],
        out_specs=blk(2 * LANES),
        out_shape=jax.ShapeDtypeStruct((T, R_V_W), bf16),
        scratch_shapes=[pltpu.VMEM((n, LANES, 2 * LANES), f32), pltpu.VMEM((n, LANES, 2 * LANES), f32)],
        compiler_params=_cparams(("arbitrary", "arbitrary")),
        name="ret",
    )(decay_f, decay_b, qr, kr, vr, gr)


def _outproj_kernel(oa_ref, ob_ref, ga_ref, gb_ref, x_ref, wpa_ref, wpb_ref, wo_ref, g_ref,
                    wrh_ref, wrl_ref, br_ref,
                    x1_ref, row_ref, meta_ref, cnt_ref, tri_ref, carry_ref, *, tm):
    i = pl.program_id(0)

    @pl.when(i == 0)
    def _():
        r = lax.broadcasted_iota(jnp.int32, (tm, tm), 0)
        c = lax.broadcasted_iota(jnp.int32, (tm, tm), 1)
        tri_ref[...] = jnp.where(c < r, 1.0, 0.0).astype(bf16)
        carry_ref[...] = jnp.zeros_like(carry_ref)

    pa = jnp.dot(oa_ref[...], wpa_ref[...], preferred_element_type=f32)
    pb = jnp.dot(ob_ref[...], wpb_ref[...], preferred_element_type=f32)
    merged = ga_ref[...].astype(f32) * pa + gb_ref[...].astype(f32) * pb
    x1 = x_ref[...] + jnp.dot(merged.astype(bf16), wo_ref[...], preferred_element_type=f32)
    x1_ref[...] = x1
    xn = x1 * lax.rsqrt(jnp.mean(x1 * x1, axis=-1, keepdims=True) + EPS)
    xn = xn * g_ref[...]
    row_ref[:, :D_MODEL] = xn

    xh = xn.astype(bf16)
    xl = (xn - xh.astype(f32)).astype(bf16)
    wh = wrh_ref[...]
    logits = (jnp.dot(xh, wh, preferred_element_type=f32) + jnp.dot(xl, wh, preferred_element_type=f32)
              + jnp.dot(xh, wrl_ref[...], preferred_element_type=f32) + br_ref[...])
    lane = lax.broadcasted_iota(jnp.int32, (tm, LANES), 1)
    ninf = jnp.full((tm, LANES), -jnp.inf, f32)
    lane_f = lane.astype(f32)
    big = jnp.full((tm, LANES), float(LANES - 1), f32)

    def first_lane(hit):
        return jnp.min(jnp.where(hit, lane_f, big), axis=-1, keepdims=True).astype(jnp.int32)

    isg = lane < N_GROUPS
    lg = jnp.where(isg, logits, ninf)
    mg = jnp.max(lg, axis=-1, keepdims=True)
    gsel = first_lane(lg == mg)
    pg_top = 1.0 / jnp.sum(jnp.where(isg, jnp.exp(logits - mg), 0.0), axis=-1, keepdims=True)
    lo = N_GROUPS + EXPERTS_PER_GROUP * gsel
    le = jnp.where((lane >= lo) & (lane < lo + EXPERTS_PER_GROUP), logits, ninf)
    m1 = jnp.max(le, axis=-1, keepdims=True)
    i1 = first_lane(le == m1)
    le2 = jnp.where(lane == i1, ninf, le)
    m2 = jnp.max(le2, axis=-1, keepdims=True)
    i2 = first_lane(le2 == m2)
    t2 = jnp.exp(m2 - m1)
    gate1 = pg_top / (1.0 + t2)
    gate2 = pg_top * t2 / (1.0 + t2)
    e1 = i1 - lo
    e2 = i2 - lo
    ea = jnp.minimum(e1, e2)
    eb = jnp.maximum(e1, e2)
    gate_a = jnp.where(e1 < e2, gate1, gate2)
    gate_b = jnp.where(e1 < e2, gate2, gate1)
    pair = lax.shift_right_logical(ea * (2 * EXPERTS_PER_GROUP - 1 - ea), 1) + (eb - ea - 1)
    cls = gsel * N_PAIRS + pair
    row_ref[:, D_MODEL:] = jnp.where(lane == 0, gate_a, jnp.where(lane == 1, gate_b, 0.0))

    hit = lane == cls
    onehot = jnp.where(hit, 1.0, 0.0)
    rank = jnp.dot(tri_ref[...], onehot.astype(bf16), preferred_element_type=f32) + carry_ref[...]
    pos = jnp.sum(jnp.where(hit, rank, 0.0), axis=-1, keepdims=True)
    carry_ref[...] = carry_ref[...] + jnp.sum(onehot, axis=0, keepdims=True)
    cnt_ref[...] = jnp.broadcast_to(carry_ref[...], cnt_ref.shape)
    meta = jnp.where(lane == 0, cls.astype(f32), jnp.where(lane == 1, pos, 0.0))
    meta_t = meta.T[:8].astype(jnp.int32)
    for c in range(tm // IDX_TILE):
        meta_ref[c] = meta_t[:, c * IDX_TILE:(c + 1) * IDX_TILE]


def _outproj_call(oa, ob, ga, gb, x2d, wpa, wpb, wo, g, wrh, wrl, br, tm):
    T = x2d.shape[0]
    row = lambda i: (i, 0)
    const = lambda i: (0, 0)
    return pl.pallas_call(
        functools.partial(_outproj_kernel, tm=tm),
        grid=(T // tm,),
        in_specs=[pl.BlockSpec((tm, A_V_W), row), pl.BlockSpec((tm, R_V_W), row),
                  pl.BlockSpec((tm, D_MODEL), row), pl.BlockSpec((tm, D_MODEL), row),
                  pl.BlockSpec((tm, D_MODEL), row),
                  pl.BlockSpec((A_V_W, D_MODEL), const), pl.BlockSpec((R_V_W, D_MODEL), const),
                  pl.BlockSpec((D_MODEL, D_MODEL), const), pl.BlockSpec((1, D_MODEL), const),
                  pl.BlockSpec((D_MODEL, LANES), const), pl.BlockSpec((D_MODEL, LANES), const),
                  pl.BlockSpec((1, LANES), const)],
        out_specs=[pl.BlockSpec((tm, D_MODEL), row), pl.BlockSpec((tm, ROW_W), row),
                   pl.BlockSpec((tm // IDX_TILE, 8, IDX_TILE), lambda i: (i, 0, 0)),
                   pl.BlockSpec((8, LANES), const)],
        out_shape=[jax.ShapeDtypeStruct((T, D_MODEL), f32), jax.ShapeDtypeStruct((T, ROW_W), f32),
                   jax.ShapeDtypeStruct((T // IDX_TILE, 8, IDX_TILE), jnp.int32),
                   jax.ShapeDtypeStruct((8, LANES), f32)],
        scratch_shapes=[pltpu.VMEM((tm, tm), bf16), pltpu.VMEM((1, LANES), f32)],
        compiler_params=_cparams(("arbitrary",)),
        name="outproj",
    )(oa, ob, ga, gb, x2d, wpa, wpb, wo, g, wrh, wrl, br)


def _dispatch_kernel(pstart_ref, meta_ref, row_ref, xs_in_ref, xs_ref, sem):
    del xs_in_ref

    def issue(r, carry):
        d = pstart_ref[meta_ref[0, 0, r]] + meta_ref[0, 1, r]
        pltpu.make_async_copy(row_ref.at[pl.ds(r, 1), :], xs_ref.at[pl.ds(d, 1), :], sem).start()
        return carry

    lax.fori_loop(0, IDX_TILE, issue, 0, unroll=8)
    pltpu.make_async_copy(row_ref, xs_ref.at[pl.ds(0, IDX_TILE), :], sem).wait()


def _dispatch_call(pstart, meta, rows, xs_zero):
    T = rows.shape[0]
    P = xs_zero.shape[0]
    return pl.pallas_call(
        _dispatch_kernel,
        grid=(T // IDX_TILE,),
        in_specs=[pl.BlockSpec(memory_space=pltpu.SMEM),
                  pl.BlockSpec((1, 8, IDX_TILE), lambda i: (i, 0, 0), memory_space=pltpu.SMEM),
                  pl.BlockSpec((IDX_TILE, ROW_W), lambda i: (i, 0)),
                  pl.BlockSpec(memory_space=pl.ANY)],
        out_specs=pl.BlockSpec(memory_space=pl.ANY),
        scratch_shapes=[pltpu.SemaphoreType.DMA(())],
        out_shape=jax.ShapeDtypeStruct((P, ROW_W), f32),
        input_output_aliases={3: 0},
        compiler_params=_cparams(("arbitrary",), has_side_effects=True),
        name="dispatch",
    )(pstart, meta, rows, xs_zero)


def _expert_kernel(ea_ref, eb_ref, nb_ref, xs_ref, w1a_ref, w3a_ref, w2a_ref, w1b_ref, w3b_ref, w2b_ref,
                   y_ref):
    j = pl.program_id(0)

    @pl.when(j < nb_ref[0])
    def _():
        x = xs_ref[:, :D_MODEL].astype(bf16)
        gates = xs_ref[:, D_MODEL:]
        ga = gates[:, 0:1]
        gb = gates[:, 1:2]

        def ffn(w1_ref, w3_ref, w2_ref):
            a = jnp.dot(x, w1_ref[0], preferred_element_type=f32)
            b = jnp.dot(x, w3_ref[0], preferred_element_type=f32)
            h = (a * jax.nn.sigmoid(a)) * b
            return jnp.dot(h.astype(bf16), w2_ref[0], preferred_element_type=f32)

        y_ref[...] = ffn(w1a_ref, w3a_ref, w2a_ref) * ga + ffn(w1b_ref, w3b_ref, w2b_ref) * gb

    @pl.when(j >= nb_ref[0])
    def _():
        y_ref[...] = jnp.zeros_like(y_ref)


def _expert_call(blk_ea, blk_eb, n_used, xs, w1, w3, w2):
    P = xs.shape[0]
    nb = P // MOE_BLK
    wa = lambda j, ea, eb, n: (ea[j], 0, 0)
    wb = lambda j, ea, eb, n: (eb[j], 0, 0)
    up = (1, D_MODEL, D_EXPERT)
    down = (1, D_EXPERT, D_MODEL)
    return pl.pallas_call(
        _expert_kernel,
        grid_spec=pltpu.PrefetchScalarGridSpec(
            num_scalar_prefetch=3,
            grid=(nb,),
            in_specs=[pl.BlockSpec((MOE_BLK, ROW_W), lambda j, ea, eb, n: (jnp.minimum(j, n[0] - 1), 0)),
                      pl.BlockSpec(up, wa), pl.BlockSpec(up, wa), pl.BlockSpec(down, wa),
                      pl.BlockSpec(up, wb), pl.BlockSpec(up, wb), pl.BlockSpec(down, wb)],
            out_specs=pl.BlockSpec((MOE_BLK, D_MODEL), lambda j, ea, eb, n: (j, 0)),
        ),
        out_shape=jax.ShapeDtypeStruct((P, D_MODEL), f32),
        compiler_params=_cparams(("arbitrary",)),
        name="experts",
    )(blk_ea, blk_eb, n_used, xs, w1, w3, w2, w1, w3, w2)


def _combine_kernel(pstart_ref, meta_ref, x1_ref, g_ref, y_ref, o_ref, buf_ref, sem):
    def issue(r, carry):
        d = pstart_ref[meta_ref[0, 0, r]] + meta_ref[0, 1, r]
        pltpu.make_async_copy(y_ref.at[pl.ds(d, 1), :], buf_ref.at[pl.ds(r, 1), :], sem).start()
        return carry

    lax.fori_loop(0, IDX_TILE, issue, 0, unroll=8)
    pltpu.make_async_copy(y_ref.at[pl.ds(0, IDX_TILE), :], buf_ref, sem).wait()
    x = x1_ref[...] + buf_ref[...]
    y = x * lax.rsqrt(jnp.mean(x * x, axis=-1, keepdims=True) + EPS)
    o_ref[...] = y * g_ref[...]


def _combine_call(pstart, meta, x1, g, y):
    T = x1.shape[0]
    return pl.pallas_call(
        _combine_kernel,
        grid=(T // IDX_TILE,),
        in_specs=[pl.BlockSpec(memory_space=pltpu.SMEM),
                  pl.BlockSpec((1, 8, IDX_TILE), lambda i: (i, 0, 0), memory_space=pltpu.SMEM),
                  pl.BlockSpec((IDX_TILE, D_MODEL), lambda i: (i, 0)),
                  pl.BlockSpec((1, D_MODEL), lambda i: (0, 0)),
                  pl.BlockSpec(memory_space=pl.ANY)],
        out_specs=pl.BlockSpec((IDX_TILE, D_MODEL), lambda i: (i, 0)),
        scratch_shapes=[pltpu.VMEM((IDX_TILE, D_MODEL), f32), pltpu.SemaphoreType.DMA(())],
        out_shape=jax.ShapeDtypeStruct((T, D_MODEL), f32),
        compiler_params=_cparams(("arbitrary",)),
        name="combine",
    )(pstart, meta, x1, g, y)


def _pair_table():
    pa, pb = [], []
    for a in range(EXPERTS_PER_GROUP):
        for b in range(a + 1, EXPERTS_PER_GROUP):
            pa.append(a)
            pb.append(b)
    return np.asarray(pa, np.int32), np.asarray(pb, np.int32)


def _route_plan(counts, T):
    nb = T // MOE_BLK + N_CLASSES
    cnt = counts[0, :N_CLASSES].astype(jnp.int32)
    nblk = (cnt + MOE_BLK - 1) // MOE_BLK
    cum = jnp.cumsum(nblk)
    pstart = jnp.pad((cum - nblk) * MOE_BLK, (0, LANES - N_CLASSES)).astype(jnp.int32)
    blk = jnp.minimum(jnp.arange(nb, dtype=jnp.int32), cum[-1] - 1)
    blk_cls = jnp.minimum(jnp.sum((cum[None, :] <= blk[:, None]).astype(jnp.int32), axis=1), N_CLASSES - 1)
    pa, pb = _pair_table()
    grp = blk_cls // N_PAIRS
    pr = blk_cls % N_PAIRS
    blk_ea = grp * EXPERTS_PER_GROUP + jnp.asarray(pa)[pr]
    blk_eb = grp * EXPERTS_PER_GROUP + jnp.asarray(pb)[pr]
    n_used = cum[-1:].astype(jnp.int32)
    return pstart, blk_ea.astype(jnp.int32), blk_eb.astype(jnp.int32), n_used, nb * MOE_BLK


ATTN_SCORE_ELEMS = 1 << 20


def _trunk(x, lam, wts, tabs, tm=512, tq=None):
    B, S, _ = x.shape
    tq = tq or min(512, ATTN_SCORE_ELEMS // S)
    T = B * S
    x2d = x.reshape(T, D_MODEL)
    tabs_a, tabs_r = tabs
    qa, ka, va, qr, kr, vr, gr, ga, gb, qn2, kn2 = _inproj_call(
        x2d, S, wts['attn_norm_g'], wts['w_in'], [t[:S] for t in tabs_a], [t[:S] for t in tabs_r], tm)
    oa = _attn_call(lam, qa, ka, va, qn2, kn2, wts['subln_g'], B, S, tq)
    ob = _ret_call(wts['ret_decay_f'], wts['ret_decay_b'], qr, kr, vr, gr, B, S)
    x1, rows, meta, counts = _outproj_call(
        oa, ob, ga, gb, x2d, wts['w_pa'], wts['w_pb'], wts['w_out'], wts['ffn_norm_g'],
        wts['wr_hi'], wts['wr_lo'], wts['b_r'], tm)
    pstart, blk_ea, blk_eb, n_used, P = _route_plan(counts, T)
    xs = _dispatch_call(pstart, meta, rows, jnp.zeros((P, ROW_W), f32))
    y = _expert_call(blk_ea, blk_eb, n_used, xs, wts['w1'], wts['w3'], wts['w2'])
    out = _combine_call(pstart, meta, x1, wts['final_norm_g'], y)
    return out.reshape(B, S, D_MODEL)


def kernel(x_prompt, x_sample, attn_norm_g, w_in, lam_q1, lam_k1, lam_q2, lam_k2, subln_g,
           ret_decay_f, ret_decay_b, w_pa, w_pb, w_out, ffn_norm_g, w_rg, b_rg, w_re, b_re,
           w1, w3, w2, final_norm_g):
    wr = jnp.concatenate([w_rg[0], jnp.transpose(w_re[0], (1, 0, 2)).reshape(D_MODEL, N_EXPERTS)], axis=1)
    wr = jnp.pad(wr, ((0, 0), (0, LANES - wr.shape[1])))
    wr_hi = wr.astype(bf16)
    b_r = jnp.pad(jnp.concatenate([b_rg[0], b_re[0].reshape(-1)]), (0, LANES - N_GROUPS - N_EXPERTS))
    wts = dict(
        attn_norm_g=attn_norm_g[0][None, :], w_in=w_in[0].astype(bf16), subln_g=subln_g[0][None, :],
        ret_decay_f=ret_decay_f[0], ret_decay_b=ret_decay_b[0],
        w_pa=w_pa[0].astype(bf16), w_pb=w_pb[0].astype(bf16), w_out=w_out[0].astype(bf16),
        ffn_norm_g=ffn_norm_g[0][None, :], wr_hi=wr_hi, wr_lo=(wr - wr_hi.astype(f32)).astype(bf16),
        b_r=b_r[None, :], w1=w1[0].astype(bf16), w3=w3[0].astype(bf16), w2=w2[0].astype(bf16),
        final_norm_g=final_norm_g[None, :])
    lam = _lam_call(lam_q1, lam_k1, lam_q2, lam_k2)[0, :1]
    s_max = max(x_prompt.shape[1], x_sample.shape[1])
    tabs = (_rot_tables(s_max, A_ROT, A_THETA, A_HD), _rot_tables(s_max, R_DK, R_THETA, R_DK))
    return (_trunk(x_prompt, lam, wts, tabs), _trunk(x_sample, lam, wts, tabs))
```

```python
import functools
import math

import numpy as np
import jax
import jax.numpy as jnp
from jax import lax
from jax.experimental import pallas as pl
from jax.experimental.pallas import tpu as pltpu

f32 = jnp.float32
bf16 = jnp.bfloat16

D_MODEL = 1024
A_HEADS = 4
A_HD = 64
A_VD = 128
A_ROT = 16
A_THETA = 500000.0
R_HEADS = 4
R_DK = 64
R_DV = 128
R_CHUNK = 128
R_THETA = 10000.0
N_GROUPS = 4
EXPERTS_PER_GROUP = 8
N_EXPERTS = 32
D_EXPERT = 512
EPS = 1e-6
LAM_INIT = 0.8 - 0.6 * math.exp(-0.3 * 0)

A_QK_W = 512
A_V_W = 512
R_QK_W = 256
R_V_W = 512
D_IN = 5120
OFF_QA, OFF_KA, OFF_VA, OFF_QR, OFF_KR, OFF_VR, OFF_GR, OFF_GA, OFF_GB = (
    0, 512, 1024, 1536, 1792, 2048, 2560, 3072, 4096)

LANES = 128
N_PAIRS = 28
N_CLASSES = N_GROUPS * N_PAIRS
ROW_W = D_MODEL + LANES
MOE_BLK = 128
IDX_TILE = 256
DEST_TILES = 128
RET_UNROLL = 4
VMEM_LIMIT = 56 * 1024 * 1024


def _cparams(sem, **kw):
    return pltpu.CompilerParams(dimension_semantics=sem, vmem_limit_bytes=VMEM_LIMIT, **kw)


def _lam_kernel(q1_ref, k1_ref, q2_ref, k2_ref, o_ref):
    a = jnp.sum(q1_ref[...] * k1_ref[...], axis=-1, keepdims=True)
    b = jnp.sum(q2_ref[...] * k2_ref[...], axis=-1, keepdims=True)
    lam = jnp.exp(a) - jnp.exp(b) + LAM_INIT
    o_ref[...] = jnp.broadcast_to(lam, o_ref.shape)


def _lam_call(q1, k1, q2, k2):
    return pl.pallas_call(
        _lam_kernel,
        out_shape=jax.ShapeDtypeStruct((8, LANES), f32),
        name="lam",
    )(q1, k1, q2, k2)


def _rotary(z, c, s1, s2, shift):
    outs = []
    for j in range(z.shape[1] // LANES):
        zz = z[:, j * LANES:(j + 1) * LANES]
        outs.append(zz * c + pltpu.roll(zz, LANES - shift, 1) * s1 + pltpu.roll(zz, shift, 1) * s2)
    return jnp.concatenate(outs, axis=1)


def _inproj_kernel(x_ref, g_ref, w_ref, ca_ref, sa1_ref, sa2_ref, cr_ref, sr1_ref, sr2_ref,
                   qa_ref, ka_ref, va_ref, qr_ref, kr_ref, vr_ref, gr_ref, ga_ref, gb_ref,
                   qn_ref, kn_ref, *, nps):
    x = x_ref[...]
    xn = x * lax.rsqrt(jnp.mean(x * x, axis=-1, keepdims=True) + EPS)
    xn = (xn * g_ref[...]).astype(bf16)

    def proj(off, width):
        return jnp.dot(xn, w_ref[:, off:off + width], preferred_element_type=f32)

    ca, sa1, sa2 = ca_ref[...], sa1_ref[...], sa2_ref[...]
    cr, sr1, sr2 = cr_ref[...], sr1_ref[...], sr2_ref[...]
    qa = (_rotary(proj(OFF_QA, A_QK_W), ca, sa1, sa2, A_ROT // 2) * (A_HD ** -0.5)).astype(bf16)
    ka = _rotary(proj(OFF_KA, A_QK_W), ca, sa1, sa2, A_ROT // 2).astype(bf16)
    qa_ref[...] = qa
    ka_ref[...] = ka

    seg = (lax.broadcasted_iota(jnp.int32, (A_QK_W, LANES), 0) // A_HD
           == lax.broadcasted_iota(jnp.int32, (A_QK_W, LANES), 1))
    ind = jnp.where(seg, 1.0, 0.0).astype(bf16)

    def norm2max(z):
        n2 = jnp.dot(z * z, ind, preferred_element_type=f32)
        return jnp.max(n2, axis=0, keepdims=True)

    qmax, kmax = norm2max(qa), norm2max(ka)
    first = pl.program_id(0) % nps == 0

    @pl.when(first)
    def _():
        qn_ref[0] = qmax
        kn_ref[0] = kmax

    @pl.when(jnp.logical_not(first))
    def _():
        qn_ref[0] = jnp.maximum(qn_ref[0], qmax)
        kn_ref[0] = jnp.maximum(kn_ref[0], kmax)

    va_ref[...] = proj(OFF_VA, A_V_W).astype(bf16)
    qr_ref[...] = _rotary(proj(OFF_QR, R_QK_W), cr, sr1, sr2, R_DK // 2)
    kr_ref[...] = _rotary(proj(OFF_KR, R_QK_W), cr, sr1, sr2, R_DK // 2) * (R_DK ** -0.5)
    vr_ref[...] = proj(OFF_VR, R_V_W).astype(bf16)
    gr = proj(OFF_GR, R_V_W)
    gr_ref[...] = (gr * jax.nn.sigmoid(gr)).astype(bf16)
    ga_ref[...] = jax.nn.sigmoid(proj(OFF_GA, D_MODEL)).astype(bf16)
    gb_ref[...] = jax.nn.sigmoid(proj(OFF_GB, D_MODEL)).astype(bf16)


def _rot_tables(S, rot_dim, theta, head_dim):
    inv = theta ** (-jnp.arange(0, rot_dim, 2, dtype=f32) / rot_dim)
    pos = jnp.arange(S, dtype=f32)
    ang = pos[:, None] * inv[None, :]
    cos, sin = jnp.cos(ang), jnp.sin(ang)
    half = rot_dim // 2
    pad = head_dim - rot_dim
    c = jnp.concatenate([cos, cos, jnp.ones((S, pad), f32)], axis=1)
    s1 = jnp.concatenate([-sin, jnp.zeros((S, half + pad), f32)], axis=1)
    s2 = jnp.concatenate([jnp.zeros((S, half), f32), sin, jnp.zeros((S, pad), f32)], axis=1)
    rep = LANES // head_dim
    return tuple(jnp.tile(t, (1, rep)) for t in (c, s1, s2))


def _inproj_call(x2d, S, g, w_in_bf, tabs_a, tabs_r, tm):
    T = x2d.shape[0]
    nps = S // tm
    row = lambda i: (i, 0)
    posmap = lambda i: (i % nps, 0)
    const = lambda i: (0, 0)
    tab_spec = pl.BlockSpec((tm, LANES), posmap)
    outs = [
        (A_QK_W, bf16), (A_QK_W, bf16), (A_V_W, bf16),
        (R_QK_W, f32), (R_QK_W, f32), (R_V_W, bf16), (R_V_W, bf16),
        (D_MODEL, bf16), (D_MODEL, bf16),
    ]
    nrm_spec = pl.BlockSpec((1, 1, LANES), lambda i: (i // nps, 0, 0))
    nrm_shape = jax.ShapeDtypeStruct((T // S, 1, LANES), f32)
    return pl.pallas_call(
        functools.partial(_inproj_kernel, nps=nps),
        grid=(T // tm,),
        in_specs=[pl.BlockSpec((tm, D_MODEL), row),
                  pl.BlockSpec((1, D_MODEL), const),
                  pl.BlockSpec((D_MODEL, D_IN), const)] + [tab_spec] * 6,
        out_specs=[pl.BlockSpec((tm, w), row) for w, _ in outs] + [nrm_spec, nrm_spec],
        out_shape=[jax.ShapeDtypeStruct((T, w), dt) for w, dt in outs] + [nrm_shape, nrm_shape],
        compiler_params=_cparams(("arbitrary",)),
        name="inproj",
    )(x2d, g, w_in_bf, *tabs_a, *tabs_r)


def _attn_kernel(kmax_ref, safe_ref, lam_ref, q_ref, k_ref, v_ref, g_ref, o_ref, vaug_ref, *, tq):
    b, h, qi = pl.program_id(0), pl.program_id(1), pl.program_id(2)
    S = v_ref.shape[0]

    @pl.when(qi == 0)
    def _():
        lane = lax.broadcasted_iota(jnp.int32, (S, LANES), 1)
        vaug_ref[:, :LANES] = v_ref[...]
        vaug_ref[:, LANES:] = jnp.where(lane == 0, 1.0, 0.0).astype(bf16)

    def stacked_q():
        q = q_ref[...]
        lane = lax.broadcasted_iota(jnp.int32, q.shape, 1)
        zero = jnp.zeros_like(q)
        return jnp.concatenate([jnp.where(lane < A_HD, q, zero), jnp.where(lane >= A_HD, q, zero)], axis=0)

    def scores(qs):
        return lax.dot_general(qs, k_ref[...], (((1,), (1,)), ((), ())), preferred_element_type=f32)

    def finish(e):
        oa = jnp.dot(e.astype(bf16), vaug_ref[...], preferred_element_type=f32)
        o = oa[:, :LANES] / oa[:, LANES:LANES + 1]
        a = o[:tq] - lam_ref[0] * o[tq:]
        y = a * lax.rsqrt(jnp.mean(a * a, axis=-1, keepdims=True) + EPS)
        o_ref[...] = ((y * g_ref[...]) * (1.0 - LAM_INIT)).astype(o_ref.dtype)

    safe = safe_ref[b * A_HEADS + h] == 1

    @pl.when(safe)
    def _():
        qs = stacked_q()
        qf = qs.astype(f32)
        qn = jnp.sqrt(jnp.sum(qf * qf, axis=-1, keepdims=True))
        row = lax.broadcasted_iota(jnp.int32, qn.shape, 0)
        kbase = (b * A_HEADS + h) * 2
        shift = qn * jnp.where(row < tq, kmax_ref[kbase], kmax_ref[kbase + 1])
        finish(jnp.exp(scores(qs) - shift))

    @pl.when(jnp.logical_not(safe))
    def _():
        s = scores(stacked_q())
        finish(jnp.exp(s - jnp.max(s, axis=-1, keepdims=True)))


ATTN_SAFE_BOUND = 40.0


def _attn_call(lam, qa, ka, va, qn2, kn2, subln_g, B, S, tq):
    T = qa.shape[0]
    nq = S // tq
    nsub = 2 * A_HEADS
    qmax = jnp.sqrt(qn2[:, 0, :nsub])
    kmax = jnp.sqrt(kn2[:, 0, :nsub])
    safe = jnp.all((qmax * kmax).reshape(B, A_HEADS, 2) < ATTN_SAFE_BOUND, axis=-1)
    return pl.pallas_call(
        functools.partial(_attn_kernel, tq=tq),
        grid_spec=pltpu.PrefetchScalarGridSpec(
            num_scalar_prefetch=2,
            grid=(B, A_HEADS, nq),
            in_specs=[pl.BlockSpec(memory_space=pltpu.SMEM),
                      pl.BlockSpec((tq, LANES), lambda b, h, i, km, sf: (b * nq + i, h)),
                      pl.BlockSpec((S, LANES), lambda b, h, i, km, sf: (b, h)),
                      pl.BlockSpec((S, LANES), lambda b, h, i, km, sf: (b, h)),
                      pl.BlockSpec((1, LANES), lambda b, h, i, km, sf: (0, 0))],
            out_specs=pl.BlockSpec((tq, LANES), lambda b, h, i, km, sf: (b * nq + i, h)),
            scratch_shapes=[pltpu.VMEM((S, 2 * LANES), bf16)],
        ),
        out_shape=jax.ShapeDtypeStruct((T, A_V_W), bf16),
        compiler_params=_cparams(("arbitrary", "arbitrary", "arbitrary")),
        name="attn",
    )(kmax.reshape(-1), safe.reshape(-1).astype(jnp.int32), lam, qa, ka, va, subln_g)


def _ret_kernel(df_ref, db_ref, q_ref, k_ref, v_ref, gr_ref, o_ref, uf_ref, ub_ref):
    p = pl.program_id(1)
    C = R_CHUNK
    S = q_ref.shape[0]
    n = S // C
    lane = lax.broadcasted_iota(jnp.int32, (C, LANES), 1)
    rowi = lax.broadcasted_iota(jnp.int32, (C, LANES), 0)
    first = lane < R_DK
    ones = jnp.ones((C, LANES), f32)
    lgf_a, lgf_b = -jnp.exp(ones * df_ref[2 * p]), -jnp.exp(ones * df_ref[2 * p + 1])
    lgb_a, lgb_b = -jnp.exp(ones * db_ref[2 * p]), -jnp.exp(ones * db_ref[2 * p + 1])
    lgf = jnp.where(first, lgf_a, lgf_b)
    lgb = jnp.where(first, lgb_a, lgb_b)
    idx = rowi.astype(f32)
    k_f = jnp.exp(lgf * (C - 1 - idx))
    k_b = jnp.exp(lgb * idx)
    q_f = jnp.exp(lgf * (idx + 1))
    q_b = jnp.exp(lgb * (C - idx))
    top = rowi < R_DK
    g_f = jnp.where(top, jnp.exp(lgf_a * C), jnp.exp(lgf_b * C))
    g_b = jnp.where(top, jnp.exp(lgb_a * C), jnp.exp(lgb_b * C))
    g_f = jnp.concatenate([g_f, g_f], axis=1)
    g_b = jnp.concatenate([g_b, g_b], axis=1)
    rel = (rowi - lane).astype(f32)

    def decay_mask(lf, lb):
        return jnp.where(rel >= 0, jnp.exp(lf * jnp.maximum(rel, 0.0)), jnp.exp(lb * jnp.maximum(-rel, 0.0)))

    mask = jnp.concatenate([decay_mask(lgf_a, lgb_a), decay_mask(lgf_b, lgb_b)], axis=0)

    def summaries(c, carry):
        r0 = pl.multiple_of(c * C, C)
        kc = k_ref[pl.ds(r0, C), :]
        vc = v_ref[pl.ds(r0, C), :]
        kt = jnp.concatenate([(kc * k_f).T, (kc * k_b).T], axis=0).astype(bf16)
        uu = jnp.dot(kt, vc, preferred_element_type=f32)
        uf_ref[c] = uu[:LANES]
        ub_ref[c] = uu[LANES:]
        return carry

    lax.fori_loop(0, n, summaries, 0, unroll=RET_UNROLL)

    def scan_f(c, st):
        u = uf_ref[c]
        uf_ref[c] = st
        return g_f * st + u

    lax.fori_loop(0, n, scan_f, jnp.zeros((LANES, 2 * LANES), f32))

    def scan_b(j, st):
        c = n - 1 - j
        u = ub_ref[c]
        ub_ref[c] = st
        return g_b * st + u

    lax.fori_loop(0, n, scan_b, jnp.zeros((LANES, 2 * LANES), f32))

    def outputs(c, carry):
        r0 = pl.multiple_of(c * C, C)
        qc = q_ref[pl.ds(r0, C), :]
        kc = k_ref[pl.ds(r0, C), :]
        vc = v_ref[pl.ds(r0, C), :]
        zero = jnp.zeros_like(qc)
        q2 = jnp.concatenate([jnp.where(first, qc, zero), jnp.where(first, zero, qc)], axis=0)
        s = lax.dot_general(q2.astype(bf16), kc.astype(bf16), (((1,), (1,)), ((), ())),
                            preferred_element_type=f32)
        qf2 = jnp.concatenate([q_f, q_f], axis=0)
        qb2 = jnp.concatenate([q_b, q_b], axis=0)
        lhs = jnp.concatenate([s * mask, q2 * qf2, q2 * qb2], axis=1).astype(bf16)
        rhs = jnp.concatenate([vc, uf_ref[c].astype(bf16), ub_ref[c].astype(bf16)], axis=0)
        yy = jnp.dot(lhs, rhs, preferred_element_type=f32)
        ya = yy[:C, :LANES]
        yb = yy[C:, LANES:]
        ya = ya * lax.rsqrt(jnp.mean(ya * ya, axis=-1, keepdims=True) + EPS)
        yb = yb * lax.rsqrt(jnp.mean(yb * yb, axis=-1, keepdims=True) + EPS)
        y = jnp.concatenate([ya, yb], axis=1)
        o_ref[pl.ds(r0, C), :] = (gr_ref[pl.ds(r0, C), :].astype(f32) * y).astype(o_ref.dtype)
        return carry

    lax.fori_loop(0, n, outputs, 0, unroll=RET_UNROLL)


def _ret_call(decay_f, decay_b, qr, kr, vr, gr, B, S):
    T = qr.shape[0]
    n = S // R_CHUNK
    blk = lambda w: pl.BlockSpec((S, w), lambda b, p: (b, p))
    return pl.pallas_call(
        _ret_kernel,
        grid=(B, R_HEADS // 2),
        in_specs=[pl.BlockSpec(memory_space=pltpu.SMEM), pl.BlockSpec(memory_space=pltpu.SMEM),
                  blk(LANES), blk(LANES), blk(2 * LANES), blk(2 * LANES)],
        out_specs=blk(2 * LANES),
        out_shape=jax.ShapeDtypeStruct((T, R_V_W), bf16),
        scratch_shapes=[pltpu.VMEM((n, LANES, 2 * LANES), f32), pltpu.VMEM((n, LANES, 2 * LANES), f32)],
        compiler_params=_cparams(("arbitrary", "arbitrary")),
        name="ret",
    )(decay_f, decay_b, qr, kr, vr, gr)


def _outproj_kernel(oa_ref, ob_ref, ga_ref, gb_ref, x_ref, wpa_ref, wpb_ref, wo_ref, g_ref,
                    wrh_ref, wrl_ref, br_ref,
                    x1_ref, row_ref, meta_ref, cnt_ref, tri_ref, carry_ref, *, tm):
    i = pl.program_id(0)

    @pl.when(i == 0)
    def _():
        r = lax.broadcasted_iota(jnp.int32, (tm, tm), 0)
        c = lax.broadcasted_iota(jnp.int32, (tm, tm), 1)
        tri_ref[...] = jnp.where(c < r, 1.0, 0.0).astype(bf16)
        carry_ref[...] = jnp.zeros_like(carry_ref)

    pa = jnp.dot(oa_ref[...], wpa_ref[...], preferred_element_type=f32)
    pb = jnp.dot(ob_ref[...], wpb_ref[...], preferred_element_type=f32)
    merged = ga_ref[...].astype(f32) * pa + gb_ref[...].astype(f32) * pb
    x1 = x_ref[...] + jnp.dot(merged.astype(bf16), wo_ref[...], preferred_element_type=f32)
    x1_ref[...] = x1
    xn = x1 * lax.rsqrt(jnp.mean(x1 * x1, axis=-1, keepdims=True) + EPS)
    xn = xn * g_ref[...]
    row_ref[:, :D_MODEL] = xn

    xh = xn.astype(bf16)
    xl = (xn - xh.astype(f32)).astype(bf16)
    wh = wrh_ref[...]
    logits = (jnp.dot(xh, wh, preferred_element_type=f32) + jnp.dot(xl, wh, preferred_element_type=f32)
              + jnp.dot(xh, wrl_ref[...], preferred_element_type=f32) + br_ref[...])
    lane = lax.broadcasted_iota(jnp.int32, (tm, LANES), 1)
    ninf = jnp.full((tm, LANES), -jnp.inf, f32)
    lane_f = lane.astype(f32)
    big = jnp.full((tm, LANES), float(LANES - 1), f32)

    def first_lane(hit):
        return jnp.min(jnp.where(hit, lane_f, big), axis=-1, keepdims=True).astype(jnp.int32)

    isg = lane < N_GROUPS
    lg = jnp.where(isg, logits, ninf)
    mg = jnp.max(lg, axis=-1, keepdims=True)
    gsel = first_lane(lg == mg)
    pg_top = 1.0 / jnp.sum(jnp.where(isg, jnp.exp(logits - mg), 0.0), axis=-1, keepdims=True)
    lo = N_GROUPS + EXPERTS_PER_GROUP * gsel
    le = jnp.where((lane >= lo) & (lane < lo + EXPERTS_PER_GROUP), logits, ninf)
    m1 = jnp.max(le, axis=-1, keepdims=True)
    i1 = first_lane(le == m1)
    le2 = jnp.where(lane == i1, ninf, le)
    m2 = jnp.max(le2, axis=-1, keepdims=True)
    i2 = first_lane(le2 == m2)
    t2 = jnp.exp(m2 - m1)
    gate1 = pg_top / (1.0 + t2)
    gate2 = pg_top * t2 / (1.0 + t2)
    e1 = i1 - lo
    e2 = i2 - lo
    ea = jnp.minimum(e1, e2)
    eb = jnp.maximum(e1, e2)
    gate_a = jnp.where(e1 < e2, gate1, gate2)
    gate_b = jnp.where(e1 < e2, gate2, gate1)
    pair = lax.shift_right_logical(ea * (2 * EXPERTS_PER_GROUP - 1 - ea), 1) + (eb - ea - 1)
    cls = gsel * N_PAIRS + pair
    row_ref[:, D_MODEL:] = jnp.where(lane == 0, gate_a, jnp.where(lane == 1, gate_b, 0.0))

    hit = lane == cls
    onehot = jnp.where(hit, 1.0, 0.0)
    rank = jnp.dot(tri_ref[...], onehot.astype(bf16), preferred_element_type=f32) + carry_ref[...]
    pos = jnp.sum(jnp.where(hit, rank, 0.0), axis=-1, keepdims=True)
    carry_ref[...] = carry_ref[...] + jnp.sum(onehot, axis=0, keepdims=True)
    cnt_ref[...] = jnp.broadcast_to(carry_ref[...], cnt_ref.shape)
    meta = jnp.where(lane == 0, cls.astype(f32), jnp.where(lane == 1, pos, 0.0))
    meta_t = meta.T[:8].astype(jnp.int32)
    for c in range(tm // IDX_TILE):
        meta_ref[c] = meta_t[:, c * IDX_TILE:(c + 1) * IDX_TILE]


def _outproj_call(oa, ob, ga, gb, x2d, wpa, wpb, wo, g, wrh, wrl, br, tm):
    T = x2d.shape[0]
    row = lambda i: (i, 0)
    const = lambda i: (0, 0)
    return pl.pallas_call(
        functools.partial(_outproj_kernel, tm=tm),
        grid=(T // tm,),
        in_specs=[pl.BlockSpec((tm, A_V_W), row), pl.BlockSpec((tm, R_V_W), row),
                  pl.BlockSpec((tm, D_MODEL), row), pl.BlockSpec((tm, D_MODEL), row),
                  pl.BlockSpec((tm, D_MODEL), row),
                  pl.BlockSpec((A_V_W, D_MODEL), const), pl.BlockSpec((R_V_W, D_MODEL), const),
                  pl.BlockSpec((D_MODEL, D_MODEL), const), pl.BlockSpec((1, D_MODEL), const),
                  pl.BlockSpec((D_MODEL, LANES), const), pl.BlockSpec((D_MODEL, LANES), const),
                  pl.BlockSpec((1, LANES), const)],
        out_specs=[pl.BlockSpec((tm, D_MODEL), row), pl.BlockSpec((tm, ROW_W), row),
                   pl.BlockSpec((tm // IDX_TILE, 8, IDX_TILE), lambda i: (i, 0, 0)),
                   pl.BlockSpec((8, LANES), const)],
        out_shape=[jax.ShapeDtypeStruct((T, D_MODEL), f32), jax.ShapeDtypeStruct((T, ROW_W), f32),
                   jax.ShapeDtypeStruct((T // IDX_TILE, 8, IDX_TILE), jnp.int32),
                   jax.ShapeDtypeStruct((8, LANES), f32)],
        scratch_shapes=[pltpu.VMEM((tm, tm), bf16), pltpu.VMEM((1, LANES), f32)],
        compiler_params=_cparams(("arbitrary",)),
        name="outproj",
    )(oa, ob, ga, gb, x2d, wpa, wpb, wo, g, wrh, wrl, br)


def _dest_kernel(pstart_ref, meta_ref, o_ref):
    cls = meta_ref[:, 0, :]
    rank = meta_ref[:, 1, :]

    def body(c, acc):
        return acc + jnp.where(cls == c, pstart_ref[c], 0)

    o_ref[:, 0, :] = lax.fori_loop(0, N_CLASSES, body, rank)


def _dest_call(pstart, meta):
    nt = meta.shape[0]
    blk = min(nt, DEST_TILES)
    return pl.pallas_call(
        _dest_kernel,
        grid=(nt // blk,),
        in_specs=[pl.BlockSpec(memory_space=pltpu.SMEM),
                  pl.BlockSpec((blk, 8, IDX_TILE), lambda i: (i, 0, 0))],
        out_specs=pl.BlockSpec((blk, 1, IDX_TILE), lambda i: (i, 0, 0)),
        out_shape=jax.ShapeDtypeStruct((nt, 1, IDX_TILE), jnp.int32),
        compiler_params=_cparams(("arbitrary",)),
        name="dest",
    )(pstart, meta)


def _dispatch_kernel(lastblk_ref, dest_ref, rows_ref, zeros_ref, xs_ref, buf_ref, sem_in, sem_out, zsem):
    i = pl.program_id(0)
    n = pl.num_programs(0)
    slot = i % 2

    def load(t, s):
        r0 = pl.multiple_of(t * IDX_TILE, IDX_TILE)
        return pltpu.make_async_copy(rows_ref.at[pl.ds(r0, IDX_TILE), :], buf_ref.at[s], sem_in.at[s])

    def scatter_wait(s):
        pltpu.make_async_copy(buf_ref.at[s], xs_ref.at[pl.ds(0, IDX_TILE), :], sem_out.at[s]).wait()

    @pl.when(i == 0)
    def _():
        def zero_block(j):
            off = pl.multiple_of(j * MOE_BLK, MOE_BLK)
            pltpu.make_async_copy(zeros_ref, xs_ref.at[pl.ds(off, MOE_BLK), :], zsem).start()

        def zfill_last(c, carry):
            j = lastblk_ref[c]

            @pl.when(j >= 0)
            def _():
                zero_block(j)

            return carry

        def zfill_unused(j, carry):
            zero_block(j)
            return carry

        def zwait(c, carry):
            pltpu.make_async_copy(zeros_ref, xs_ref.at[pl.ds(0, MOE_BLK), :], zsem).wait()
            return carry

        n_used = lastblk_ref[N_CLASSES]
        n_blocks = xs_ref.shape[0] // MOE_BLK
        lax.fori_loop(0, N_CLASSES, zfill_last, 0)
        lax.fori_loop(n_used, n_blocks, zfill_unused, 0)
        lax.fori_loop(0, lastblk_ref[N_CLASSES + 1] + n_blocks - n_used, zwait, 0)
        load(0, 0).start()

    @pl.when(i >= 1)
    def _():
        scatter_wait(1 - slot)

    @pl.when(i + 1 < n)
    def _():
        load(i + 1, 1 - slot).start()

    load(i, slot).wait()
    for r in range(IDX_TILE):
        d = dest_ref[0, 0, r]
        pltpu.make_async_copy(buf_ref.at[slot, pl.ds(r, 1), :], xs_ref.at[pl.ds(d, 1), :], sem_out.at[slot]).start()

    @pl.when(i == n - 1)
    def _():
        scatter_wait(slot)


def _dispatch_call(lastblk, dest, rows, P):
    T = rows.shape[0]
    return pl.pallas_call(
        _dispatch_kernel,
        grid=(T // IDX_TILE,),
        in_specs=[pl.BlockSpec(memory_space=pltpu.SMEM),
                  pl.BlockSpec((1, 1, IDX_TILE), lambda i: (i, 0, 0), memory_space=pltpu.SMEM),
                  pl.BlockSpec(memory_space=pl.ANY),
                  pl.BlockSpec(memory_space=pl.ANY)],
        out_specs=pl.BlockSpec(memory_space=pl.ANY),
        scratch_shapes=[pltpu.VMEM((2, IDX_TILE, ROW_W), f32), pltpu.SemaphoreType.DMA((2,)),
                        pltpu.SemaphoreType.DMA((2,)), pltpu.SemaphoreType.DMA(())],
        out_shape=jax.ShapeDtypeStruct((P, ROW_W), f32),
        compiler_params=_cparams(("arbitrary",), has_side_effects=True),
        name="dispatch",
    )(lastblk, dest, rows, jnp.zeros((MOE_BLK, ROW_W), f32))


def _expert_kernel(ea_ref, eb_ref, nb_ref, xs_ref, w1a_ref, w3a_ref, w2a_ref, w1b_ref, w3b_ref, w2b_ref,
                   y_ref):
    j = pl.program_id(0)

    @pl.when(j < nb_ref[0])
    def _():
        x = xs_ref[:, :D_MODEL].astype(bf16)
        gates = xs_ref[:, D_MODEL:]
        ga = gates[:, 0:1]
        gb = gates[:, 1:2]

        def ffn(w1_ref, w3_ref, w2_ref):
            a = jnp.dot(x, w1_ref[0], preferred_element_type=f32)
            b = jnp.dot(x, w3_ref[0], preferred_element_type=f32)
            h = (a * jax.nn.sigmoid(a)) * b
            return jnp.dot(h.astype(bf16), w2_ref[0], preferred_element_type=f32)

        y_ref[...] = ffn(w1a_ref, w3a_ref, w2a_ref) * ga + ffn(w1b_ref, w3b_ref, w2b_ref) * gb

    @pl.when(j >= nb_ref[0])
    def _():
        y_ref[...] = jnp.zeros_like(y_ref)


def _expert_call(blk_ea, blk_eb, n_used, xs, w1, w3, w2):
    P = xs.shape[0]
    nb = P // MOE_BLK
    wa = lambda j, ea, eb, n: (ea[j], 0, 0)
    wb = lambda j, ea, eb, n: (eb[j], 0, 0)
    up = (1, D_MODEL, D_EXPERT)
    down = (1, D_EXPERT, D_MODEL)
    return pl.pallas_call(
        _expert_kernel,
        grid_spec=pltpu.PrefetchScalarGridSpec(
            num_scalar_prefetch=3,
            grid=(nb,),
            in_specs=[pl.BlockSpec((MOE_BLK, ROW_W), lambda j, ea, eb, n: (jnp.minimum(j, n[0] - 1), 0)),
                      pl.BlockSpec(up, wa), pl.BlockSpec(up, wa), pl.BlockSpec(down, wa),
                      pl.BlockSpec(up, wb), pl.BlockSpec(up, wb), pl.BlockSpec(down, wb)],
            out_specs=pl.BlockSpec((MOE_BLK, D_MODEL), lambda j, ea, eb, n: (j, 0)),
        ),
        out_shape=jax.ShapeDtypeStruct((P, D_MODEL), f32),
        compiler_params=_cparams(("arbitrary",)),
        name="experts",
    )(blk_ea, blk_eb, n_used, xs, w1, w3, w2, w1, w3, w2)


def _combine_kernel(dest_ref, x1_ref, g_ref, y_ref, o_ref, buf_ref, sems):
    i = pl.program_id(0)
    n = pl.num_programs(0) - 1
    slot = i % 2

    @pl.when(i < n)
    def _():
        for r in range(IDX_TILE):
            d = dest_ref[0, 0, r]
            pltpu.make_async_copy(y_ref.at[pl.ds(d, 1), :], buf_ref.at[slot, pl.ds(r, 1), :], sems.at[slot]).start()

    @pl.when(i >= 1)
    def _():
        prev = 1 - slot
        pltpu.make_async_copy(y_ref.at[pl.ds(0, IDX_TILE), :], buf_ref.at[prev], sems.at[prev]).wait()
        x = x1_ref[...] + buf_ref[prev]
        y = x * lax.rsqrt(jnp.mean(x * x, axis=-1, keepdims=True) + EPS)
        o_ref[...] = y * g_ref[...]


def _combine_call(dest, x1, g, y):
    T = x1.shape[0]
    n = T // IDX_TILE
    done = lambda i: (jnp.maximum(i - 1, 0), 0)
    return pl.pallas_call(
        _combine_kernel,
        grid=(n + 1,),
        in_specs=[pl.BlockSpec((1, 1, IDX_TILE), lambda i: (jnp.minimum(i, n - 1), 0, 0), memory_space=pltpu.SMEM),
                  pl.BlockSpec((IDX_TILE, D_MODEL), done),
                  pl.BlockSpec((1, D_MODEL), lambda i: (0, 0)),
                  pl.BlockSpec(memory_space=pl.ANY)],
        out_specs=pl.BlockSpec((IDX_TILE, D_MODEL), done),
        scratch_shapes=[pltpu.VMEM((2, IDX_TILE, D_MODEL), f32), pltpu.SemaphoreType.DMA((2,))],
        out_shape=jax.ShapeDtypeStruct((T, D_MODEL), f32),
        compiler_params=_cparams(("arbitrary",)),
        name="combine",
    )(dest, x1, g, y)


def _pair_table():
    pa, pb = [], []
    for a in range(EXPERTS_PER_GROUP):
        for b in range(a + 1, EXPERTS_PER_GROUP):
            pa.append(a)
            pb.append(b)
    return np.asarray(pa, np.int32), np.asarray(pb, np.int32)


def _route_plan(counts, T):
    nb = T // MOE_BLK + N_CLASSES
    cnt = counts[0, :N_CLASSES].astype(jnp.int32)
    nblk = (cnt + MOE_BLK - 1) // MOE_BLK
    cum = jnp.cumsum(nblk)
    pstart = jnp.pad((cum - nblk) * MOE_BLK, (0, LANES - N_CLASSES)).astype(jnp.int32)
    blk = jnp.minimum(jnp.arange(nb, dtype=jnp.int32), cum[-1] - 1)
    blk_cls = jnp.minimum(jnp.sum((cum[None, :] <= blk[:, None]).astype(jnp.int32), axis=1), N_CLASSES - 1)
    pa, pb = _pair_table()
    grp = blk_cls // N_PAIRS
    pr = blk_cls % N_PAIRS
    blk_ea = grp * EXPERTS_PER_GROUP + jnp.asarray(pa)[pr]
    blk_eb = grp * EXPERTS_PER_GROUP + jnp.asarray(pb)[pr]
    n_used = cum[-1:].astype(jnp.int32)
    lastblk = jnp.pad(jnp.concatenate([jnp.where(cnt > 0, cum - 1, -1), cum[-1:], jnp.sum(cnt > 0)[None]]),
                      (0, LANES - N_CLASSES - 2)).astype(jnp.int32)
    return pstart, lastblk, blk_ea.astype(jnp.int32), blk_eb.astype(jnp.int32), n_used, nb * MOE_BLK


ATTN_SCORE_ELEMS = 1 << 20


def _trunk(x, lam, wts, tabs, tm=512, tq=None):
    B, S, _ = x.shape
    tq = tq or min(512, ATTN_SCORE_ELEMS // S)
    T = B * S
    x2d = x.reshape(T, D_MODEL)
    tabs_a, tabs_r = tabs
    qa, ka, va, qr, kr, vr, gr, ga, gb, qn2, kn2 = _inproj_call(
        x2d, S, wts['attn_norm_g'], wts['w_in'], [t[:S] for t in tabs_a], [t[:S] for t in tabs_r], tm)
    oa = _attn_call(lam, qa, ka, va, qn2, kn2, wts['subln_g'], B, S, tq)
    ob = _ret_call(wts['ret_decay_f'], wts['ret_decay_b'], qr, kr, vr, gr, B, S)
    x1, rows, meta, counts = _outproj_call(
        oa, ob, ga, gb, x2d, wts['w_pa'], wts['w_pb'], wts['w_out'], wts['ffn_norm_g'],
        wts['wr_hi'], wts['wr_lo'], wts['b_r'], tm)
    pstart, lastblk, blk_ea, blk_eb, n_used, P = _route_plan(counts, T)
    dest = _dest_call(pstart, meta)
    xs = _dispatch_call(lastblk, dest, rows, P)
    y = _expert_call(blk_ea, blk_eb, n_used, xs, wts['w1'], wts['w3'], wts['w2'])
    out = _combine_call(dest, x1, wts['final_norm_g'], y)
    return out.reshape(B, S, D_MODEL)


def kernel(x_prompt, x_sample, attn_norm_g, w_in, lam_q1, lam_k1, lam_q2, lam_k2, subln_g,
           ret_decay_f, ret_decay_b, w_pa, w_pb, w_out, ffn_norm_g, w_rg, b_rg, w_re, b_re,
           w1, w3, w2, final_norm_g):
    wr = jnp.concatenate([w_rg[0], jnp.transpose(w_re[0], (1, 0, 2)).reshape(D_MODEL, N_EXPERTS)], axis=1)
    wr = jnp.pad(wr, ((0, 0), (0, LANES - wr.shape[1])))
    wr_hi = wr.astype(bf16)
    b_r = jnp.pad(jnp.concatenate([b_rg[0], b_re[0].reshape(-1)]), (0, LANES - N_GROUPS - N_EXPERTS))
    wts = dict(
        attn_norm_g=attn_norm_g[0][None, :], w_in=w_in[0].astype(bf16), subln_g=subln_g[0][None, :],
        ret_decay_f=ret_decay_f[0], ret_decay_b=ret_decay_b[0],
        w_pa=w_pa[0].astype(bf16), w_pb=w_pb[0].astype(bf16), w_out=w_out[0].astype(bf16),
        ffn_norm_g=ffn_norm_g[0][None, :], wr_hi=wr_hi, wr_lo=(wr - wr_hi.astype(f32)).astype(bf16),
        b_r=b_r[None, :], w1=w1[0].astype(bf16), w3=w3[0].astype(bf16), w2=w2[0].astype(bf16),
        final_norm_g=final_norm_g[None, :])
    lam = _lam_call(lam_q1, lam_k1, lam_q2, lam_k2)[0, :1]
    s_max = max(x_prompt.shape[1], x_sample.shape[1])
    tabs = (_rot_tables(s_max, A_ROT, A_THETA, A_HD), _rot_tables(s_max, R_DK, R_THETA, R_DK))
    return (_trunk(x_prompt, lam, wts, tabs), _trunk(x_sample, lam, wts, tabs))
```

```python
import functools
import math

import numpy as np
import jax
import jax.numpy as jnp
from jax import lax
from jax.experimental import pallas as pl
from jax.experimental.pallas import tpu as pltpu

f32 = jnp.float32
bf16 = jnp.bfloat16

D_MODEL = 1024
A_HEADS = 4
A_HD = 64
A_VD = 128
A_ROT = 16
A_THETA = 500000.0
R_HEADS = 4
R_DK = 64
R_DV = 128
R_CHUNK = 128
R_THETA = 10000.0
N_GROUPS = 4
EXPERTS_PER_GROUP = 8
N_EXPERTS = 32
D_EXPERT = 512
EPS = 1e-6
LAM_INIT = 0.8 - 0.6 * math.exp(-0.3 * 0)

A_QK_W = 512
A_V_W = 512
R_QK_W = 256
R_V_W = 512
D_IN = 5120
OFF_QA, OFF_KA, OFF_VA, OFF_QR, OFF_KR, OFF_VR, OFF_GR, OFF_GA, OFF_GB = (
    0, 512, 1024, 1536, 1792, 2048, 2560, 3072, 4096)

LANES = 128
N_PAIRS = 28
N_CLASSES = N_GROUPS * N_PAIRS
ROW_W = D_MODEL + LANES
MOE_BLK = 128
IDX_TILE = 256
DEST_TILES = 128
RET_UNROLL = 4
VMEM_LIMIT = 56 * 1024 * 1024


def _cparams(sem, **kw):
    return pltpu.CompilerParams(dimension_semantics=sem, vmem_limit_bytes=VMEM_LIMIT, **kw)


def _lam_kernel(q1_ref, k1_ref, q2_ref, k2_ref, o_ref):
    a = jnp.sum(q1_ref[...] * k1_ref[...], axis=-1, keepdims=True)
    b = jnp.sum(q2_ref[...] * k2_ref[...], axis=-1, keepdims=True)
    lam = jnp.exp(a) - jnp.exp(b) + LAM_INIT
    o_ref[...] = jnp.broadcast_to(lam, o_ref.shape)


def _lam_call(q1, k1, q2, k2):
    return pl.pallas_call(
        _lam_kernel,
        out_shape=jax.ShapeDtypeStruct((8, LANES), f32),
        name="lam",
    )(q1, k1, q2, k2)


def _rotary(z, c, s1, s2, shift):
    outs = []
    for j in range(z.shape[1] // LANES):
        zz = z[:, j * LANES:(j + 1) * LANES]
        outs.append(zz * c + pltpu.roll(zz, LANES - shift, 1) * s1 + pltpu.roll(zz, shift, 1) * s2)
    return jnp.concatenate(outs, axis=1)


def _inproj_kernel(x_ref, g_ref, w_ref, ca_ref, sa1_ref, sa2_ref, cr_ref, sr1_ref, sr2_ref,
                   qa_ref, ka_ref, va_ref, qr_ref, kr_ref, vr_ref, gr_ref, ga_ref, gb_ref,
                   qn_ref, kn_ref, *, nps):
    x = x_ref[...]
    xn = x * lax.rsqrt(jnp.mean(x * x, axis=-1, keepdims=True) + EPS)
    xn = (xn * g_ref[...]).astype(bf16)

    def proj(off, width):
        return jnp.dot(xn, w_ref[:, off:off + width], preferred_element_type=f32)

    ca, sa1, sa2 = ca_ref[...], sa1_ref[...], sa2_ref[...]
    cr, sr1, sr2 = cr_ref[...], sr1_ref[...], sr2_ref[...]
    qa = (_rotary(proj(OFF_QA, A_QK_W), ca, sa1, sa2, A_ROT // 2) * (A_HD ** -0.5)).astype(bf16)
    ka = _rotary(proj(OFF_KA, A_QK_W), ca, sa1, sa2, A_ROT // 2).astype(bf16)
    qa_ref[...] = qa
    ka_ref[...] = ka

    seg = (lax.broadcasted_iota(jnp.int32, (A_QK_W, LANES), 0) // A_HD
           == lax.broadcasted_iota(jnp.int32, (A_QK_W, LANES), 1))
    ind = jnp.where(seg, 1.0, 0.0).astype(bf16)

    def norm2max(z):
        n2 = jnp.dot(z * z, ind, preferred_element_type=f32)
        return jnp.max(n2, axis=0, keepdims=True)

    qmax, kmax = norm2max(qa), norm2max(ka)
    first = pl.program_id(0) % nps == 0

    @pl.when(first)
    def _():
        qn_ref[0] = qmax
        kn_ref[0] = kmax

    @pl.when(jnp.logical_not(first))
    def _():
        qn_ref[0] = jnp.maximum(qn_ref[0], qmax)
        kn_ref[0] = jnp.maximum(kn_ref[0], kmax)

    va_ref[...] = proj(OFF_VA, A_V_W).astype(bf16)
    qr_ref[...] = _rotary(proj(OFF_QR, R_QK_W), cr, sr1, sr2, R_DK // 2)
    kr_ref[...] = _rotary(proj(OFF_KR, R_QK_W), cr, sr1, sr2, R_DK // 2) * (R_DK ** -0.5)
    vr_ref[...] = proj(OFF_VR, R_V_W).astype(bf16)
    gr = proj(OFF_GR, R_V_W)
    gr_ref[...] = (gr * jax.nn.sigmoid(gr)).astype(bf16)
    ga_ref[...] = jax.nn.sigmoid(proj(OFF_GA, D_MODEL)).astype(bf16)
    gb_ref[...] = jax.nn.sigmoid(proj(OFF_GB, D_MODEL)).astype(bf16)


def _rot_tables(S, rot_dim, theta, head_dim):
    inv = theta ** (-jnp.arange(0, rot_dim, 2, dtype=f32) / rot_dim)
    pos = jnp.arange(S, dtype=f32)
    ang = pos[:, None] * inv[None, :]
    cos, sin = jnp.cos(ang), jnp.sin(ang)
    half = rot_dim // 2
    pad = head_dim - rot_dim
    c = jnp.concatenate([cos, cos, jnp.ones((S, pad), f32)], axis=1)
    s1 = jnp.concatenate([-sin, jnp.zeros((S, half + pad), f32)], axis=1)
    s2 = jnp.concatenate([jnp.zeros((S, half), f32), sin, jnp.zeros((S, pad), f32)], axis=1)
    rep = LANES // head_dim
    return tuple(jnp.tile(t, (1, rep)) for t in (c, s1, s2))


def _inproj_call(x2d, S, g, w_in_bf, tabs_a, tabs_r, tm):
    T = x2d.shape[0]
    nps = S // tm
    row = lambda i: (i, 0)
    posmap = lambda i: (i % nps, 0)
    const = lambda i: (0, 0)
    tab_spec = pl.BlockSpec((tm, LANES), posmap)
    outs = [
        (A_QK_W, bf16), (A_QK_W, bf16), (A_V_W, bf16),
        (R_QK_W, f32), (R_QK_W, f32), (R_V_W, bf16), (R_V_W, bf16),
        (D_MODEL, bf16), (D_MODEL, bf16),
    ]
    nrm_spec = pl.BlockSpec((1, 1, LANES), lambda i: (i // nps, 0, 0))
    nrm_shape = jax.ShapeDtypeStruct((T // S, 1, LANES), f32)
    return pl.pallas_call(
        functools.partial(_inproj_kernel, nps=nps),
        grid=(T // tm,),
        in_specs=[pl.BlockSpec((tm, D_MODEL), row),
                  pl.BlockSpec((1, D_MODEL), const),
                  pl.BlockSpec((D_MODEL, D_IN), const)] + [tab_spec] * 6,
        out_specs=[pl.BlockSpec((tm, w), row) for w, _ in outs] + [nrm_spec, nrm_spec],
        out_shape=[jax.ShapeDtypeStruct((T, w), dt) for w, dt in outs] + [nrm_shape, nrm_shape],
        compiler_params=_cparams(("arbitrary",)),
        name="inproj",
    )(x2d, g, w_in_bf, *tabs_a, *tabs_r)


def _attn_kernel(kmax_ref, safe_ref, lam_ref, q_ref, k_ref, v_ref, g_ref, o_ref, vaug_ref, *, tq):
    b, h, qi = pl.program_id(0), pl.program_id(1), pl.program_id(2)
    S = v_ref.shape[0]

    @pl.when(qi == 0)
    def _():
        lane = lax.broadcasted_iota(jnp.int32, (S, LANES), 1)
        vaug_ref[:, :LANES] = v_ref[...]
        vaug_ref[:, LANES:] = jnp.where(lane == 0, 1.0, 0.0).astype(bf16)

    def stacked_q():
        q = q_ref[...]
        lane = lax.broadcasted_iota(jnp.int32, q.shape, 1)
        zero = jnp.zeros_like(q)
        return jnp.concatenate([jnp.where(lane < A_HD, q, zero), jnp.where(lane >= A_HD, q, zero)], axis=0)

    def scores(qs):
        return lax.dot_general(qs, k_ref[...], (((1,), (1,)), ((), ())), preferred_element_type=f32)

    def finish(e):
        oa = jnp.dot(e.astype(bf16), vaug_ref[...], preferred_element_type=f32)
        o = oa[:, :LANES] / oa[:, LANES:LANES + 1]
        a = o[:tq] - lam_ref[0] * o[tq:]
        y = a * lax.rsqrt(jnp.mean(a * a, axis=-1, keepdims=True) + EPS)
        o_ref[...] = ((y * g_ref[...]) * (1.0 - LAM_INIT)).astype(o_ref.dtype)

    safe = safe_ref[b * A_HEADS + h] == 1

    @pl.when(safe)
    def _():
        qs = stacked_q()
        qf = qs.astype(f32)
        qn = jnp.sqrt(jnp.sum(qf * qf, axis=-1, keepdims=True))
        row = lax.broadcasted_iota(jnp.int32, qn.shape, 0)
        kbase = (b * A_HEADS + h) * 2
        shift = qn * jnp.where(row < tq, kmax_ref[kbase], kmax_ref[kbase + 1])
        finish(jnp.exp(scores(qs) - shift))

    @pl.when(jnp.logical_not(safe))
    def _():
        s = scores(stacked_q())
        finish(jnp.exp(s - jnp.max(s, axis=-1, keepdims=True)))


ATTN_SAFE_BOUND = 40.0


def _attn_call(lam, qa, ka, va, qn2, kn2, subln_g, B, S, tq):
    T = qa.shape[0]
    nq = S // tq
    nsub = 2 * A_HEADS
    qmax = jnp.sqrt(qn2[:, 0, :nsub])
    kmax = jnp.sqrt(kn2[:, 0, :nsub])
    safe = jnp.all((qmax * kmax).reshape(B, A_HEADS, 2) < ATTN_SAFE_BOUND, axis=-1)
    return pl.pallas_call(
        functools.partial(_attn_kernel, tq=tq),
        grid_spec=pltpu.PrefetchScalarGridSpec(
            num_scalar_prefetch=2,
            grid=(B, A_HEADS, nq),
            in_specs=[pl.BlockSpec(memory_space=pltpu.SMEM),
                      pl.BlockSpec((tq, LANES), lambda b, h, i, km, sf: (b * nq + i, h)),
                      pl.BlockSpec((S, LANES), lambda b, h, i, km, sf: (b, h)),
                      pl.BlockSpec((S, LANES), lambda b, h, i, km, sf: (b, h)),
                      pl.BlockSpec((1, LANES), lambda b, h, i, km, sf: (0, 0))],
            out_specs=pl.BlockSpec((tq, LANES), lambda b, h, i, km, sf: (b * nq + i, h)),
            scratch_shapes=[pltpu.VMEM((S, 2 * LANES), bf16)],
        ),
        out_shape=jax.ShapeDtypeStruct((T, A_V_W), bf16),
        compiler_params=_cparams(("arbitrary", "arbitrary", "arbitrary")),
        name="attn",
    )(kmax.reshape(-1), safe.reshape(-1).astype(jnp.int32), lam, qa, ka, va, subln_g)


def _ret_kernel(df_ref, db_ref, q_ref, k_ref, v_ref, gr_ref, o_ref, uf_ref, ub_ref):
    p = pl.program_id(1)
    C = R_CHUNK
    S = q_ref.shape[0]
    n = S // C
    lane = lax.broadcasted_iota(jnp.int32, (C, LANES), 1)
    rowi = lax.broadcasted_iota(jnp.int32, (C, LANES), 0)
    first = lane < R_DK
    ones = jnp.ones((C, LANES), f32)
    lgf_a, lgf_b = -jnp.exp(ones * df_ref[2 * p]), -jnp.exp(ones * df_ref[2 * p + 1])
    lgb_a, lgb_b = -jnp.exp(ones * db_ref[2 * p]), -jnp.exp(ones * db_ref[2 * p + 1])
    lgf = jnp.where(first, lgf_a, lgf_b)
    lgb = jnp.where(first, lgb_a, lgb_b)
    idx = rowi.astype(f32)
    k_f = jnp.exp(lgf * (C - 1 - idx))
    k_b = jnp.exp(lgb * idx)
    q_f = jnp.exp(lgf * (idx + 1))
    q_b = jnp.exp(lgb * (C - idx))
    top = rowi < R_DK
    g_f = jnp.where(top, jnp.exp(lgf_a * C), jnp.exp(lgf_b * C))
    g_b = jnp.where(top, jnp.exp(lgb_a * C), jnp.exp(lgb_b * C))
    g_f = jnp.concatenate([g_f, g_f], axis=1)
    g_b = jnp.concatenate([g_b, g_b], axis=1)
    rel = (rowi - lane).astype(f32)

    def decay_mask(lf, lb):
        return jnp.where(rel >= 0, jnp.exp(lf * jnp.maximum(rel, 0.0)), jnp.exp(lb * jnp.maximum(-rel, 0.0)))

    mask = jnp.concatenate([decay_mask(lgf_a, lgb_a), decay_mask(lgf_b, lgb_b)], axis=0)

    def summaries(c, carry):
        r0 = pl.multiple_of(c * C, C)
        kc = k_ref[pl.ds(r0, C), :]
        vc = v_ref[pl.ds(r0, C), :]
        kt = jnp.concatenate([(kc * k_f).T, (kc * k_b).T], axis=0).astype(bf16)
        uu = jnp.dot(kt, vc, preferred_element_type=f32)
        uf_ref[c] = uu[:LANES]
        ub_ref[c] = uu[LANES:]
        return carry

    lax.fori_loop(0, n, summaries, 0, unroll=RET_UNROLL)

    def scan_f(c, st):
        u = uf_ref[c]
        uf_ref[c] = st
        return g_f * st + u

    lax.fori_loop(0, n, scan_f, jnp.zeros((LANES, 2 * LANES), f32))

    def scan_b(j, st):
        c = n - 1 - j
        u = ub_ref[c]
        ub_ref[c] = st
        return g_b * st + u

    lax.fori_loop(0, n, scan_b, jnp.zeros((LANES, 2 * LANES), f32))

    def outputs(c, carry):
        r0 = pl.multiple_of(c * C, C)
        qc = q_ref[pl.ds(r0, C), :]
        kc = k_ref[pl.ds(r0, C), :]
        vc = v_ref[pl.ds(r0, C), :]
        zero = jnp.zeros_like(qc)
        q2 = jnp.concatenate([jnp.where(first, qc, zero), jnp.where(first, zero, qc)], axis=0)
        s = lax.dot_general(q2.astype(bf16), kc.astype(bf16), (((1,), (1,)), ((), ())),
                            preferred_element_type=f32)
        qf2 = jnp.concatenate([q_f, q_f], axis=0)
        qb2 = jnp.concatenate([q_b, q_b], axis=0)
        lhs = jnp.concatenate([s * mask, q2 * qf2, q2 * qb2], axis=1).astype(bf16)
        rhs = jnp.concatenate([vc, uf_ref[c].astype(bf16), ub_ref[c].astype(bf16)], axis=0)
        yy = jnp.dot(lhs, rhs, preferred_element_type=f32)
        ya = yy[:C, :LANES]
        yb = yy[C:, LANES:]
        ya = ya * lax.rsqrt(jnp.mean(ya * ya, axis=-1, keepdims=True) + EPS)
        yb = yb * lax.rsqrt(jnp.mean(yb * yb, axis=-1, keepdims=True) + EPS)
        y = jnp.concatenate([ya, yb], axis=1)
        o_ref[pl.ds(r0, C), :] = (gr_ref[pl.ds(r0, C), :].astype(f32) * y).astype(o_ref.dtype)
        return carry

    lax.fori_loop(0, n, outputs, 0, unroll=RET_UNROLL)


def _ret_call(decay_f, decay_b, qr, kr, vr, gr, B, S):
    T = qr.shape[0]
    n = S // R_CHUNK
    blk = lambda w: pl.BlockSpec((S, w), lambda b, p: (b, p))
    return pl.pallas_call(
        _ret_kernel,
        grid=(B, R_HEADS // 2),
        in_specs=[pl.BlockSpec(memory_space=pltpu.SMEM), pl.BlockSpec(memory_space=pltpu.SMEM),
                  blk(LANES), blk(LANES), blk(2 * LANES), blk(2 * LANES)],
        out_specs=blk(2 * LANES),
        out_shape=jax.ShapeDtypeStruct((T, R_V_W), bf16),
        scratch_shapes=[pltpu.VMEM((n, LANES, 2 * LANES), f32), pltpu.VMEM((n, LANES, 2 * LANES), f32)],
        compiler_params=_cparams(("arbitrary", "arbitrary")),
        name="ret",
    )(decay_f, decay_b, qr, kr, vr, gr)


def _outproj_kernel(oa_ref, ob_ref, ga_ref, gb_ref, x_ref, wpa_ref, wpb_ref, wo_ref, g_ref,
                    wr_ref, br_ref,
                    x1_ref, row_ref, meta_ref, cnt_ref, tri_ref, carry_ref, *, tm):
    i = pl.program_id(0)

    @pl.when(i == 0)
    def _():
        r = lax.broadcasted_iota(jnp.int32, (tm, tm), 0)
        c = lax.broadcasted_iota(jnp.int32, (tm, tm), 1)
        tri_ref[...] = jnp.where(c < r, 1.0, 0.0).astype(bf16)
        carry_ref[...] = jnp.zeros_like(carry_ref)

    pa = jnp.dot(oa_ref[...], wpa_ref[...], preferred_element_type=f32)
    pb = jnp.dot(ob_ref[...], wpb_ref[...], preferred_element_type=f32)
    merged = ga_ref[...].astype(f32) * pa + gb_ref[...].astype(f32) * pb
    x1 = x_ref[...] + jnp.dot(merged.astype(bf16), wo_ref[...], preferred_element_type=f32)
    x1_ref[...] = x1
    xn = x1 * lax.rsqrt(jnp.mean(x1 * x1, axis=-1, keepdims=True) + EPS)
    xn = xn * g_ref[...]
    row_ref[:, :D_MODEL] = xn

    xh = xn.astype(bf16)
    xl = (xn - xh.astype(f32)).astype(bf16)
    parts = jnp.dot(jnp.concatenate([xh, xl], axis=0), wr_ref[...], preferred_element_type=f32)
    logits = parts[:tm, :LANES] + parts[:tm, LANES:] + parts[tm:, :LANES] + br_ref[...]
    lane = lax.broadcasted_iota(jnp.int32, (tm, LANES), 1)
    ninf = jnp.full((tm, LANES), -jnp.inf, f32)
    lane_f = lane.astype(f32)
    big = jnp.full((tm, LANES), float(LANES - 1), f32)

    def first_lane(hit):
        return jnp.min(jnp.where(hit, lane_f, big), axis=-1, keepdims=True).astype(jnp.int32)

    isg = lane < N_GROUPS
    lg = jnp.where(isg, logits, ninf)
    mg = jnp.max(lg, axis=-1, keepdims=True)
    gsel = first_lane(lg == mg)
    pg_top = 1.0 / jnp.sum(jnp.where(isg, jnp.exp(logits - mg), 0.0), axis=-1, keepdims=True)
    lo = N_GROUPS + EXPERTS_PER_GROUP * gsel
    le = jnp.where((lane >= lo) & (lane < lo + EXPERTS_PER_GROUP), logits, ninf)
    m1 = jnp.max(le, axis=-1, keepdims=True)
    i1 = first_lane(le == m1)
    le2 = jnp.where(lane == i1, ninf, le)
    m2 = jnp.max(le2, axis=-1, keepdims=True)
    i2 = first_lane(le2 == m2)
    t2 = jnp.exp(m2 - m1)
    gate1 = pg_top / (1.0 + t2)
    gate2 = pg_top * t2 / (1.0 + t2)
    e1 = i1 - lo
    e2 = i2 - lo
    ea = jnp.minimum(e1, e2)
    eb = jnp.maximum(e1, e2)
    gate_a = jnp.where(e1 < e2, gate1, gate2)
    gate_b = jnp.where(e1 < e2, gate2, gate1)
    pair = lax.shift_right_logical(ea * (2 * EXPERTS_PER_GROUP - 1 - ea), 1) + (eb - ea - 1)
    cls = gsel * N_PAIRS + pair
    row_ref[:, D_MODEL:] = jnp.where(lane == 0, gate_a, jnp.where(lane == 1, gate_b, 0.0))

    hit = lane == cls
    onehot = jnp.where(hit, 1.0, 0.0)
    rank = jnp.dot(tri_ref[...], onehot.astype(bf16), preferred_element_type=f32) + carry_ref[...]
    pos = jnp.sum(jnp.where(hit, rank, 0.0), axis=-1, keepdims=True)
    carry_ref[...] = carry_ref[...] + jnp.sum(onehot, axis=0, keepdims=True)
    cnt_ref[...] = jnp.broadcast_to(carry_ref[...], cnt_ref.shape)
    meta = jnp.where(lane == 0, cls.astype(f32), jnp.where(lane == 1, pos, 0.0))
    meta_t = meta.T[:8].astype(jnp.int32)
    for c in range(tm // IDX_TILE):
        meta_ref[c] = meta_t[:, c * IDX_TILE:(c + 1) * IDX_TILE]


def _outproj_call(oa, ob, ga, gb, x2d, wpa, wpb, wo, g, wr, br, tm):
    T = x2d.shape[0]
    row = lambda i: (i, 0)
    const = lambda i: (0, 0)
    return pl.pallas_call(
        functools.partial(_outproj_kernel, tm=tm),
        grid=(T // tm,),
        in_specs=[pl.BlockSpec((tm, A_V_W), row), pl.BlockSpec((tm, R_V_W), row),
                  pl.BlockSpec((tm, D_MODEL), row), pl.BlockSpec((tm, D_MODEL), row),
                  pl.BlockSpec((tm, D_MODEL), row),
                  pl.BlockSpec((A_V_W, D_MODEL), const), pl.BlockSpec((R_V_W, D_MODEL), const),
                  pl.BlockSpec((D_MODEL, D_MODEL), const), pl.BlockSpec((1, D_MODEL), const),
                  pl.BlockSpec((D_MODEL, 2 * LANES), const), pl.BlockSpec((1, LANES), const)],
        out_specs=[pl.BlockSpec((tm, D_MODEL), row), pl.BlockSpec((tm, ROW_W), row),
                   pl.BlockSpec((tm // IDX_TILE, 8, IDX_TILE), lambda i: (i, 0, 0)),
                   pl.BlockSpec((8, LANES), const)],
        out_shape=[jax.ShapeDtypeStruct((T, D_MODEL), f32), jax.ShapeDtypeStruct((T, ROW_W), f32),
                   jax.ShapeDtypeStruct((T // IDX_TILE, 8, IDX_TILE), jnp.int32),
                   jax.ShapeDtypeStruct((8, LANES), f32)],
        scratch_shapes=[pltpu.VMEM((tm, tm), bf16), pltpu.VMEM((1, LANES), f32)],
        compiler_params=_cparams(("arbitrary",)),
        name="outproj",
    )(oa, ob, ga, gb, x2d, wpa, wpb, wo, g, wr, br)


def _dest_kernel(pstart_ref, meta_ref, o_ref):
    cls = meta_ref[:, 0, :]
    rank = meta_ref[:, 1, :]

    def body(c, acc):
        return acc + jnp.where(cls == c, pstart_ref[c], 0)

    o_ref[:, 0, :] = lax.fori_loop(0, N_CLASSES, body, rank)


def _dest_call(pstart, meta):
    nt = meta.shape[0]
    blk = min(nt, DEST_TILES)
    return pl.pallas_call(
        _dest_kernel,
        grid=(nt // blk,),
        in_specs=[pl.BlockSpec(memory_space=pltpu.SMEM),
                  pl.BlockSpec((blk, 8, IDX_TILE), lambda i: (i, 0, 0))],
        out_specs=pl.BlockSpec((blk, 1, IDX_TILE), lambda i: (i, 0, 0)),
        out_shape=jax.ShapeDtypeStruct((nt, 1, IDX_TILE), jnp.int32),
        compiler_params=_cparams(("arbitrary",)),
        name="dest",
    )(pstart, meta)


def _dispatch_kernel(lastblk_ref, dest_ref, rows_ref, xs_ref, buf_ref, zero_ref, sem_in, sem_out, zsem):
    i = pl.program_id(0)
    n = pl.num_programs(0)
    slot = i % 2

    def load(t, s):
        r0 = pl.multiple_of(t * IDX_TILE, IDX_TILE)
        return pltpu.make_async_copy(rows_ref.at[pl.ds(r0, IDX_TILE), :], buf_ref.at[s], sem_in.at[s])

    def scatter_wait(s):
        pltpu.make_async_copy(buf_ref.at[s], xs_ref.at[pl.ds(0, IDX_TILE), :], sem_out.at[s]).wait()

    @pl.when(i == 0)
    def _():
        zero_ref[...] = jnp.zeros_like(zero_ref)

        def zero_block(j):
            off = pl.multiple_of(j * MOE_BLK, MOE_BLK)
            pltpu.make_async_copy(zero_ref, xs_ref.at[pl.ds(off, MOE_BLK), :], zsem).start()

        def zfill_last(c, carry):
            j = lastblk_ref[c]

            @pl.when(j >= 0)
            def _():
                zero_block(j)

            return carry

        def zfill_unused(j, carry):
            zero_block(j)
            return carry

        def zwait(c, carry):
            pltpu.make_async_copy(zero_ref, xs_ref.at[pl.ds(0, MOE_BLK), :], zsem).wait()
            return carry

        n_used = lastblk_ref[N_CLASSES]
        n_blocks = xs_ref.shape[0] // MOE_BLK
        lax.fori_loop(0, N_CLASSES, zfill_last, 0)
        lax.fori_loop(n_used, n_blocks, zfill_unused, 0)
        lax.fori_loop(0, lastblk_ref[N_CLASSES + 1] + n_blocks - n_used, zwait, 0)
        load(0, 0).start()

    @pl.when(i >= 1)
    def _():
        scatter_wait(1 - slot)

    @pl.when(i + 1 < n)
    def _():
        load(i + 1, 1 - slot).start()

    load(i, slot).wait()
    for r in range(IDX_TILE):
        d = dest_ref[0, 0, r]
        pltpu.make_async_copy(buf_ref.at[slot, pl.ds(r, 1), :], xs_ref.at[pl.ds(d, 1), :], sem_out.at[slot]).start()

    @pl.when(i == n - 1)
    def _():
        scatter_wait(slot)


def _dispatch_call(lastblk, dest, rows, P):
    T = rows.shape[0]
    return pl.pallas_call(
        _dispatch_kernel,
        grid=(T // IDX_TILE,),
        in_specs=[pl.BlockSpec(memory_space=pltpu.SMEM),
                  pl.BlockSpec((1, 1, IDX_TILE), lambda i: (i, 0, 0), memory_space=pltpu.SMEM),
                  pl.BlockSpec(memory_space=pl.ANY)],
        out_specs=pl.BlockSpec(memory_space=pl.ANY),
        scratch_shapes=[pltpu.VMEM((2, IDX_TILE, ROW_W), f32), pltpu.VMEM((MOE_BLK, ROW_W), f32),
                        pltpu.SemaphoreType.DMA((2,)), pltpu.SemaphoreType.DMA((2,)),
                        pltpu.SemaphoreType.DMA(())],
        out_shape=jax.ShapeDtypeStruct((P, ROW_W), f32),
        compiler_params=_cparams(("arbitrary",), has_side_effects=True),
        name="dispatch",
    )(lastblk, dest, rows)


def _expert_kernel(ea_ref, eb_ref, nb_ref, xs_ref, w1a_ref, w3a_ref, w2a_ref, w1b_ref, w3b_ref, w2b_ref,
                   y_ref):
    j = pl.program_id(0)

    @pl.when(j < nb_ref[0])
    def _():
        x = xs_ref[:, :D_MODEL].astype(bf16)
        gates = xs_ref[:, D_MODEL:]
        ga = gates[:, 0:1]
        gb = gates[:, 1:2]

        def ffn(w1_ref, w3_ref, w2_ref):
            a = jnp.dot(x, w1_ref[0], preferred_element_type=f32)
            b = jnp.dot(x, w3_ref[0], preferred_element_type=f32)
            h = (a * jax.nn.sigmoid(a)) * b
            return jnp.dot(h.astype(bf16), w2_ref[0], preferred_element_type=f32)

        y_ref[...] = ffn(w1a_ref, w3a_ref, w2a_ref) * ga + ffn(w1b_ref, w3b_ref, w2b_ref) * gb

    @pl.when(j >= nb_ref[0])
    def _():
        y_ref[...] = jnp.zeros_like(y_ref)


def _expert_call(blk_ea, blk_eb, n_used, xs, w1, w3, w2):
    P = xs.shape[0]
    nb = P // MOE_BLK
    wa = lambda j, ea, eb, n: (ea[j], 0, 0)
    wb = lambda j, ea, eb, n: (eb[j], 0, 0)
    up = (1, D_MODEL, D_EXPERT)
    down = (1, D_EXPERT, D_MODEL)
    return pl.pallas_call(
        _expert_kernel,
        grid_spec=pltpu.PrefetchScalarGridSpec(
            num_scalar_prefetch=3,
            grid=(nb,),
            in_specs=[pl.BlockSpec((MOE_BLK, ROW_W), lambda j, ea, eb, n: (jnp.minimum(j, n[0] - 1), 0)),
                      pl.BlockSpec(up, wa), pl.BlockSpec(up, wa), pl.BlockSpec(down, wa),
                      pl.BlockSpec(up, wb), pl.BlockSpec(up, wb), pl.BlockSpec(down, wb)],
            out_specs=pl.BlockSpec((MOE_BLK, D_MODEL), lambda j, ea, eb, n: (j, 0)),
        ),
        out_shape=jax.ShapeDtypeStruct((P, D_MODEL), f32),
        compiler_params=_cparams(("arbitrary",)),
        name="experts",
    )(blk_ea, blk_eb, n_used, xs, w1, w3, w2, w1, w3, w2)


def _combine_kernel(dest_ref, x1_ref, g_ref, y_ref, o_ref, buf_ref, sems):
    i = pl.program_id(0)
    n = pl.num_programs(0) - 1
    slot = i % 2

    @pl.when(i < n)
    def _():
        for r in range(IDX_TILE):
            d = dest_ref[0, 0, r]
            pltpu.make_async_copy(y_ref.at[pl.ds(d, 1), :], buf_ref.at[slot, pl.ds(r, 1), :], sems.at[slot]).start()

    @pl.when(i >= 1)
    def _():
        prev = 1 - slot
        pltpu.make_async_copy(y_ref.at[pl.ds(0, IDX_TILE), :], buf_ref.at[prev], sems.at[prev]).wait()
        x = x1_ref[...] + buf_ref[prev]
        y = x * lax.rsqrt(jnp.mean(x * x, axis=-1, keepdims=True) + EPS)
        o_ref[...] = y * g_ref[...]


def _combine_call(dest, x1, g, y):
    T = x1.shape[0]
    n = T // IDX_TILE
    done = lambda i: (jnp.maximum(i - 1, 0), 0)
    return pl.pallas_call(
        _combine_kernel,
        grid=(n + 1,),
        in_specs=[pl.BlockSpec((1, 1, IDX_TILE), lambda i: (jnp.minimum(i, n - 1), 0, 0), memory_space=pltpu.SMEM),
                  pl.BlockSpec((IDX_TILE, D_MODEL), done),
                  pl.BlockSpec((1, D_MODEL), lambda i: (0, 0)),
                  pl.BlockSpec(memory_space=pl.ANY)],
        out_specs=pl.BlockSpec((IDX_TILE, D_MODEL), done),
        scratch_shapes=[pltpu.VMEM((2, IDX_TILE, D_MODEL), f32), pltpu.SemaphoreType.DMA((2,))],
        out_shape=jax.ShapeDtypeStruct((T, D_MODEL), f32),
        compiler_params=_cparams(("arbitrary",)),
        name="combine",
    )(dest, x1, g, y)


def _pair_table():
    pa, pb = [], []
    for a in range(EXPERTS_PER_GROUP):
        for b in range(a + 1, EXPERTS_PER_GROUP):
            pa.append(a)
            pb.append(b)
    return np.asarray(pa, np.int32), np.asarray(pb, np.int32)


def _route_plan(counts, T):
    nb = T // MOE_BLK + N_CLASSES
    cnt = counts[0, :N_CLASSES].astype(jnp.int32)
    nblk = (cnt + MOE_BLK - 1) // MOE_BLK
    cum = jnp.cumsum(nblk)
    pstart = jnp.pad((cum - nblk) * MOE_BLK, (0, LANES - N_CLASSES)).astype(jnp.int32)
    blk = jnp.minimum(jnp.arange(nb, dtype=jnp.int32), cum[-1] - 1)
    blk_cls = jnp.minimum(jnp.sum((cum[None, :] <= blk[:, None]).astype(jnp.int32), axis=1), N_CLASSES - 1)
    pa, pb = _pair_table()
    grp = blk_cls // N_PAIRS
    pr = blk_cls % N_PAIRS
    blk_ea = grp * EXPERTS_PER_GROUP + jnp.asarray(pa)[pr]
    blk_eb = grp * EXPERTS_PER_GROUP + jnp.asarray(pb)[pr]
    n_used = cum[-1:].astype(jnp.int32)
    lastblk = jnp.pad(jnp.concatenate([jnp.where(cnt > 0, cum - 1, -1), cum[-1:], jnp.sum(cnt > 0)[None]]),
                      (0, LANES - N_CLASSES - 2)).astype(jnp.int32)
    return pstart, lastblk, blk_ea.astype(jnp.int32), blk_eb.astype(jnp.int32), n_used, nb * MOE_BLK


ATTN_SCORE_ELEMS = 1 << 20


def _trunk(x, lam, wts, tabs, tm=512, tq=None):
    B, S, _ = x.shape
    tq = tq or min(512, ATTN_SCORE_ELEMS // S)
    T = B * S
    x2d = x.reshape(T, D_MODEL)
    tabs_a, tabs_r = tabs
    qa, ka, va, qr, kr, vr, gr, ga, gb, qn2, kn2 = _inproj_call(
        x2d, S, wts['attn_norm_g'], wts['w_in'], [t[:S] for t in tabs_a], [t[:S] for t in tabs_r], tm)
    oa = _attn_call(lam, qa, ka, va, qn2, kn2, wts['subln_g'], B, S, tq)
    ob = _ret_call(wts['ret_decay_f'], wts['ret_decay_b'], qr, kr, vr, gr, B, S)
    x1, rows, meta, counts = _outproj_call(
        oa, ob, ga, gb, x2d, wts['w_pa'], wts['w_pb'], wts['w_out'], wts['ffn_norm_g'],
        wts['wr'], wts['b_r'], tm)
    pstart, lastblk, blk_ea, blk_eb, n_used, P = _route_plan(counts, T)
    dest = _dest_call(pstart, meta)
    xs = _dispatch_call(lastblk, dest, rows, P)
    y = _expert_call(blk_ea, blk_eb, n_used, xs, wts['w1'], wts['w3'], wts['w2'])
    out = _combine_call(dest, x1, wts['final_norm_g'], y)
    return out.reshape(B, S, D_MODEL)


def kernel(x_prompt, x_sample, attn_norm_g, w_in, lam_q1, lam_k1, lam_q2, lam_k2, subln_g,
           ret_decay_f, ret_decay_b, w_pa, w_pb, w_out, ffn_norm_g, w_rg, b_rg, w_re, b_re,
           w1, w3, w2, final_norm_g):
    wr = jnp.concatenate([w_rg[0], jnp.transpose(w_re[0], (1, 0, 2)).reshape(D_MODEL, N_EXPERTS)], axis=1)
    wr = jnp.pad(wr, ((0, 0), (0, LANES - wr.shape[1])))
    wr_hi = wr.astype(bf16)
    b_r = jnp.pad(jnp.concatenate([b_rg[0], b_re[0].reshape(-1)]), (0, LANES - N_GROUPS - N_EXPERTS))
    wts = dict(
        attn_norm_g=attn_norm_g[0][None, :], w_in=w_in[0].astype(bf16), subln_g=subln_g[0][None, :],
        ret_decay_f=ret_decay_f[0], ret_decay_b=ret_decay_b[0],
        w_pa=w_pa[0].astype(bf16), w_pb=w_pb[0].astype(bf16), w_out=w_out[0].astype(bf16),
        ffn_norm_g=ffn_norm_g[0][None, :],
        wr=jnp.concatenate([wr_hi, (wr - wr_hi.astype(f32)).astype(bf16)], axis=1),
        b_r=b_r[None, :], w1=w1[0].astype(bf16), w3=w3[0].astype(bf16), w2=w2[0].astype(bf16),
        final_norm_g=final_norm_g[None, :])
    lam = _lam_call(lam_q1, lam_k1, lam_q2, lam_k2)[0, :1]
    s_max = max(x_prompt.shape[1], x_sample.shape[1])
    tabs = (_rot_tables(s_max, A_ROT, A_THETA, A_HD), _rot_tables(s_max, R_DK, R_THETA, R_DK))
    return (_trunk(x_prompt, lam, wts, tabs), _trunk(x_sample, lam, wts, tabs))
```

```python
import functools
import math

import numpy as np
import jax
import jax.numpy as jnp
from jax import lax
from jax.experimental import pallas as pl
from jax.experimental.pallas import tpu as pltpu

f32 = jnp.float32
bf16 = jnp.bfloat16

D_MODEL = 1024
A_HEADS = 4
A_HD = 64
A_VD = 128
A_ROT = 16
A_THETA = 500000.0
R_HEADS = 4
R_DK = 64
R_DV = 128
R_CHUNK = 128
R_THETA = 10000.0
N_GROUPS = 4
EXPERTS_PER_GROUP = 8
N_EXPERTS = 32
D_EXPERT = 512
EPS = 1e-6
LAM_INIT = 0.8 - 0.6 * math.exp(-0.3 * 0)

A_QK_W = 512
A_V_W = 512
R_QK_W = 256
R_V_W = 512
D_IN = 5120
OFF_QA, OFF_KA, OFF_VA, OFF_QR, OFF_KR, OFF_VR, OFF_GR, OFF_GA, OFF_GB = (
    0, 512, 1024, 1536, 1792, 2048, 2560, 3072, 4096)

LANES = 128
N_PAIRS = 28
N_CLASSES = N_GROUPS * N_PAIRS
ROW_W = D_MODEL + LANES
MOE_BLK = 256
IDX_TILE = 256
DEST_TILES = 128
RET_UNROLL = 4
VMEM_LIMIT = 56 * 1024 * 1024


def _cparams(sem, **kw):
    return pltpu.CompilerParams(dimension_semantics=sem, vmem_limit_bytes=VMEM_LIMIT, **kw)


def _lam_kernel(q1_ref, k1_ref, q2_ref, k2_ref, o_ref):
    a = jnp.sum(q1_ref[...] * k1_ref[...], axis=-1, keepdims=True)
    b = jnp.sum(q2_ref[...] * k2_ref[...], axis=-1, keepdims=True)
    lam = jnp.exp(a) - jnp.exp(b) + LAM_INIT
    o_ref[...] = jnp.broadcast_to(lam, o_ref.shape)


def _lam_call(q1, k1, q2, k2):
    return pl.pallas_call(
        _lam_kernel,
        out_shape=jax.ShapeDtypeStruct((8, LANES), f32),
        name="lam",
    )(q1, k1, q2, k2)


def _rotary(z, c, s1, s2, shift):
    outs = []
    for j in range(z.shape[1] // LANES):
        zz = z[:, j * LANES:(j + 1) * LANES]
        outs.append(zz * c + pltpu.roll(zz, LANES - shift, 1) * s1 + pltpu.roll(zz, shift, 1) * s2)
    return jnp.concatenate(outs, axis=1)


def _inproj_kernel(x_ref, g_ref, w_ref, ca_ref, sa1_ref, sa2_ref, cr_ref, sr1_ref, sr2_ref,
                   qa_ref, ka_ref, va_ref, qr_ref, kr_ref, vr_ref, gr_ref, ga_ref, gb_ref,
                   qn_ref, kn_ref, *, nps):
    x = x_ref[...]
    xn = x * lax.rsqrt(jnp.mean(x * x, axis=-1, keepdims=True) + EPS)
    xn = (xn * g_ref[...]).astype(bf16)

    def proj(off, width):
        return jnp.dot(xn, w_ref[:, off:off + width], preferred_element_type=f32)

    ca, sa1, sa2 = ca_ref[...], sa1_ref[...], sa2_ref[...]
    cr, sr1, sr2 = cr_ref[...], sr1_ref[...], sr2_ref[...]
    qa = (_rotary(proj(OFF_QA, A_QK_W), ca, sa1, sa2, A_ROT // 2) * (A_HD ** -0.5)).astype(bf16)
    ka = _rotary(proj(OFF_KA, A_QK_W), ca, sa1, sa2, A_ROT // 2).astype(bf16)
    qa_ref[...] = qa
    ka_ref[...] = ka

    seg = (lax.broadcasted_iota(jnp.int32, (A_QK_W, LANES), 0) // A_HD
           == lax.broadcasted_iota(jnp.int32, (A_QK_W, LANES), 1))
    ind = jnp.where(seg, 1.0, 0.0).astype(bf16)

    def norm2max(z):
        n2 = jnp.dot(z * z, ind, preferred_element_type=f32)
        return jnp.max(n2, axis=0, keepdims=True)

    qmax, kmax = norm2max(qa), norm2max(ka)
    first = pl.program_id(0) % nps == 0

    @pl.when(first)
    def _():
        qn_ref[0] = qmax
        kn_ref[0] = kmax

    @pl.when(jnp.logical_not(first))
    def _():
        qn_ref[0] = jnp.maximum(qn_ref[0], qmax)
        kn_ref[0] = jnp.maximum(kn_ref[0], kmax)

    va_ref[...] = proj(OFF_VA, A_V_W).astype(bf16)
    qr_ref[...] = _rotary(proj(OFF_QR, R_QK_W), cr, sr1, sr2, R_DK // 2)
    kr_ref[...] = _rotary(proj(OFF_KR, R_QK_W), cr, sr1, sr2, R_DK // 2) * (R_DK ** -0.5)
    vr_ref[...] = proj(OFF_VR, R_V_W).astype(bf16)
    gr = proj(OFF_GR, R_V_W)
    gr_ref[...] = (gr * jax.nn.sigmoid(gr)).astype(bf16)
    ga_ref[...] = jax.nn.sigmoid(proj(OFF_GA, D_MODEL)).astype(bf16)
    gb_ref[...] = jax.nn.sigmoid(proj(OFF_GB, D_MODEL)).astype(bf16)


def _rot_tables(S, rot_dim, theta, head_dim):
    inv = theta ** (-jnp.arange(0, rot_dim, 2, dtype=f32) / rot_dim)
    pos = jnp.arange(S, dtype=f32)
    ang = pos[:, None] * inv[None, :]
    cos, sin = jnp.cos(ang), jnp.sin(ang)
    half = rot_dim // 2
    pad = head_dim - rot_dim
    c = jnp.concatenate([cos, cos, jnp.ones((S, pad), f32)], axis=1)
    s1 = jnp.concatenate([-sin, jnp.zeros((S, half + pad), f32)], axis=1)
    s2 = jnp.concatenate([jnp.zeros((S, half), f32), sin, jnp.zeros((S, pad), f32)], axis=1)
    rep = LANES // head_dim
    return tuple(jnp.tile(t, (1, rep)) for t in (c, s1, s2))


def _inproj_call(x2d, S, g, w_in_bf, tabs_a, tabs_r, tm):
    T = x2d.shape[0]
    nps = S // tm
    row = lambda i: (i, 0)
    posmap = lambda i: (i % nps, 0)
    const = lambda i: (0, 0)
    tab_spec = pl.BlockSpec((tm, LANES), posmap)
    outs = [
        (A_QK_W, bf16), (A_QK_W, bf16), (A_V_W, bf16),
        (R_QK_W, f32), (R_QK_W, f32), (R_V_W, bf16), (R_V_W, bf16),
        (D_MODEL, bf16), (D_MODEL, bf16),
    ]
    nrm_spec = pl.BlockSpec((1, 1, LANES), lambda i: (i // nps, 0, 0))
    nrm_shape = jax.ShapeDtypeStruct((T // S, 1, LANES), f32)
    return pl.pallas_call(
        functools.partial(_inproj_kernel, nps=nps),
        grid=(T // tm,),
        in_specs=[pl.BlockSpec((tm, D_MODEL), row),
                  pl.BlockSpec((1, D_MODEL), const),
                  pl.BlockSpec((D_MODEL, D_IN), const)] + [tab_spec] * 6,
        out_specs=[pl.BlockSpec((tm, w), row) for w, _ in outs] + [nrm_spec, nrm_spec],
        out_shape=[jax.ShapeDtypeStruct((T, w), dt) for w, dt in outs] + [nrm_shape, nrm_shape],
        compiler_params=_cparams(("arbitrary",)),
        name="inproj",
    )(x2d, g, w_in_bf, *tabs_a, *tabs_r)


def _attn_kernel(kmax_ref, safe_ref, lam_ref, q_ref, k_ref, v_ref, g_ref, o_ref, vaug_ref, *, tq):
    b, h, qi = pl.program_id(0), pl.program_id(1), pl.program_id(2)
    S = v_ref.shape[0]

    @pl.when(qi == 0)
    def _():
        vaug_ref[:, :LANES] = v_ref[...]
        vaug_ref[:, LANES:] = jnp.ones((S, LANES), bf16)

    def stacked_q():
        q = q_ref[...]
        lane = lax.broadcasted_iota(jnp.int32, q.shape, 1)
        zero = jnp.zeros_like(q)
        return jnp.concatenate([jnp.where(lane < A_HD, q, zero), jnp.where(lane >= A_HD, q, zero)], axis=0)

    def scores(qs):
        return lax.dot_general(qs, k_ref[...], (((1,), (1,)), ((), ())), preferred_element_type=f32)

    def finish(e):
        oa = jnp.dot(e.astype(bf16), vaug_ref[...], preferred_element_type=f32)
        o = oa[:, :LANES] / oa[:, LANES:]
        a = o[:tq] - lam_ref[0] * o[tq:]
        y = a * lax.rsqrt(jnp.mean(a * a, axis=-1, keepdims=True) + EPS)
        o_ref[...] = ((y * g_ref[...]) * (1.0 - LAM_INIT)).astype(o_ref.dtype)

    safe = safe_ref[b * A_HEADS + h] == 1

    @pl.when(safe)
    def _():
        qs = stacked_q()
        qf = qs.astype(f32)
        qn = jnp.sqrt(jnp.sum(qf * qf, axis=-1, keepdims=True))
        row = lax.broadcasted_iota(jnp.int32, qn.shape, 0)
        kbase = (b * A_HEADS + h) * 2
        shift = qn * jnp.where(row < tq, kmax_ref[kbase], kmax_ref[kbase + 1])
        finish(jnp.exp(scores(qs) - shift))

    @pl.when(jnp.logical_not(safe))
    def _():
        s = scores(stacked_q())
        finish(jnp.exp(s - jnp.max(s, axis=-1, keepdims=True)))


ATTN_SAFE_BOUND = 40.0


def _attn_call(lam, qa, ka, va, qn2, kn2, subln_g, B, S, tq):
    T = qa.shape[0]
    nq = S // tq
    nsub = 2 * A_HEADS
    qmax = jnp.sqrt(qn2[:, 0, :nsub])
    kmax = jnp.sqrt(kn2[:, 0, :nsub])
    safe = jnp.all((qmax * kmax).reshape(B, A_HEADS, 2) < ATTN_SAFE_BOUND, axis=-1)
    return pl.pallas_call(
        functools.partial(_attn_kernel, tq=tq),
        grid_spec=pltpu.PrefetchScalarGridSpec(
            num_scalar_prefetch=2,
            grid=(B, A_HEADS, nq),
            in_specs=[pl.BlockSpec(memory_space=pltpu.SMEM),
                      pl.BlockSpec((tq, LANES), lambda b, h, i, km, sf: (b * nq + i, h)),
                      pl.BlockSpec((S, LANES), lambda b, h, i, km, sf: (b, h)),
                      pl.BlockSpec((S, LANES), lambda b, h, i, km, sf: (b, h)),
                      pl.BlockSpec((1, LANES), lambda b, h, i, km, sf: (0, 0))],
            out_specs=pl.BlockSpec((tq, LANES), lambda b, h, i, km, sf: (b * nq + i, h)),
            scratch_shapes=[pltpu.VMEM((S, 2 * LANES), bf16)],
        ),
        out_shape=jax.ShapeDtypeStruct((T, A_V_W), bf16),
        compiler_params=_cparams(("arbitrary", "arbitrary", "arbitrary")),
        name="attn",
    )(kmax.reshape(-1), safe.reshape(-1).astype(jnp.int32), lam, qa, ka, va, subln_g)


def _ret_kernel(df_ref, db_ref, q_ref, k_ref, v_ref, gr_ref, o_ref, uf_ref, ub_ref):
    p = pl.program_id(1)
    C = R_CHUNK
    S = q_ref.shape[0]
    n = S // C
    lane = lax.broadcasted_iota(jnp.int32, (C, LANES), 1)
    rowi = lax.broadcasted_iota(jnp.int32, (C, LANES), 0)
    first = lane < R_DK
    ones = jnp.ones((C, LANES), f32)
    lgf_a, lgf_b = -jnp.exp(ones * df_ref[2 * p]), -jnp.exp(ones * df_ref[2 * p + 1])
    lgb_a, lgb_b = -jnp.exp(ones * db_ref[2 * p]), -jnp.exp(ones * db_ref[2 * p + 1])
    lgf = jnp.where(first, lgf_a, lgf_b)
    lgb = jnp.where(first, lgb_a, lgb_b)
    idx = rowi.astype(f32)
    k_f = jnp.exp(lgf * (C - 1 - idx))
    k_b = jnp.exp(lgb * idx)
    q_f = jnp.exp(lgf * (idx + 1))
    q_b = jnp.exp(lgb * (C - idx))
    top = rowi < R_DK
    g_f = jnp.where(top, jnp.exp(lgf_a * C), jnp.exp(lgf_b * C))
    g_b = jnp.where(top, jnp.exp(lgb_a * C), jnp.exp(lgb_b * C))
    g_f = jnp.concatenate([g_f, g_f], axis=1)
    g_b = jnp.concatenate([g_b, g_b], axis=1)
    rel = (rowi - lane).astype(f32)

    def decay_mask(lf, lb):
        return jnp.where(rel >= 0, jnp.exp(lf * jnp.maximum(rel, 0.0)), jnp.exp(lb * jnp.maximum(-rel, 0.0)))

    mask = jnp.concatenate([decay_mask(lgf_a, lgb_a), decay_mask(lgf_b, lgb_b)], axis=0)

    def summaries(c, carry):
        r0 = pl.multiple_of(c * C, C)
        kc = k_ref[pl.ds(r0, C), :]
        vc = v_ref[pl.ds(r0, C), :]
        kt = jnp.concatenate([(kc * k_f).T, (kc * k_b).T], axis=0).astype(bf16)
        uu = jnp.dot(kt, vc, preferred_element_type=f32)
        uf_ref[c] = uu[:LANES]
        ub_ref[c] = uu[LANES:]
        return carry

    lax.fori_loop(0, n, summaries, 0, unroll=RET_UNROLL)

    def scan_f(c, st):
        u = uf_ref[c]
        uf_ref[c] = st
        return g_f * st + u

    lax.fori_loop(0, n, scan_f, jnp.zeros((LANES, 2 * LANES), f32))

    def scan_b(j, st):
        c = n - 1 - j
        u = ub_ref[c]
        ub_ref[c] = st
        return g_b * st + u

    lax.fori_loop(0, n, scan_b, jnp.zeros((LANES, 2 * LANES), f32))

    def outputs(c, carry):
        r0 = pl.multiple_of(c * C, C)
        qc = q_ref[pl.ds(r0, C), :]
        kc = k_ref[pl.ds(r0, C), :]
        vc = v_ref[pl.ds(r0, C), :]
        zero = jnp.zeros_like(qc)
        q2 = jnp.concatenate([jnp.where(first, qc, zero), jnp.where(first, zero, qc)], axis=0)
        s = lax.dot_general(q2.astype(bf16), kc.astype(bf16), (((1,), (1,)), ((), ())),
                            preferred_element_type=f32)
        qf2 = jnp.concatenate([q_f, q_f], axis=0)
        qb2 = jnp.concatenate([q_b, q_b], axis=0)
        lhs = jnp.concatenate([s * mask, q2 * qf2, q2 * qb2], axis=1).astype(bf16)
        rhs = jnp.concatenate([vc, uf_ref[c].astype(bf16), ub_ref[c].astype(bf16)], axis=0)
        yy = jnp.dot(lhs, rhs, preferred_element_type=f32)
        ya = yy[:C, :LANES]
        yb = yy[C:, LANES:]
        ya = ya * lax.rsqrt(jnp.mean(ya * ya, axis=-1, keepdims=True) + EPS)
        yb = yb * lax.rsqrt(jnp.mean(yb * yb, axis=-1, keepdims=True) + EPS)
        y = jnp.concatenate([ya, yb], axis=1)
        o_ref[pl.ds(r0, C), :] = (gr_ref[pl.ds(r0, C), :].astype(f32) * y).astype(o_ref.dtype)
        return carry

    lax.fori_loop(0, n, outputs, 0, unroll=RET_UNROLL)


def _ret_call(decay_f, decay_b, qr, kr, vr, gr, B, S):
    T = qr.shape[0]
    n = S // R_CHUNK
    blk = lambda w: pl.BlockSpec((S, w), lambda b, p: (b, p))
    return pl.pallas_call(
        _ret_kernel,
        grid=(B, R_HEADS // 2),
        in_specs=[pl.BlockSpec(memory_space=pltpu.SMEM), pl.BlockSpec(memory_space=pltpu.SMEM),
                  blk(LANES), blk(LANES), blk(2 * LANES), blk(2 * LANES)],
        out_specs=blk(2 * LANES),
        out_shape=jax.ShapeDtypeStruct((T, R_V_W), bf16),
        scratch_shapes=[pltpu.VMEM((n, LANES, 2 * LANES), f32), pltpu.VMEM((n, LANES, 2 * LANES), f32)],
        compiler_params=_cparams(("arbitrary", "arbitrary")),
        name="ret",
    )(decay_f, decay_b, qr, kr, vr, gr)


def _outproj_kernel(oa_ref, ob_ref, ga_ref, gb_ref, x_ref, wpa_ref, wpb_ref, wo_ref, g_ref,
                    wr_ref, br_ref, cnt0_ref,
                    x1_ref, row_ref, meta_ref, cnt_ref, tri_ref, carry_ref, *, tm):
    i = pl.program_id(0)

    @pl.when(i == 0)
    def _():
        r = lax.broadcasted_iota(jnp.int32, (tm, tm), 0)
        c = lax.broadcasted_iota(jnp.int32, (tm, tm), 1)
        tri_ref[...] = jnp.where(c < r, 1.0, 0.0).astype(bf16)
        carry_ref[...] = cnt0_ref[0:1, :]

    pa = jnp.dot(oa_ref[...], wpa_ref[...], preferred_element_type=f32)
    pb = jnp.dot(ob_ref[...], wpb_ref[...], preferred_element_type=f32)
    merged = ga_ref[...].astype(f32) * pa + gb_ref[...].astype(f32) * pb
    x1 = x_ref[...] + jnp.dot(merged.astype(bf16), wo_ref[...], preferred_element_type=f32)
    x1_ref[...] = x1
    xn = x1 * lax.rsqrt(jnp.mean(x1 * x1, axis=-1, keepdims=True) + EPS)
    xn = xn * g_ref[...]
    row_ref[:, :D_MODEL] = xn

    xh = xn.astype(bf16)
    xl = (xn - xh.astype(f32)).astype(bf16)
    parts = jnp.dot(jnp.concatenate([xh, xl], axis=0), wr_ref[...], preferred_element_type=f32)
    logits = parts[:tm, :LANES] + parts[:tm, LANES:] + parts[tm:, :LANES] + br_ref[...]
    lane = lax.broadcasted_iota(jnp.int32, (tm, LANES), 1)
    ninf = jnp.full((tm, LANES), -jnp.inf, f32)
    lane_f = lane.astype(f32)
    big = jnp.full((tm, LANES), float(LANES - 1), f32)

    def first_lane(hit):
        return jnp.min(jnp.where(hit, lane_f, big), axis=-1, keepdims=True).astype(jnp.int32)

    isg = lane < N_GROUPS
    lg = jnp.where(isg, logits, ninf)
    mg = jnp.max(lg, axis=-1, keepdims=True)
    gsel = first_lane(lg == mg)
    pg_top = 1.0 / jnp.sum(jnp.where(isg, jnp.exp(logits - mg), 0.0), axis=-1, keepdims=True)
    lo = N_GROUPS + EXPERTS_PER_GROUP * gsel
    le = jnp.where((lane >= lo) & (lane < lo + EXPERTS_PER_GROUP), logits, ninf)
    m1 = jnp.max(le, axis=-1, keepdims=True)
    i1 = first_lane(le == m1)
    le2 = jnp.where(lane == i1, ninf, le)
    m2 = jnp.max(le2, axis=-1, keepdims=True)
    i2 = first_lane(le2 == m2)
    t2 = jnp.exp(m2 - m1)
    gate1 = pg_top / (1.0 + t2)
    gate2 = pg_top * t2 / (1.0 + t2)
    e1 = i1 - lo
    e2 = i2 - lo
    ea = jnp.minimum(e1, e2)
    eb = jnp.maximum(e1, e2)
    gate_a = jnp.where(e1 < e2, gate1, gate2)
    gate_b = jnp.where(e1 < e2, gate2, gate1)
    pair = lax.shift_right_logical(ea * (2 * EXPERTS_PER_GROUP - 1 - ea), 1) + (eb - ea - 1)
    cls = gsel * N_PAIRS + pair
    row_ref[:, D_MODEL:] = jnp.where(lane == 0, gate_a, jnp.where(lane == 1, gate_b, 0.0))

    hit = lane == cls
    onehot = jnp.where(hit, 1.0, 0.0)
    rank = jnp.dot(tri_ref[...], onehot.astype(bf16), preferred_element_type=f32) + carry_ref[...]
    pos = jnp.sum(jnp.where(hit, rank, 0.0), axis=-1, keepdims=True)
    carry_ref[...] = carry_ref[...] + jnp.sum(onehot, axis=0, keepdims=True)
    cnt_ref[...] = jnp.broadcast_to(carry_ref[...], cnt_ref.shape)
    meta = jnp.where(lane == 0, cls.astype(f32), jnp.where(lane == 1, pos, 0.0))
    meta_t = meta.T[:8].astype(jnp.int32)
    for c in range(tm // IDX_TILE):
        meta_ref[c] = meta_t[:, c * IDX_TILE:(c + 1) * IDX_TILE]


def _outproj_call(oa, ob, ga, gb, x2d, wpa, wpb, wo, g, wr, br, cnt0, tm):
    T = x2d.shape[0]
    row = lambda i: (i, 0)
    const = lambda i: (0, 0)
    return pl.pallas_call(
        functools.partial(_outproj_kernel, tm=tm),
        grid=(T // tm,),
        in_specs=[pl.BlockSpec((tm, A_V_W), row), pl.BlockSpec((tm, R_V_W), row),
                  pl.BlockSpec((tm, D_MODEL), row), pl.BlockSpec((tm, D_MODEL), row),
                  pl.BlockSpec((tm, D_MODEL), row),
                  pl.BlockSpec((A_V_W, D_MODEL), const), pl.BlockSpec((R_V_W, D_MODEL), const),
                  pl.BlockSpec((D_MODEL, D_MODEL), const), pl.BlockSpec((1, D_MODEL), const),
                  pl.BlockSpec((D_MODEL, 2 * LANES), const), pl.BlockSpec((1, LANES), const),
                  pl.BlockSpec((8, LANES), const)],
        out_specs=[pl.BlockSpec((tm, D_MODEL), row), pl.BlockSpec((tm, ROW_W), row),
                   pl.BlockSpec((tm // IDX_TILE, 8, IDX_TILE), lambda i: (i, 0, 0)),
                   pl.BlockSpec((8, LANES), const)],
        out_shape=[jax.ShapeDtypeStruct((T, D_MODEL), f32), jax.ShapeDtypeStruct((T, ROW_W), f32),
                   jax.ShapeDtypeStruct((T // IDX_TILE, 8, IDX_TILE), jnp.int32),
                   jax.ShapeDtypeStruct((8, LANES), f32)],
        scratch_shapes=[pltpu.VMEM((tm, tm), bf16), pltpu.VMEM((1, LANES), f32)],
        compiler_params=_cparams(("arbitrary",)),
        name="outproj",
    )(oa, ob, ga, gb, x2d, wpa, wpb, wo, g, wr, br, cnt0)


def _dest_kernel(pstart_ref, meta_ref, o_ref):
    cls = meta_ref[:, 0, :]
    rank = meta_ref[:, 1, :]

    def body(c, acc):
        return acc + jnp.where(cls == c, pstart_ref[c], 0)

    o_ref[:, 0, :] = lax.fori_loop(0, N_CLASSES, body, rank)


def _dest_call(pstart, meta):
    nt = meta.shape[0]
    blk = min(nt, DEST_TILES)
    return pl.pallas_call(
        _dest_kernel,
        grid=(nt // blk,),
        in_specs=[pl.BlockSpec(memory_space=pltpu.SMEM),
                  pl.BlockSpec((blk, 8, IDX_TILE), lambda i: (i, 0, 0))],
        out_specs=pl.BlockSpec((blk, 1, IDX_TILE), lambda i: (i, 0, 0)),
        out_shape=jax.ShapeDtypeStruct((nt, 1, IDX_TILE), jnp.int32),
        compiler_params=_cparams(("arbitrary",)),
        name="dest",
    )(pstart, meta)


def _dispatch_kernel(lastblk_ref, dest_ref, *refs, tiles):
    rows_refs = refs[:len(tiles)]
    xs_ref, buf_ref, zero_ref, sem_in, sem_out, zsem = refs[len(tiles):]
    i = pl.program_id(0)
    n = sum(tiles)
    slot = i % 2

    def tile_copy(rows_ref, t, s):
        r0 = pl.multiple_of(t * IDX_TILE, IDX_TILE)
        return pltpu.make_async_copy(rows_ref.at[pl.ds(r0, IDX_TILE), :], buf_ref.at[s], sem_in.at[s])

    def load_start(t, s):
        t = jnp.asarray(t, jnp.int32)
        first = 0
        for rows_ref, cnt in zip(rows_refs, tiles):
            @pl.when((t >= first) & (t < first + cnt))
            def _(rows_ref=rows_ref, first=first):
                tile_copy(rows_ref, t - first, s).start()
            first += cnt

    def load_wait(s):
        tile_copy(rows_refs[0], 0, s).wait()

    def scatter_wait(s):
        pltpu.make_async_copy(buf_ref.at[s], xs_ref.at[pl.ds(0, IDX_TILE), :], sem_out.at[s]).wait()

    @pl.when(i == 0)
    def _():
        load_start(0, 0)
        zero_ref[...] = jnp.zeros_like(zero_ref)

        def zero_block(j):
            off = pl.multiple_of(j * MOE_BLK, MOE_BLK)
            pltpu.make_async_copy(zero_ref, xs_ref.at[pl.ds(off, MOE_BLK), :], zsem).start()

        def zfill_last(c, carry):
            j = lastblk_ref[c]

            @pl.when(j >= 0)
            def _():
                zero_block(j)

            return carry

        def zfill_unused(j, carry):
            zero_block(j)
            return carry

        def zwait(c, carry):
            pltpu.make_async_copy(zero_ref, xs_ref.at[pl.ds(0, MOE_BLK), :], zsem).wait()
            return carry

        n_used = lastblk_ref[N_CLASSES]
        n_blocks = xs_ref.shape[0] // MOE_BLK
        lax.fori_loop(0, N_CLASSES, zfill_last, 0)
        lax.fori_loop(n_used, n_blocks, zfill_unused, 0)
        lax.fori_loop(0, lastblk_ref[N_CLASSES + 1] + n_blocks - n_used, zwait, 0)

    @pl.when(i >= 1)
    def _():
        scatter_wait(1 - slot)

    @pl.when(i + 1 < n)
    def _():
        load_start(i + 1, 1 - slot)

    load_wait(slot)
    for r in range(IDX_TILE):
        d = dest_ref[0, 0, r]
        pltpu.make_async_copy(buf_ref.at[slot, pl.ds(r, 1), :], xs_ref.at[pl.ds(d, 1), :], sem_out.at[slot]).start()

    @pl.when(i == n - 1)
    def _():
        scatter_wait(slot)


def _dispatch_call(lastblk, dests, rows, P):
    tiles = tuple(r.shape[0] // IDX_TILE for r in rows)
    any_spec = pl.BlockSpec(memory_space=pl.ANY)
    return pl.pallas_call(
        functools.partial(_dispatch_kernel, tiles=tiles),
        grid=(sum(tiles),),
        in_specs=[pl.BlockSpec(memory_space=pltpu.SMEM),
                  pl.BlockSpec((1, 1, IDX_TILE), lambda i: (i, 0, 0), memory_space=pltpu.SMEM)]
                 + [any_spec] * len(rows),
        out_specs=any_spec,
        scratch_shapes=[pltpu.VMEM((2, IDX_TILE, ROW_W), f32), pltpu.VMEM((MOE_BLK, ROW_W), f32),
                        pltpu.SemaphoreType.DMA((2,)), pltpu.SemaphoreType.DMA((2,)),
                        pltpu.SemaphoreType.DMA(())],
        out_shape=jax.ShapeDtypeStruct((P, ROW_W), f32),
        compiler_params=_cparams(("arbitrary",), has_side_effects=True),
        name="dispatch",
    )(lastblk, jnp.concatenate(dests, axis=0), *rows)


def _expert_kernel(ea_ref, eb_ref, nb_ref, xs_ref, w1a_ref, w3a_ref, w2a_ref, w1b_ref, w3b_ref, w2b_ref,
                   y_ref):
    j = pl.program_id(0)

    @pl.when(j < nb_ref[0])
    def _():
        x = xs_ref[:, :D_MODEL].astype(bf16)
        gates = xs_ref[:, D_MODEL:]
        ga = gates[:, 0:1]
        gb = gates[:, 1:2]

        def ffn(w1_ref, w3_ref, w2_ref, gate):
            a = jnp.dot(x, w1_ref[0], preferred_element_type=f32)
            b = jnp.dot(x, w3_ref[0], preferred_element_type=f32)
            h = ((a * jax.nn.sigmoid(a)) * b) * gate
            return jnp.dot(h.astype(bf16), w2_ref[0], preferred_element_type=f32)

        y_ref[...] = ffn(w1a_ref, w3a_ref, w2a_ref, ga) + ffn(w1b_ref, w3b_ref, w2b_ref, gb)

    @pl.when(j >= nb_ref[0])
    def _():
        y_ref[...] = jnp.zeros_like(y_ref)


def _expert_call(blk_ea, blk_eb, n_used, xs, w1, w3, w2):
    P = xs.shape[0]
    nb = P // MOE_BLK
    wa = lambda j, ea, eb, n: (ea[j], 0, 0)
    wb = lambda j, ea, eb, n: (eb[j], 0, 0)
    up = (1, D_MODEL, D_EXPERT)
    down = (1, D_EXPERT, D_MODEL)
    return pl.pallas_call(
        _expert_kernel,
        grid_spec=pltpu.PrefetchScalarGridSpec(
            num_scalar_prefetch=3,
            grid=(nb,),
            in_specs=[pl.BlockSpec((MOE_BLK, ROW_W), lambda j, ea, eb, n: (jnp.minimum(j, n[0] - 1), 0)),
                      pl.BlockSpec(up, wa), pl.BlockSpec(up, wa), pl.BlockSpec(down, wa),
                      pl.BlockSpec(up, wb), pl.BlockSpec(up, wb), pl.BlockSpec(down, wb)],
            out_specs=pl.BlockSpec((MOE_BLK, D_MODEL), lambda j, ea, eb, n: (j, 0)),
        ),
        out_shape=jax.ShapeDtypeStruct((P, D_MODEL), f32),
        compiler_params=_cparams(("arbitrary",)),
        name="experts",
    )(blk_ea, blk_eb, n_used, xs, w1, w3, w2, w1, w3, w2)


def _combine_kernel(dest_ref, x1_ref, g_ref, y_ref, o_ref, buf_ref, sems, *, n):
    i = pl.program_id(0)
    slot = i % 2

    @pl.when(i < n)
    def _():
        for r in range(IDX_TILE):
            d = dest_ref[0, 0, r]
            pltpu.make_async_copy(y_ref.at[pl.ds(d, 1), :], buf_ref.at[slot, pl.ds(r, 1), :], sems.at[slot]).start()

    @pl.when(i >= 1)
    def _():
        prev = 1 - slot
        pltpu.make_async_copy(y_ref.at[pl.ds(0, IDX_TILE), :], buf_ref.at[prev], sems.at[prev]).wait()
        x = x1_ref[...] + buf_ref[prev]
        y = x * lax.rsqrt(jnp.mean(x * x, axis=-1, keepdims=True) + EPS)
        o_ref[...] = y * g_ref[...]


def _combine_call(dest, x1, g, y):
    T = x1.shape[0]
    n = T // IDX_TILE
    done = lambda i: (jnp.maximum(i - 1, 0), 0)
    return pl.pallas_call(
        functools.partial(_combine_kernel, n=n),
        grid=(n + 1,),
        in_specs=[pl.BlockSpec((1, 1, IDX_TILE), lambda i: (jnp.minimum(i, n - 1), 0, 0), memory_space=pltpu.SMEM),
                  pl.BlockSpec((IDX_TILE, D_MODEL), done),
                  pl.BlockSpec((1, D_MODEL), lambda i: (0, 0)),
                  pl.BlockSpec(memory_space=pl.ANY)],
        out_specs=pl.BlockSpec((IDX_TILE, D_MODEL), done),
        scratch_shapes=[pltpu.VMEM((2, IDX_TILE, D_MODEL), f32), pltpu.SemaphoreType.DMA((2,))],
        out_shape=jax.ShapeDtypeStruct((T, D_MODEL), f32),
        compiler_params=_cparams(("arbitrary",)),
        name="combine",
    )(dest, x1, g, y)


def _pair_table():
    pa, pb = [], []
    for a in range(EXPERTS_PER_GROUP):
        for b in range(a + 1, EXPERTS_PER_GROUP):
            pa.append(a)
            pb.append(b)
    return np.asarray(pa, np.int32), np.asarray(pb, np.int32)


def _route_plan(counts, T):
    nb = T // MOE_BLK + N_CLASSES
    cnt = counts[0, :N_CLASSES].astype(jnp.int32)
    nblk = (cnt + MOE_BLK - 1) // MOE_BLK
    cum = jnp.cumsum(nblk)
    pstart = jnp.pad((cum - nblk) * MOE_BLK, (0, LANES - N_CLASSES)).astype(jnp.int32)
    blk = jnp.minimum(jnp.arange(nb, dtype=jnp.int32), cum[-1] - 1)
    blk_cls = jnp.minimum(jnp.sum((cum[None, :] <= blk[:, None]).astype(jnp.int32), axis=1), N_CLASSES - 1)
    pa, pb = _pair_table()
    grp = blk_cls // N_PAIRS
    pr = blk_cls % N_PAIRS
    blk_ea = grp * EXPERTS_PER_GROUP + jnp.asarray(pa)[pr]
    blk_eb = grp * EXPERTS_PER_GROUP + jnp.asarray(pb)[pr]
    n_used = cum[-1:].astype(jnp.int32)
    lastblk = jnp.pad(jnp.concatenate([jnp.where(cnt > 0, cum - 1, -1), cum[-1:], jnp.sum(cnt > 0)[None]]),
                      (0, LANES - N_CLASSES - 2)).astype(jnp.int32)
    return pstart, lastblk, blk_ea.astype(jnp.int32), blk_eb.astype(jnp.int32), n_used, nb * MOE_BLK


ATTN_SCORE_ELEMS = 1 << 21


def _mixer(x, lam, wts, tabs, counts0, tm, tq):
    B, S, _ = x.shape
    tq = tq or min(S, ATTN_SCORE_ELEMS // S)
    T = B * S
    x2d = x.reshape(T, D_MODEL)
    tabs_a, tabs_r = tabs
    qa, ka, va, qr, kr, vr, gr, ga, gb, qn2, kn2 = _inproj_call(
        x2d, S, wts['attn_norm_g'], wts['w_in'], [t[:S] for t in tabs_a], [t[:S] for t in tabs_r], tm)
    oa = _attn_call(lam, qa, ka, va, qn2, kn2, wts['subln_g'], B, S, tq)
    ob = _ret_call(wts['ret_decay_f'], wts['ret_decay_b'], qr, kr, vr, gr, B, S)
    return _outproj_call(oa, ob, ga, gb, x2d, wts['w_pa'], wts['w_pb'], wts['w_out'], wts['ffn_norm_g'],
                         wts['wr'], wts['b_r'], counts0, tm)


def _layer(xs_in, lam, wts, tabs, tm=512, tq=None):
    counts = jnp.zeros((8, LANES), f32)
    mixed = []
    for x in xs_in:
        x1, rows, meta, counts = _mixer(x, lam, wts, tabs, counts, tm, tq)
        mixed.append((x1, rows, meta))
    total = sum(x.shape[0] * x.shape[1] for x in xs_in)
    pstart, lastblk, blk_ea, blk_eb, n_used, P = _route_plan(counts, total)
    dests = [_dest_call(pstart, meta) for _, _, meta in mixed]
    xs = _dispatch_call(lastblk, dests, [rows for _, rows, _ in mixed], P)
    y = _expert_call(blk_ea, blk_eb, n_used, xs, wts['w1'], wts['w3'], wts['w2'])
    return tuple(_combine_call(dest, x1, wts['final_norm_g'], y).reshape(x.shape)
                 for (x1, _, _), dest, x in zip(mixed, dests, xs_in))


def kernel(x_prompt, x_sample, attn_norm_g, w_in, lam_q1, lam_k1, lam_q2, lam_k2, subln_g,
           ret_decay_f, ret_decay_b, w_pa, w_pb, w_out, ffn_norm_g, w_rg, b_rg, w_re, b_re,
           w1, w3, w2, final_norm_g):
    wr = jnp.concatenate([w_rg[0], jnp.transpose(w_re[0], (1, 0, 2)).reshape(D_MODEL, N_EXPERTS)], axis=1)
    wr = jnp.pad(wr, ((0, 0), (0, LANES - wr.shape[1])))
    wr_hi = wr.astype(bf16)
    b_r = jnp.pad(jnp.concatenate([b_rg[0], b_re[0].reshape(-1)]), (0, LANES - N_GROUPS - N_EXPERTS))
    wts = dict(
        attn_norm_g=attn_norm_g[0][None, :], w_in=w_in[0].astype(bf16), subln_g=subln_g[0][None, :],
        ret_decay_f=ret_decay_f[0], ret_decay_b=ret_decay_b[0],
        w_pa=w_pa[0].astype(bf16), w_pb=w_pb[0].astype(bf16), w_out=w_out[0].astype(bf16),
        ffn_norm_g=ffn_norm_g[0][None, :],
        wr=jnp.concatenate([wr_hi, (wr - wr_hi.astype(f32)).astype(bf16)], axis=1),
        b_r=b_r[None, :], w1=w1[0].astype(bf16), w3=w3[0].astype(bf16), w2=w2[0].astype(bf16),
        final_norm_g=final_norm_g[None, :])
    lam = _lam_call(lam_q1, lam_k1, lam_q2, lam_k2)[0, :1]
    s_max = max(x_prompt.shape[1], x_sample.shape[1])
    tabs = (_rot_tables(s_max, A_ROT, A_THETA, A_HD), _rot_tables(s_max, R_DK, R_THETA, R_DK))
    return _layer((x_prompt, x_sample), lam, wts, tabs)
```

```python
import functools
import math

import numpy as np
import jax
import jax.numpy as jnp
from jax import lax
from jax.experimental import pallas as pl
from jax.experimental.pallas import tpu as pltpu

f32 = jnp.float32
bf16 = jnp.bfloat16

D_MODEL = 1024
A_HEADS = 4
A_HD = 64
A_VD = 128
A_ROT = 16
A_THETA = 500000.0
R_HEADS = 4
R_DK = 64
R_DV = 128
R_CHUNK = 128
R_THETA = 10000.0
N_GROUPS = 4
EXPERTS_PER_GROUP = 8
N_EXPERTS = 32
D_EXPERT = 512
EPS = 1e-6
LAM_INIT = 0.8 - 0.6 * math.exp(-0.3 * 0)

A_QK_W = 512
A_V_W = 512
R_QK_W = 256
R_V_W = 512
D_IN = 5120
OFF_QA, OFF_KA, OFF_VA, OFF_QR, OFF_KR, OFF_VR, OFF_GR, OFF_GA, OFF_GB = (
    0, 512, 1024, 1536, 1792, 2048, 2560, 3072, 4096)

LANES = 128
N_PAIRS = 28
N_CLASSES = N_GROUPS * N_PAIRS
ROW_W = D_MODEL + LANES
MOE_BLK = 256
IDX_TILE = 256
DEST_TILES = 128
RET_UNROLL = 8
VMEM_LIMIT = 56 * 1024 * 1024


def _cparams(sem, **kw):
    return pltpu.CompilerParams(dimension_semantics=sem, vmem_limit_bytes=VMEM_LIMIT, **kw)


def _lam_kernel(q1_ref, k1_ref, q2_ref, k2_ref, o_ref):
    a = jnp.sum(q1_ref[...] * k1_ref[...], axis=-1, keepdims=True)
    b = jnp.sum(q2_ref[...] * k2_ref[...], axis=-1, keepdims=True)
    lam = jnp.exp(a) - jnp.exp(b) + LAM_INIT
    o_ref[...] = jnp.broadcast_to(lam, o_ref.shape)


def _lam_call(q1, k1, q2, k2):
    return pl.pallas_call(
        _lam_kernel,
        out_shape=jax.ShapeDtypeStruct((8, LANES), f32),
        name="lam",
    )(q1, k1, q2, k2)


def _rotary(z, c, s1, s2, shift):
    outs = []
    for j in range(z.shape[1] // LANES):
        zz = z[:, j * LANES:(j + 1) * LANES]
        outs.append(zz * c + pltpu.roll(zz, LANES - shift, 1) * s1 + pltpu.roll(zz, shift, 1) * s2)
    return jnp.concatenate(outs, axis=1)


def _inproj_kernel(x_ref, g_ref, w_ref, ca_ref, sa1_ref, sa2_ref, cr_ref, sr1_ref, sr2_ref,
                   qa_ref, ka_ref, va_ref, qr_ref, kr_ref, vr_ref, gr_ref, ga_ref, gb_ref,
                   qn_ref, kn_ref, *, nps):
    x = x_ref[...]
    xn = x * lax.rsqrt(jnp.mean(x * x, axis=-1, keepdims=True) + EPS)
    xn = (xn * g_ref[...]).astype(bf16)

    def proj(off, width):
        return jnp.dot(xn, w_ref[:, off:off + width], preferred_element_type=f32)

    ca, sa1, sa2 = ca_ref[...], sa1_ref[...], sa2_ref[...]
    cr, sr1, sr2 = cr_ref[...], sr1_ref[...], sr2_ref[...]
    qa = (_rotary(proj(OFF_QA, A_QK_W), ca, sa1, sa2, A_ROT // 2) * (A_HD ** -0.5)).astype(bf16)
    ka = _rotary(proj(OFF_KA, A_QK_W), ca, sa1, sa2, A_ROT // 2).astype(bf16)
    qa_ref[...] = qa
    ka_ref[...] = ka

    seg = (lax.broadcasted_iota(jnp.int32, (A_QK_W, LANES), 0) // A_HD
           == lax.broadcasted_iota(jnp.int32, (A_QK_W, LANES), 1))
    ind = jnp.where(seg, 1.0, 0.0).astype(bf16)

    def norm2max(z):
        n2 = jnp.dot(z * z, ind, preferred_element_type=f32)
        return jnp.max(n2, axis=0, keepdims=True)

    qmax, kmax = norm2max(qa), norm2max(ka)
    first = pl.program_id(0) % nps == 0

    @pl.when(first)
    def _():
        qn_ref[0] = qmax
        kn_ref[0] = kmax

    @pl.when(jnp.logical_not(first))
    def _():
        qn_ref[0] = jnp.maximum(qn_ref[0], qmax)
        kn_ref[0] = jnp.maximum(kn_ref[0], kmax)

    va_ref[...] = proj(OFF_VA, A_V_W).astype(bf16)
    qr_ref[...] = _rotary(proj(OFF_QR, R_QK_W), cr, sr1, sr2, R_DK // 2)
    kr_ref[...] = _rotary(proj(OFF_KR, R_QK_W), cr, sr1, sr2, R_DK // 2) * (R_DK ** -0.5)
    vr_ref[...] = proj(OFF_VR, R_V_W).astype(bf16)
    gr = proj(OFF_GR, R_V_W)
    gr_ref[...] = (gr * jax.nn.sigmoid(gr)).astype(bf16)
    ga_ref[...] = jax.nn.sigmoid(proj(OFF_GA, D_MODEL)).astype(bf16)
    gb_ref[...] = jax.nn.sigmoid(proj(OFF_GB, D_MODEL)).astype(bf16)


def _rot_tables(S, rot_dim, theta, head_dim):
    inv = theta ** (-jnp.arange(0, rot_dim, 2, dtype=f32) / rot_dim)
    pos = jnp.arange(S, dtype=f32)
    ang = pos[:, None] * inv[None, :]
    cos, sin = jnp.cos(ang), jnp.sin(ang)
    half = rot_dim // 2
    pad = head_dim - rot_dim
    c = jnp.concatenate([cos, cos, jnp.ones((S, pad), f32)], axis=1)
    s1 = jnp.concatenate([-sin, jnp.zeros((S, half + pad), f32)], axis=1)
    s2 = jnp.concatenate([jnp.zeros((S, half), f32), sin, jnp.zeros((S, pad), f32)], axis=1)
    rep = LANES // head_dim
    return tuple(jnp.tile(t, (1, rep)) for t in (c, s1, s2))


def _inproj_call(x2d, S, g, w_in_bf, tabs_a, tabs_r, tm):
    T = x2d.shape[0]
    nps = S // tm
    row = lambda i: (i, 0)
    posmap = lambda i: (i % nps, 0)
    const = lambda i: (0, 0)
    tab_spec = pl.BlockSpec((tm, LANES), posmap)
    outs = [
        (A_QK_W, bf16), (A_QK_W, bf16), (A_V_W, bf16),
        (R_QK_W, f32), (R_QK_W, f32), (R_V_W, bf16), (R_V_W, bf16),
        (D_MODEL, bf16), (D_MODEL, bf16),
    ]
    nrm_spec = pl.BlockSpec((1, 1, LANES), lambda i: (i // nps, 0, 0))
    nrm_shape = jax.ShapeDtypeStruct((T // S, 1, LANES), f32)
    return pl.pallas_call(
        functools.partial(_inproj_kernel, nps=nps),
        grid=(T // tm,),
        in_specs=[pl.BlockSpec((tm, D_MODEL), row),
                  pl.BlockSpec((1, D_MODEL), const),
                  pl.BlockSpec((D_MODEL, D_IN), const)] + [tab_spec] * 6,
        out_specs=[pl.BlockSpec((tm, w), row) for w, _ in outs] + [nrm_spec, nrm_spec],
        out_shape=[jax.ShapeDtypeStruct((T, w), dt) for w, dt in outs] + [nrm_shape, nrm_shape],
        compiler_params=_cparams(("arbitrary",)),
        name="inproj",
    )(x2d, g, w_in_bf, *tabs_a, *tabs_r)


def _attn_kernel(kmax_ref, safe_ref, lam_ref, q_ref, k_ref, v_ref, g_ref, o_ref, vaug_ref, *, tq):
    b, h, qi = pl.program_id(0), pl.program_id(1), pl.program_id(2)
    S = v_ref.shape[0]

    @pl.when(qi == 0)
    def _():
        vaug_ref[:, :LANES] = v_ref[...]
        vaug_ref[:, LANES:] = jnp.ones((S, LANES), bf16)

    def stacked_q():
        q = q_ref[...]
        lane = lax.broadcasted_iota(jnp.int32, q.shape, 1)
        zero = jnp.zeros_like(q)
        return jnp.concatenate([jnp.where(lane < A_HD, q, zero), jnp.where(lane >= A_HD, q, zero)], axis=0)

    def scores(qs):
        return lax.dot_general(qs, k_ref[...], (((1,), (1,)), ((), ())), preferred_element_type=f32)

    def finish(e):
        oa = jnp.dot(e.astype(bf16), vaug_ref[...], preferred_element_type=f32)
        o = oa[:, :LANES] / oa[:, LANES:]
        a = o[:tq] - lam_ref[0] * o[tq:]
        y = a * lax.rsqrt(jnp.mean(a * a, axis=-1, keepdims=True) + EPS)
        o_ref[...] = ((y * g_ref[...]) * (1.0 - LAM_INIT)).astype(o_ref.dtype)

    safe = safe_ref[b * A_HEADS + h] == 1

    @pl.when(safe)
    def _():
        qs = stacked_q()
        qf = qs.astype(f32)
        qn = jnp.sqrt(jnp.sum(qf * qf, axis=-1, keepdims=True))
        row = lax.broadcasted_iota(jnp.int32, qn.shape, 0)
        kbase = (b * A_HEADS + h) * 2
        shift = qn * jnp.where(row < tq, kmax_ref[kbase], kmax_ref[kbase + 1])
        finish(jnp.exp(scores(qs) - shift))

    @pl.when(jnp.logical_not(safe))
    def _():
        s = scores(stacked_q())
        finish(jnp.exp(s - jnp.max(s, axis=-1, keepdims=True)))


ATTN_SAFE_BOUND = 40.0


def _attn_call(lam, qa, ka, va, qn2, kn2, subln_g, B, S, tq):
    T = qa.shape[0]
    nq = S // tq
    nsub = 2 * A_HEADS
    qmax = jnp.sqrt(qn2[:, 0, :nsub])
    kmax = jnp.sqrt(kn2[:, 0, :nsub])
    safe = jnp.all((qmax * kmax).reshape(B, A_HEADS, 2) < ATTN_SAFE_BOUND, axis=-1)
    return pl.pallas_call(
        functools.partial(_attn_kernel, tq=tq),
        grid_spec=pltpu.PrefetchScalarGridSpec(
            num_scalar_prefetch=2,
            grid=(B, A_HEADS, nq),
            in_specs=[pl.BlockSpec(memory_space=pltpu.SMEM),
                      pl.BlockSpec((tq, LANES), lambda b, h, i, km, sf: (b * nq + i, h)),
                      pl.BlockSpec((S, LANES), lambda b, h, i, km, sf: (b, h)),
                      pl.BlockSpec((S, LANES), lambda b, h, i, km, sf: (b, h)),
                      pl.BlockSpec((1, LANES), lambda b, h, i, km, sf: (0, 0))],
            out_specs=pl.BlockSpec((tq, LANES), lambda b, h, i, km, sf: (b * nq + i, h)),
            scratch_shapes=[pltpu.VMEM((S, 2 * LANES), bf16)],
        ),
        out_shape=jax.ShapeDtypeStruct((T, A_V_W), bf16),
        compiler_params=_cparams(("arbitrary", "arbitrary", "arbitrary")),
        name="attn",
    )(kmax.reshape(-1), safe.reshape(-1).astype(jnp.int32), lam, qa, ka, va, subln_g)


def _ret_kernel(df_ref, db_ref, q_ref, k_ref, v_ref, gr_ref, o_ref, uf_ref, ub_ref):
    p = pl.program_id(1)
    C = R_CHUNK
    S = q_ref.shape[0]
    n = S // C
    lane = lax.broadcasted_iota(jnp.int32, (C, LANES), 1)
    rowi = lax.broadcasted_iota(jnp.int32, (C, LANES), 0)
    first = lane < R_DK
    ones = jnp.ones((C, LANES), f32)
    lgf_a, lgf_b = -jnp.exp(ones * df_ref[2 * p]), -jnp.exp(ones * df_ref[2 * p + 1])
    lgb_a, lgb_b = -jnp.exp(ones * db_ref[2 * p]), -jnp.exp(ones * db_ref[2 * p + 1])
    lgf = jnp.where(first, lgf_a, lgf_b)
    lgb = jnp.where(first, lgb_a, lgb_b)
    idx = rowi.astype(f32)
    k_f = jnp.exp(lgf * (C - 1 - idx))
    k_b = jnp.exp(lgb * idx)
    q_f = jnp.exp(lgf * (idx + 1))
    q_b = jnp.exp(lgb * (C - idx))
    top = rowi < R_DK
    g_f = jnp.where(top, jnp.exp(lgf_a * C), jnp.exp(lgf_b * C))
    g_b = jnp.where(top, jnp.exp(lgb_a * C), jnp.exp(lgb_b * C))
    g_f = jnp.concatenate([g_f, g_f], axis=1)
    g_b = jnp.concatenate([g_b, g_b], axis=1)
    rel = (rowi - lane).astype(f32)

    def decay_mask(lf, lb):
        return jnp.where(rel >= 0, jnp.exp(lf * jnp.maximum(rel, 0.0)), jnp.exp(lb * jnp.maximum(-rel, 0.0)))

    mask = jnp.concatenate([decay_mask(lgf_a, lgb_a), decay_mask(lgf_b, lgb_b)], axis=0)

    def summaries(c, carry):
        r0 = pl.multiple_of(c * C, C)
        kc = k_ref[pl.ds(r0, C), :]
        vc = v_ref[pl.ds(r0, C), :]
        kt = jnp.concatenate([(kc * k_f).T, (kc * k_b).T], axis=0).astype(bf16)
        uu = jnp.dot(kt, vc, preferred_element_type=f32)
        uf_ref[c] = uu[:LANES]
        ub_ref[c] = uu[LANES:]
        return carry

    lax.fori_loop(0, n, summaries, 0, unroll=RET_UNROLL)

    def scan_f(c, st):
        u = uf_ref[c]
        uf_ref[c] = st
        return g_f * st + u

    lax.fori_loop(0, n, scan_f, jnp.zeros((LANES, 2 * LANES), f32))

    def scan_b(j, st):
        c = n - 1 - j
        u = ub_ref[c]
        ub_ref[c] = st
        return g_b * st + u

    lax.fori_loop(0, n, scan_b, jnp.zeros((LANES, 2 * LANES), f32))

    def outputs(c, carry):
        r0 = pl.multiple_of(c * C, C)
        qc = q_ref[pl.ds(r0, C), :]
        kc = k_ref[pl.ds(r0, C), :]
        vc = v_ref[pl.ds(r0, C), :]
        zero = jnp.zeros_like(qc)
        q2 = jnp.concatenate([jnp.where(first, qc, zero), jnp.where(first, zero, qc)], axis=0)
        s = lax.dot_general(q2.astype(bf16), kc.astype(bf16), (((1,), (1,)), ((), ())),
                            preferred_element_type=f32)
        qf2 = jnp.concatenate([q_f, q_f], axis=0)
        qb2 = jnp.concatenate([q_b, q_b], axis=0)
        lhs = jnp.concatenate([s * mask, q2 * qf2, q2 * qb2], axis=1).astype(bf16)
        rhs = jnp.concatenate([vc, uf_ref[c].astype(bf16), ub_ref[c].astype(bf16)], axis=0)
        yy = jnp.dot(lhs, rhs, preferred_element_type=f32)
        ya = yy[:C, :LANES]
        yb = yy[C:, LANES:]
        ya = ya * lax.rsqrt(jnp.mean(ya * ya, axis=-1, keepdims=True) + EPS)
        yb = yb * lax.rsqrt(jnp.mean(yb * yb, axis=-1, keepdims=True) + EPS)
        y = jnp.concatenate([ya, yb], axis=1)
        o_ref[pl.ds(r0, C), :] = (gr_ref[pl.ds(r0, C), :].astype(f32) * y).astype(o_ref.dtype)
        return carry

    lax.fori_loop(0, n, outputs, 0, unroll=RET_UNROLL)


def _ret_call(decay_f, decay_b, qr, kr, vr, gr, B, S):
    T = qr.shape[0]
    n = S // R_CHUNK
    blk = lambda w: pl.BlockSpec((S, w), lambda b, p: (b, p))
    return pl.pallas_call(
        _ret_kernel,
        grid=(B, R_HEADS // 2),
        in_specs=[pl.BlockSpec(memory_space=pltpu.SMEM), pl.BlockSpec(memory_space=pltpu.SMEM),
                  blk(LANES), blk(LANES), blk(2 * LANES), blk(2 * LANES)],
        out_specs=blk(2 * LANES),
        out_shape=jax.ShapeDtypeStruct((T, R_V_W), bf16),
        scratch_shapes=[pltpu.VMEM((n, LANES, 2 * LANES), f32), pltpu.VMEM((n, LANES, 2 * LANES), f32)],
        compiler_params=_cparams(("arbitrary", "arbitrary")),
        name="ret",
    )(decay_f, decay_b, qr, kr, vr, gr)


def _outproj_kernel(oa_ref, ob_ref, ga_ref, gb_ref, x_ref, wpa_ref, wpb_ref, wo_ref, g_ref,
                    wr_ref, br_ref, cnt0_ref,
                    x1_ref, row_ref, meta_ref, cnt_ref, tri_ref, carry_ref, *, tm):
    i = pl.program_id(0)

    @pl.when(i == 0)
    def _():
        r = lax.broadcasted_iota(jnp.int32, (tm, tm), 0)
        c = lax.broadcasted_iota(jnp.int32, (tm, tm), 1)
        tri_ref[...] = jnp.where(r < c, 1.0, 0.0).astype(bf16)
        carry_ref[...] = cnt0_ref[...]

    pa = jnp.dot(oa_ref[...], wpa_ref[...], preferred_element_type=f32)
    pb = jnp.dot(ob_ref[...], wpb_ref[...], preferred_element_type=f32)
    merged = ga_ref[...].astype(f32) * pa + gb_ref[...].astype(f32) * pb
    x1 = x_ref[...] + jnp.dot(merged.astype(bf16), wo_ref[...], preferred_element_type=f32)
    x1_ref[...] = x1
    xn = x1 * lax.rsqrt(jnp.mean(x1 * x1, axis=-1, keepdims=True) + EPS)
    xn = xn * g_ref[...]
    row_ref[:, :D_MODEL] = xn

    xh = xn.astype(bf16)
    xl = (xn - xh.astype(f32)).astype(bf16)
    parts = jnp.dot(jnp.concatenate([xh, xl], axis=0), wr_ref[...], preferred_element_type=f32)
    logits = parts[:tm, :LANES] + parts[:tm, LANES:] + parts[tm:, :LANES] + br_ref[...]

    lt = logits.T
    row = lax.broadcasted_iota(jnp.int32, (LANES, tm), 0)
    row_f = row.astype(f32)
    ninf = jnp.full((LANES, tm), -jnp.inf, f32)
    big = jnp.full((LANES, tm), float(LANES - 1), f32)

    def first_row(hit):
        return jnp.min(jnp.where(hit, row_f, big), axis=0, keepdims=True).astype(jnp.int32)

    isg = row < N_GROUPS
    lg = jnp.where(isg, lt, ninf)
    mg = jnp.max(lg, axis=0, keepdims=True)
    gsel = first_row(lg == mg)
    pg_top = 1.0 / jnp.sum(jnp.where(isg, jnp.exp(lt - mg), 0.0), axis=0, keepdims=True)
    lo = N_GROUPS + EXPERTS_PER_GROUP * gsel
    le = jnp.where((row >= lo) & (row < lo + EXPERTS_PER_GROUP), lt, ninf)
    m1 = jnp.max(le, axis=0, keepdims=True)
    i1 = first_row(le == m1)
    le2 = jnp.where(row == i1, ninf, le)
    m2 = jnp.max(le2, axis=0, keepdims=True)
    i2 = first_row(le2 == m2)
    t2 = jnp.exp(m2 - m1)
    gate1 = pg_top / (1.0 + t2)
    gate2 = pg_top * t2 / (1.0 + t2)
    e1 = i1 - lo
    e2 = i2 - lo
    ea = jnp.minimum(e1, e2)
    eb = jnp.maximum(e1, e2)
    gate_a = jnp.where(e1 < e2, gate1, gate2)
    gate_b = jnp.where(e1 < e2, gate2, gate1)
    pair = lax.shift_right_logical(ea * (2 * EXPERTS_PER_GROUP - 1 - ea), 1) + (eb - ea - 1)
    cls = gsel * N_PAIRS + pair
    gates_t = jnp.where(row == 0, gate_a, jnp.where(row == 1, gate_b, 0.0))
    row_ref[:, D_MODEL:] = gates_t.T

    hit = row == cls
    onehot = jnp.where(hit, 1.0, 0.0)
    carry = jnp.concatenate([carry_ref[...]] * (tm // LANES), axis=1)
    rank = jnp.dot(onehot.astype(bf16), tri_ref[...], preferred_element_type=f32) + carry
    pos = jnp.sum(jnp.where(hit, rank, 0.0), axis=0, keepdims=True).astype(jnp.int32)
    carry_ref[...] = carry_ref[...] + jnp.sum(onehot, axis=1, keepdims=True)
    cnt_ref[...] = carry_ref[...]
    row8 = lax.broadcasted_iota(jnp.int32, (8, tm), 0)
    meta = jnp.where(row8 == 0, cls, jnp.where(row8 == 1, pos, 0))
    for c in range(tm // IDX_TILE):
        meta_ref[c] = meta[:, c * IDX_TILE:(c + 1) * IDX_TILE]


def _outproj_call(oa, ob, ga, gb, x2d, wpa, wpb, wo, g, wr, br, cnt0, tm):
    T = x2d.shape[0]
    row = lambda i: (i, 0)
    const = lambda i: (0, 0)
    return pl.pallas_call(
        functools.partial(_outproj_kernel, tm=tm),
        grid=(T // tm,),
        in_specs=[pl.BlockSpec((tm, A_V_W), row), pl.BlockSpec((tm, R_V_W), row),
                  pl.BlockSpec((tm, D_MODEL), row), pl.BlockSpec((tm, D_MODEL), row),
                  pl.BlockSpec((tm, D_MODEL), row),
                  pl.BlockSpec((A_V_W, D_MODEL), const), pl.BlockSpec((R_V_W, D_MODEL), const),
                  pl.BlockSpec((D_MODEL, D_MODEL), const), pl.BlockSpec((1, D_MODEL), const),
                  pl.BlockSpec((D_MODEL, 2 * LANES), const), pl.BlockSpec((1, LANES), const),
                  pl.BlockSpec((LANES, LANES), const)],
        out_specs=[pl.BlockSpec((tm, D_MODEL), row), pl.BlockSpec((tm, ROW_W), row),
                   pl.BlockSpec((tm // IDX_TILE, 8, IDX_TILE), lambda i: (i, 0, 0)),
                   pl.BlockSpec((LANES, LANES), const)],
        out_shape=[jax.ShapeDtypeStruct((T, D_MODEL), f32), jax.ShapeDtypeStruct((T, ROW_W), f32),
                   jax.ShapeDtypeStruct((T // IDX_TILE, 8, IDX_TILE), jnp.int32),
                   jax.ShapeDtypeStruct((LANES, LANES), f32)],
        scratch_shapes=[pltpu.VMEM((tm, tm), bf16), pltpu.VMEM((LANES, LANES), f32)],
        compiler_params=_cparams(("arbitrary",)),
        name="outproj",
    )(oa, ob, ga, gb, x2d, wpa, wpb, wo, g, wr, br, cnt0)


def _dest_kernel(pstart_ref, meta_ref, o_ref):
    cls = meta_ref[:, 0, :]
    rank = meta_ref[:, 1, :]

    def body(c, acc):
        return acc + jnp.where(cls == c, pstart_ref[c], 0)

    o_ref[:, 0, :] = lax.fori_loop(0, N_CLASSES, body, rank)


def _dest_call(pstart, meta):
    nt = meta.shape[0]
    blk = min(nt, DEST_TILES)
    return pl.pallas_call(
        _dest_kernel,
        grid=(nt // blk,),
        in_specs=[pl.BlockSpec(memory_space=pltpu.SMEM),
                  pl.BlockSpec((blk, 8, IDX_TILE), lambda i: (i, 0, 0))],
        out_specs=pl.BlockSpec((blk, 1, IDX_TILE), lambda i: (i, 0, 0)),
        out_shape=jax.ShapeDtypeStruct((nt, 1, IDX_TILE), jnp.int32),
        compiler_params=_cparams(("arbitrary",)),
        name="dest",
    )(pstart, meta)


def _dispatch_kernel(lastblk_ref, dest_ref, *refs, tiles):
    rows_refs = refs[:len(tiles)]
    xs_ref, buf_ref, zero_ref, sem_in, sem_out, zsem = refs[len(tiles):]
    i = pl.program_id(0)
    n = sum(tiles)
    slot = i % 2

    def tile_copy(rows_ref, t, s):
        r0 = pl.multiple_of(t * IDX_TILE, IDX_TILE)
        return pltpu.make_async_copy(rows_ref.at[pl.ds(r0, IDX_TILE), :], buf_ref.at[s], sem_in.at[s])

    def load_start(t, s):
        t = jnp.asarray(t, jnp.int32)
        first = 0
        for rows_ref, cnt in zip(rows_refs, tiles):
            @pl.when((t >= first) & (t < first + cnt))
            def _(rows_ref=rows_ref, first=first):
                tile_copy(rows_ref, t - first, s).start()
            first += cnt

    def load_wait(s):
        tile_copy(rows_refs[0], 0, s).wait()

    def scatter_wait(s):
        pltpu.make_async_copy(buf_ref.at[s], xs_ref.at[pl.ds(0, IDX_TILE), :], sem_out.at[s]).wait()

    @pl.when(i == 0)
    def _():
        load_start(0, 0)
        zero_ref[...] = jnp.zeros_like(zero_ref)

        def zero_block(j):
            off = pl.multiple_of(j * MOE_BLK, MOE_BLK)
            pltpu.make_async_copy(zero_ref, xs_ref.at[pl.ds(off, MOE_BLK), :], zsem).start()

        def zfill_last(c, carry):
            j = lastblk_ref[c]

            @pl.when(j >= 0)
            def _():
                zero_block(j)

            return carry

        def zfill_unused(j, carry):
            zero_block(j)
            return carry

        def zwait(c, carry):
            pltpu.make_async_copy(zero_ref, xs_ref.at[pl.ds(0, MOE_BLK), :], zsem).wait()
            return carry

        n_used = lastblk_ref[N_CLASSES]
        n_blocks = xs_ref.shape[0] // MOE_BLK
        lax.fori_loop(0, N_CLASSES, zfill_last, 0)
        lax.fori_loop(n_used, n_blocks, zfill_unused, 0)
        lax.fori_loop(0, lastblk_ref[N_CLASSES + 1] + n_blocks - n_used, zwait, 0)

    @pl.when(i >= 1)
    def _():
        scatter_wait(1 - slot)

    @pl.when(i + 1 < n)
    def _():
        load_start(i + 1, 1 - slot)

    load_wait(slot)
    for r in range(IDX_TILE):
        d = dest_ref[0, 0, r]
        pltpu.make_async_copy(buf_ref.at[slot, pl.ds(r, 1), :], xs_ref.at[pl.ds(d, 1), :], sem_out.at[slot]).start()

    @pl.when(i == n - 1)
    def _():
        scatter_wait(slot)


def _dispatch_call(lastblk, dests, rows, P):
    tiles = tuple(r.shape[0] // IDX_TILE for r in rows)
    any_spec = pl.BlockSpec(memory_space=pl.ANY)
    return pl.pallas_call(
        functools.partial(_dispatch_kernel, tiles=tiles),
        grid=(sum(tiles),),
        in_specs=[pl.BlockSpec(memory_space=pltpu.SMEM),
                  pl.BlockSpec((1, 1, IDX_TILE), lambda i: (i, 0, 0), memory_space=pltpu.SMEM)]
                 + [any_spec] * len(rows),
        out_specs=any_spec,
        scratch_shapes=[pltpu.VMEM((2, IDX_TILE, ROW_W), f32), pltpu.VMEM((MOE_BLK, ROW_W), f32),
                        pltpu.SemaphoreType.DMA((2,)), pltpu.SemaphoreType.DMA((2,)),
                        pltpu.SemaphoreType.DMA(())],
        out_shape=jax.ShapeDtypeStruct((P, ROW_W), f32),
        compiler_params=_cparams(("arbitrary",), has_side_effects=True),
        name="dispatch",
    )(lastblk, jnp.concatenate(dests, axis=0), *rows)


def _expert_kernel(ea_ref, eb_ref, nb_ref, xs_ref, w1a_ref, w3a_ref, w2a_ref, w1b_ref, w3b_ref, w2b_ref,
                   y_ref):
    j = pl.program_id(0)

    @pl.when(j < nb_ref[0])
    def _():
        x = xs_ref[:, :D_MODEL].astype(bf16)
        gates = xs_ref[:, D_MODEL:]
        ga = gates[:, 0:1]
        gb = gates[:, 1:2]

        def ffn(w1_ref, w3_ref, w2_ref, gate):
            a = jnp.dot(x, w1_ref[0], preferred_element_type=f32)
            b = jnp.dot(x, w3_ref[0], preferred_element_type=f32)
            h = ((a * jax.nn.sigmoid(a)) * b) * gate
            return jnp.dot(h.astype(bf16), w2_ref[0], preferred_element_type=f32)

        y_ref[...] = ffn(w1a_ref, w3a_ref, w2a_ref, ga) + ffn(w1b_ref, w3b_ref, w2b_ref, gb)

    @pl.when(j >= nb_ref[0])
    def _():
        y_ref[...] = jnp.zeros_like(y_ref)


def _expert_call(blk_ea, blk_eb, n_used, xs, w1, w3, w2):
    P = xs.shape[0]
    nb = P // MOE_BLK
    wa = lambda j, ea, eb, n: (ea[j], 0, 0)
    wb = lambda j, ea, eb, n: (eb[j], 0, 0)
    up = (1, D_MODEL, D_EXPERT)
    down = (1, D_EXPERT, D_MODEL)
    return pl.pallas_call(
        _expert_kernel,
        grid_spec=pltpu.PrefetchScalarGridSpec(
            num_scalar_prefetch=3,
            grid=(nb,),
            in_specs=[pl.BlockSpec((MOE_BLK, ROW_W), lambda j, ea, eb, n: (jnp.minimum(j, n[0] - 1), 0)),
                      pl.BlockSpec(up, wa), pl.BlockSpec(up, wa), pl.BlockSpec(down, wa),
                      pl.BlockSpec(up, wb), pl.BlockSpec(up, wb), pl.BlockSpec(down, wb)],
            out_specs=pl.BlockSpec((MOE_BLK, D_MODEL), lambda j, ea, eb, n: (j, 0)),
        ),
        out_shape=jax.ShapeDtypeStruct((P, D_MODEL), f32),
        compiler_params=_cparams(("arbitrary",)),
        name="experts",
    )(blk_ea, blk_eb, n_used, xs, w1, w3, w2, w1, w3, w2)


def _combine_kernel(dest_ref, x1_ref, g_ref, y_ref, o_ref, buf_ref, sems, *, n):
    i = pl.program_id(0)
    slot = i % 2

    @pl.when(i < n)
    def _():
        for r in range(IDX_TILE):
            d = dest_ref[0, 0, r]
            pltpu.make_async_copy(y_ref.at[pl.ds(d, 1), :], buf_ref.at[slot, pl.ds(r, 1), :], sems.at[slot]).start()

    @pl.when(i >= 1)
    def _():
        prev = 1 - slot
        pltpu.make_async_copy(y_ref.at[pl.ds(0, IDX_TILE), :], buf_ref.at[prev], sems.at[prev]).wait()
        x = x1_ref[...] + buf_ref[prev]
        y = x * lax.rsqrt(jnp.mean(x * x, axis=-1, keepdims=True) + EPS)
        o_ref[...] = y * g_ref[...]


def _combine_call(dest, x1, g, y):
    T = x1.shape[0]
    n = T // IDX_TILE
    done = lambda i: (jnp.maximum(i - 1, 0), 0)
    return pl.pallas_call(
        functools.partial(_combine_kernel, n=n),
        grid=(n + 1,),
        in_specs=[pl.BlockSpec((1, 1, IDX_TILE), lambda i: (jnp.minimum(i, n - 1), 0, 0), memory_space=pltpu.SMEM),
                  pl.BlockSpec((IDX_TILE, D_MODEL), done),
                  pl.BlockSpec((1, D_MODEL), lambda i: (0, 0)),
                  pl.BlockSpec(memory_space=pl.ANY)],
        out_specs=pl.BlockSpec((IDX_TILE, D_MODEL), done),
        scratch_shapes=[pltpu.VMEM((2, IDX_TILE, D_MODEL), f32), pltpu.SemaphoreType.DMA((2,))],
        out_shape=jax.ShapeDtypeStruct((T, D_MODEL), f32),
        compiler_params=_cparams(("arbitrary",)),
        name="combine",
    )(dest, x1, g, y)


def _pair_table():
    pa, pb = [], []
    for a in range(EXPERTS_PER_GROUP):
        for b in range(a + 1, EXPERTS_PER_GROUP):
            pa.append(a)
            pb.append(b)
    return np.asarray(pa, np.int32), np.asarray(pb, np.int32)


def _route_plan(counts, T):
    nb = T // MOE_BLK + N_CLASSES
    cnt = counts[:N_CLASSES, 0].astype(jnp.int32)
    nblk = (cnt + MOE_BLK - 1) // MOE_BLK
    cum = jnp.cumsum(nblk)
    pstart = jnp.pad((cum - nblk) * MOE_BLK, (0, LANES - N_CLASSES)).astype(jnp.int32)
    blk = jnp.minimum(jnp.arange(nb, dtype=jnp.int32), cum[-1] - 1)
    blk_cls = jnp.minimum(jnp.sum((cum[None, :] <= blk[:, None]).astype(jnp.int32), axis=1), N_CLASSES - 1)
    pa, pb = _pair_table()
    grp = blk_cls // N_PAIRS
    pr = blk_cls % N_PAIRS
    blk_ea = grp * EXPERTS_PER_GROUP + jnp.asarray(pa)[pr]
    blk_eb = grp * EXPERTS_PER_GROUP + jnp.asarray(pb)[pr]
    n_used = cum[-1:].astype(jnp.int32)
    lastblk = jnp.pad(jnp.concatenate([jnp.where(cnt > 0, cum - 1, -1), cum[-1:], jnp.sum(cnt > 0)[None]]),
                      (0, LANES - N_CLASSES - 2)).astype(jnp.int32)
    return pstart, lastblk, blk_ea.astype(jnp.int32), blk_eb.astype(jnp.int32), n_used, nb * MOE_BLK


ATTN_SCORE_ELEMS = 1 << 21


def _mixer(x, lam, wts, tabs, counts0, tm, tq):
    B, S, _ = x.shape
    tq = tq or min(S, ATTN_SCORE_ELEMS // S)
    T = B * S
    x2d = x.reshape(T, D_MODEL)
    tabs_a, tabs_r = tabs
    qa, ka, va, qr, kr, vr, gr, ga, gb, qn2, kn2 = _inproj_call(
        x2d, S, wts['attn_norm_g'], wts['w_in'], [t[:S] for t in tabs_a], [t[:S] for t in tabs_r], tm)
    oa = _attn_call(lam, qa, ka, va, qn2, kn2, wts['subln_g'], B, S, tq)
    ob = _ret_call(wts['ret_decay_f'], wts['ret_decay_b'], qr, kr, vr, gr, B, S)
    return _outproj_call(oa, ob, ga, gb, x2d, wts['w_pa'], wts['w_pb'], wts['w_out'], wts['ffn_norm_g'],
                         wts['wr'], wts['b_r'], counts0, tm)


def _layer(xs_in, lam, wts, tabs, tm=512, tq=None):
    counts = jnp.zeros((LANES, LANES), f32)
    mixed = []
    for x in xs_in:
        x1, rows, meta, counts = _mixer(x, lam, wts, tabs, counts, tm, tq)
        mixed.append((x1, rows, meta))
    total = sum(x.shape[0] * x.shape[1] for x in xs_in)
    pstart, lastblk, blk_ea, blk_eb, n_used, P = _route_plan(counts, total)
    dests = [_dest_call(pstart, meta) for _, _, meta in mixed]
    xs = _dispatch_call(lastblk, dests, [rows for _, rows, _ in mixed], P)
    y = _expert_call(blk_ea, blk_eb, n_used, xs, wts['w1'], wts['w3'], wts['w2'])
    return tuple(_combine_call(dest, x1, wts['final_norm_g'], y).reshape(x.shape)
                 for (x1, _, _), dest, x in zip(mixed, dests, xs_in))


def kernel(x_prompt, x_sample, attn_norm_g, w_in, lam_q1, lam_k1, lam_q2, lam_k2, subln_g,
           ret_decay_f, ret_decay_b, w_pa, w_pb, w_out, ffn_norm_g, w_rg, b_rg, w_re, b_re,
           w1, w3, w2, final_norm_g):
    wr = jnp.concatenate([w_rg[0], jnp.transpose(w_re[0], (1, 0, 2)).reshape(D_MODEL, N_EXPERTS)], axis=1)
    wr = jnp.pad(wr, ((0, 0), (0, LANES - wr.shape[1])))
    wr_hi = wr.astype(bf16)
    b_r = jnp.pad(jnp.concatenate([b_rg[0], b_re[0].reshape(-1)]), (0, LANES - N_GROUPS - N_EXPERTS))
    wts = dict(
        attn_norm_g=attn_norm_g[0][None, :], w_in=w_in[0].astype(bf16), subln_g=subln_g[0][None, :],
        ret_decay_f=ret_decay_f[0], ret_decay_b=ret_decay_b[0],
        w_pa=w_pa[0].astype(bf16), w_pb=w_pb[0].astype(bf16), w_out=w_out[0].astype(bf16),
        ffn_norm_g=ffn_norm_g[0][None, :],
        wr=jnp.concatenate([wr_hi, (wr - wr_hi.astype(f32)).astype(bf16)], axis=1),
        b_r=b_r[None, :], w1=w1[0].astype(bf16), w3=w3[0].astype(bf16), w2=w2[0].astype(bf16),
        final_norm_g=final_norm_g[None, :])
    lam = _lam_call(lam_q1, lam_k1, lam_q2, lam_k2)[0, :1]
    s_max = max(x_prompt.shape[1], x_sample.shape[1])
    tabs = (_rot_tables(s_max, A_ROT, A_THETA, A_HD), _rot_tables(s_max, R_DK, R_THETA, R_DK))
    return _layer((x_prompt, x_sample), lam, wts, tabs)
```

```python
import functools
import math

import numpy as np
import jax
import jax.numpy as jnp
from jax import lax
from jax.experimental import pallas as pl
from jax.experimental.pallas import tpu as pltpu

f32 = jnp.float32
bf16 = jnp.bfloat16

D_MODEL = 1024
A_HEADS = 4
A_HD = 64
A_VD = 128
A_ROT = 16
A_THETA = 500000.0
R_HEADS = 4
R_DK = 64
R_DV = 128
R_CHUNK = 128
R_THETA = 10000.0
N_GROUPS = 4
EXPERTS_PER_GROUP = 8
N_EXPERTS = 32
D_EXPERT = 512
EPS = 1e-6
LAM_INIT = 0.8 - 0.6 * math.exp(-0.3 * 0)

A_QK_W = 512
A_V_W = 512
R_QK_W = 256
R_V_W = 512
D_IN = 5120
OFF_QA, OFF_KA, OFF_VA, OFF_QR, OFF_KR, OFF_VR, OFF_GR, OFF_GA, OFF_GB = (
    0, 512, 1024, 1536, 1792, 2048, 2560, 3072, 4096)

LANES = 128
N_PAIRS = 28
N_CLASSES = N_GROUPS * N_PAIRS
ROW_W = D_MODEL + LANES
MOE_BLK = 256
IDX_TILE = 256
DEST_TILES = 128
RET_UNROLL = 8
VMEM_LIMIT = 56 * 1024 * 1024


def _sigmoid(x):
    return 0.5 * jnp.tanh(0.5 * x) + 0.5


def _cparams(sem, **kw):
    return pltpu.CompilerParams(dimension_semantics=sem, vmem_limit_bytes=VMEM_LIMIT, **kw)


def _lam_kernel(q1_ref, k1_ref, q2_ref, k2_ref, o_ref):
    a = jnp.sum(q1_ref[...] * k1_ref[...], axis=-1, keepdims=True)
    b = jnp.sum(q2_ref[...] * k2_ref[...], axis=-1, keepdims=True)
    lam = jnp.exp(a) - jnp.exp(b) + LAM_INIT
    o_ref[...] = jnp.broadcast_to(lam, o_ref.shape)


def _lam_call(q1, k1, q2, k2):
    return pl.pallas_call(
        _lam_kernel,
        out_shape=jax.ShapeDtypeStruct((8, LANES), f32),
        name="lam",
    )(q1, k1, q2, k2)


def _rotary(z, c, s1, s2, shift):
    outs = []
    for j in range(z.shape[1] // LANES):
        zz = z[:, j * LANES:(j + 1) * LANES]
        outs.append(zz * c + pltpu.roll(zz, LANES - shift, 1) * s1 + pltpu.roll(zz, shift, 1) * s2)
    return jnp.concatenate(outs, axis=1)


def _inproj_kernel(x_ref, g_ref, w_ref, ca_ref, sa1_ref, sa2_ref, cr_ref, sr1_ref, sr2_ref,
                   qa_ref, ka_ref, va_ref, qr_ref, kr_ref, vr_ref, gr_ref, ga_ref, gb_ref,
                   qn_ref, kn_ref, *, nps):
    x = x_ref[...]
    xn = x * lax.rsqrt(jnp.mean(x * x, axis=-1, keepdims=True) + EPS)
    xn = (xn * g_ref[...]).astype(bf16)

    def proj(off, width):
        return jnp.dot(xn, w_ref[:, off:off + width], preferred_element_type=f32)

    ca, sa1, sa2 = ca_ref[...], sa1_ref[...], sa2_ref[...]
    cr, sr1, sr2 = cr_ref[...], sr1_ref[...], sr2_ref[...]
    qa = (_rotary(proj(OFF_QA, A_QK_W), ca, sa1, sa2, A_ROT // 2) * (A_HD ** -0.5)).astype(bf16)
    ka = _rotary(proj(OFF_KA, A_QK_W), ca, sa1, sa2, A_ROT // 2).astype(bf16)
    qa_ref[...] = qa
    ka_ref[...] = ka

    seg = (lax.broadcasted_iota(jnp.int32, (A_QK_W, LANES), 0) // A_HD
           == lax.broadcasted_iota(jnp.int32, (A_QK_W, LANES), 1))
    ind = jnp.where(seg, 1.0, 0.0).astype(bf16)

    def norm2max(z):
        n2 = jnp.dot(z * z, ind, preferred_element_type=f32)
        return jnp.max(n2, axis=0, keepdims=True)

    qmax, kmax = norm2max(qa), norm2max(ka)
    first = pl.program_id(0) % nps == 0

    @pl.when(first)
    def _():
        qn_ref[0] = qmax
        kn_ref[0] = kmax

    @pl.when(jnp.logical_not(first))
    def _():
        qn_ref[0] = jnp.maximum(qn_ref[0], qmax)
        kn_ref[0] = jnp.maximum(kn_ref[0], kmax)

    va_ref[...] = proj(OFF_VA, A_V_W).astype(bf16)
    qr_ref[...] = _rotary(proj(OFF_QR, R_QK_W), cr, sr1, sr2, R_DK // 2)
    kr_ref[...] = _rotary(proj(OFF_KR, R_QK_W), cr, sr1, sr2, R_DK // 2) * (R_DK ** -0.5)
    vr_ref[...] = proj(OFF_VR, R_V_W).astype(bf16)
    gr = proj(OFF_GR, R_V_W)
    gr_ref[...] = (gr * _sigmoid(gr)).astype(bf16)
    ga_ref[...] = _sigmoid(proj(OFF_GA, D_MODEL)).astype(bf16)
    gb_ref[...] = _sigmoid(proj(OFF_GB, D_MODEL)).astype(bf16)


def _rot_tables(S, rot_dim, theta, head_dim):
    inv = theta ** (-jnp.arange(0, rot_dim, 2, dtype=f32) / rot_dim)
    pos = jnp.arange(S, dtype=f32)
    ang = pos[:, None] * inv[None, :]
    cos, sin = jnp.cos(ang), jnp.sin(ang)
    half = rot_dim // 2
    pad = head_dim - rot_dim
    c = jnp.concatenate([cos, cos, jnp.ones((S, pad), f32)], axis=1)
    s1 = jnp.concatenate([-sin, jnp.zeros((S, half + pad), f32)], axis=1)
    s2 = jnp.concatenate([jnp.zeros((S, half), f32), sin, jnp.zeros((S, pad), f32)], axis=1)
    rep = LANES // head_dim
    return tuple(jnp.tile(t, (1, rep)) for t in (c, s1, s2))


def _inproj_call(x2d, S, g, w_in_bf, tabs_a, tabs_r, tm):
    T = x2d.shape[0]
    nps = S // tm
    row = lambda i: (i, 0)
    posmap = lambda i: (i % nps, 0)
    const = lambda i: (0, 0)
    tab_spec = pl.BlockSpec((tm, LANES), posmap)
    outs = [
        (A_QK_W, bf16), (A_QK_W, bf16), (A_V_W, bf16),
        (R_QK_W, f32), (R_QK_W, f32), (R_V_W, bf16), (R_V_W, bf16),
        (D_MODEL, bf16), (D_MODEL, bf16),
    ]
    nrm_spec = pl.BlockSpec((1, 1, LANES), lambda i: (i // nps, 0, 0))
    nrm_shape = jax.ShapeDtypeStruct((T // S, 1, LANES), f32)
    return pl.pallas_call(
        functools.partial(_inproj_kernel, nps=nps),
        grid=(T // tm,),
        in_specs=[pl.BlockSpec((tm, D_MODEL), row),
                  pl.BlockSpec((1, D_MODEL), const),
                  pl.BlockSpec((D_MODEL, D_IN), const)] + [tab_spec] * 6,
        out_specs=[pl.BlockSpec((tm, w), row) for w, _ in outs] + [nrm_spec, nrm_spec],
        out_shape=[jax.ShapeDtypeStruct((T, w), dt) for w, dt in outs] + [nrm_shape, nrm_shape],
        compiler_params=_cparams(("arbitrary",)),
        name="inproj",
    )(x2d, g, w_in_bf, *tabs_a, *tabs_r)


def _attn_kernel(kmax_ref, safe_ref, lam_ref, q_ref, k_ref, v_ref, g_ref, o_ref, vaug_ref, *, tq):
    b, h, qi = pl.program_id(0), pl.program_id(1), pl.program_id(2)
    S = v_ref.shape[0]

    @pl.when(qi == 0)
    def _():
        vaug_ref[:, :LANES] = v_ref[...]
        vaug_ref[:, LANES:] = jnp.ones((S, LANES), bf16)

    def stacked_q():
        q = q_ref[...]
        lane = lax.broadcasted_iota(jnp.int32, q.shape, 1)
        zero = jnp.zeros_like(q)
        return jnp.concatenate([jnp.where(lane < A_HD, q, zero), jnp.where(lane >= A_HD, q, zero)], axis=0)

    def scores(qs):
        return lax.dot_general(qs, k_ref[...], (((1,), (1,)), ((), ())), preferred_element_type=f32)

    def finish(e):
        oa = jnp.dot(e.astype(bf16), vaug_ref[...], preferred_element_type=f32)
        o = oa[:, :LANES] / oa[:, LANES:]
        a = o[:tq] - lam_ref[0] * o[tq:]
        y = a * lax.rsqrt(jnp.mean(a * a, axis=-1, keepdims=True) + EPS)
        o_ref[...] = ((y * g_ref[...]) * (1.0 - LAM_INIT)).astype(o_ref.dtype)

    safe = safe_ref[b * A_HEADS + h] == 1

    @pl.when(safe)
    def _():
        qs = stacked_q()
        qf = qs.astype(f32)
        qn = jnp.sqrt(jnp.sum(qf * qf, axis=-1, keepdims=True))
        row = lax.broadcasted_iota(jnp.int32, qn.shape, 0)
        kbase = (b * A_HEADS + h) * 2
        shift = qn * jnp.where(row < tq, kmax_ref[kbase], kmax_ref[kbase + 1])
        finish(jnp.exp(scores(qs) - shift))

    @pl.when(jnp.logical_not(safe))
    def _():
        s = scores(stacked_q())
        finish(jnp.exp(s - jnp.max(s, axis=-1, keepdims=True)))


ATTN_SAFE_BOUND = 40.0


def _attn_call(lam, qa, ka, va, qn2, kn2, subln_g, B, S, tq):
    T = qa.shape[0]
    nq = S // tq
    nsub = 2 * A_HEADS
    qmax = jnp.sqrt(qn2[:, 0, :nsub])
    kmax = jnp.sqrt(kn2[:, 0, :nsub])
    safe = jnp.all((qmax * kmax).reshape(B, A_HEADS, 2) < ATTN_SAFE_BOUND, axis=-1)
    return pl.pallas_call(
        functools.partial(_attn_kernel, tq=tq),
        grid_spec=pltpu.PrefetchScalarGridSpec(
            num_scalar_prefetch=2,
            grid=(B, A_HEADS, nq),
            in_specs=[pl.BlockSpec(memory_space=pltpu.SMEM),
                      pl.BlockSpec((tq, LANES), lambda b, h, i, km, sf: (b * nq + i, h)),
                      pl.BlockSpec((S, LANES), lambda b, h, i, km, sf: (b, h)),
                      pl.BlockSpec((S, LANES), lambda b, h, i, km, sf: (b, h)),
                      pl.BlockSpec((1, LANES), lambda b, h, i, km, sf: (0, 0))],
            out_specs=pl.BlockSpec((tq, LANES), lambda b, h, i, km, sf: (b * nq + i, h)),
            scratch_shapes=[pltpu.VMEM((S, 2 * LANES), bf16)],
        ),
        out_shape=jax.ShapeDtypeStruct((T, A_V_W), bf16),
        compiler_params=_cparams(("arbitrary", "arbitrary", "arbitrary")),
        name="attn",
    )(kmax.reshape(-1), safe.reshape(-1).astype(jnp.int32), lam, qa, ka, va, subln_g)


def _ret_kernel(df_ref, db_ref, q_ref, k_ref, v_ref, gr_ref, o_ref, uf_ref, ub_ref):
    p = pl.program_id(1)
    C = R_CHUNK
    S = q_ref.shape[0]
    n = S // C
    lane = lax.broadcasted_iota(jnp.int32, (C, LANES), 1)
    rowi = lax.broadcasted_iota(jnp.int32, (C, LANES), 0)
    first = lane < R_DK
    ones = jnp.ones((C, LANES), f32)
    lgf_a, lgf_b = -jnp.exp(ones * df_ref[2 * p]), -jnp.exp(ones * df_ref[2 * p + 1])
    lgb_a, lgb_b = -jnp.exp(ones * db_ref[2 * p]), -jnp.exp(ones * db_ref[2 * p + 1])
    lgf = jnp.where(first, lgf_a, lgf_b)
    lgb = jnp.where(first, lgb_a, lgb_b)
    idx = rowi.astype(f32)
    k_f = jnp.exp(lgf * (C - 1 - idx))
    k_b = jnp.exp(lgb * idx)
    q_f = jnp.exp(lgf * (idx + 1))
    q_b = jnp.exp(lgb * (C - idx))
    top = rowi < R_DK
    g_f = jnp.where(top, jnp.exp(lgf_a * C), jnp.exp(lgf_b * C))
    g_b = jnp.where(top, jnp.exp(lgb_a * C), jnp.exp(lgb_b * C))
    g_f = jnp.concatenate([g_f, g_f], axis=1)
    g_b = jnp.concatenate([g_b, g_b], axis=1)
    rel = (rowi - lane).astype(f32)

    def decay_mask(lf, lb):
        return jnp.where(rel >= 0, jnp.exp(lf * jnp.maximum(rel, 0.0)), jnp.exp(lb * jnp.maximum(-rel, 0.0)))

    mask = jnp.concatenate([decay_mask(lgf_a, lgb_a), decay_mask(lgf_b, lgb_b)], axis=0)

    def summaries(c, carry):
        r0 = pl.multiple_of(c * C, C)
        kc = k_ref[pl.ds(r0, C), :]
        vc = v_ref[pl.ds(r0, C), :]
        kt = jnp.concatenate([(kc * k_f).T, (kc * k_b).T], axis=0).astype(bf16)
        uu = jnp.dot(kt, vc, preferred_element_type=f32)
        uf_ref[c] = uu[:LANES]
        ub_ref[c] = uu[LANES:]
        return carry

    lax.fori_loop(0, n, summaries, 0, unroll=RET_UNROLL)

    def scan_f(c, st):
        u = uf_ref[c]
        uf_ref[c] = st
        return g_f * st + u

    lax.fori_loop(0, n, scan_f, jnp.zeros((LANES, 2 * LANES), f32))

    def scan_b(j, st):
        c = n - 1 - j
        u = ub_ref[c]
        ub_ref[c] = st
        return g_b * st + u

    lax.fori_loop(0, n, scan_b, jnp.zeros((LANES, 2 * LANES), f32))

    def outputs(c, carry):
        r0 = pl.multiple_of(c * C, C)
        qc = q_ref[pl.ds(r0, C), :]
        kc = k_ref[pl.ds(r0, C), :]
        vc = v_ref[pl.ds(r0, C), :]
        zero = jnp.zeros_like(qc)
        q2 = jnp.concatenate([jnp.where(first, qc, zero), jnp.where(first, zero, qc)], axis=0)
        s = lax.dot_general(q2.astype(bf16), kc.astype(bf16), (((1,), (1,)), ((), ())),
                            preferred_element_type=f32)
        qf2 = jnp.concatenate([q_f, q_f], axis=0)
        qb2 = jnp.concatenate([q_b, q_b], axis=0)
        lhs = jnp.concatenate([s * mask, q2 * qf2, q2 * qb2], axis=1).astype(bf16)
        rhs = jnp.concatenate([vc, uf_ref[c].astype(bf16), ub_ref[c].astype(bf16)], axis=0)
        yy = jnp.dot(lhs, rhs, preferred_element_type=f32)
        ya = yy[:C, :LANES]
        yb = yy[C:, LANES:]
        ya = ya * lax.rsqrt(jnp.mean(ya * ya, axis=-1, keepdims=True) + EPS)
        yb = yb * lax.rsqrt(jnp.mean(yb * yb, axis=-1, keepdims=True) + EPS)
        y = jnp.concatenate([ya, yb], axis=1)
        o_ref[pl.ds(r0, C), :] = (gr_ref[pl.ds(r0, C), :].astype(f32) * y).astype(o_ref.dtype)
        return carry

    lax.fori_loop(0, n, outputs, 0, unroll=RET_UNROLL)


def _ret_call(decay_f, decay_b, qr, kr, vr, gr, B, S):
    T = qr.shape[0]
    n = S // R_CHUNK
    blk = lambda w: pl.BlockSpec((S, w), lambda b, p: (b, p))
    return pl.pallas_call(
        _ret_kernel,
        grid=(B, R_HEADS // 2),
        in_specs=[pl.BlockSpec(memory_space=pltpu.SMEM), pl.BlockSpec(memory_space=pltpu.SMEM),
                  blk(LANES), blk(LANES), blk(2 * LANES), blk(2 * LANES)],
        out_specs=blk(2 * LANES),
        out_shape=jax.ShapeDtypeStruct((T, R_V_W), bf16),
        scratch_shapes=[pltpu.VMEM((n, LANES, 2 * LANES), f32), pltpu.VMEM((n, LANES, 2 * LANES), f32)],
        compiler_params=_cparams(("arbitrary", "arbitrary")),
        name="ret",
    )(decay_f, decay_b, qr, kr, vr, gr)


def _outproj_kernel(oa_ref, ob_ref, ga_ref, gb_ref, x_ref, wpa_ref, wpb_ref, wo_ref, g_ref,
                    wr_ref, br_ref, cnt0_ref,
                    x1_ref, row_ref, meta_ref, cnt_ref, tri_ref, carry_ref, *, tm):
    i = pl.program_id(0)

    @pl.when(i == 0)
    def _():
        r = lax.broadcasted_iota(jnp.int32, (tm, tm), 0)
        c = lax.broadcasted_iota(jnp.int32, (tm, tm), 1)
        tri_ref[...] = jnp.where(r < c, 1.0, 0.0).astype(bf16)
        carry_ref[...] = cnt0_ref[...]

    pa = jnp.dot(oa_ref[...], wpa_ref[...], preferred_element_type=f32)
    pb = jnp.dot(ob_ref[...], wpb_ref[...], preferred_element_type=f32)
    merged = ga_ref[...].astype(f32) * pa + gb_ref[...].astype(f32) * pb
    x1 = x_ref[...] + jnp.dot(merged.astype(bf16), wo_ref[...], preferred_element_type=f32)
    x1_ref[...] = x1
    xn = x1 * lax.rsqrt(jnp.mean(x1 * x1, axis=-1, keepdims=True) + EPS)
    xn = xn * g_ref[...]
    row_ref[:, :D_MODEL] = xn

    xh = xn.astype(bf16)
    xl = (xn - xh.astype(f32)).astype(bf16)
    parts = jnp.dot(jnp.concatenate([xh, xl], axis=0), wr_ref[...], preferred_element_type=f32)
    logits = parts[:tm, :LANES] + parts[:tm, LANES:] + parts[tm:, :LANES] + br_ref[...]

    lt = logits.T
    row = lax.broadcasted_iota(jnp.int32, (LANES, tm), 0)
    row_f = row.astype(f32)
    ninf = jnp.full((LANES, tm), -jnp.inf, f32)
    big = jnp.full((LANES, tm), float(LANES - 1), f32)

    def first_row(hit):
        return jnp.min(jnp.where(hit, row_f, big), axis=0, keepdims=True).astype(jnp.int32)

    isg = row < N_GROUPS
    lg = jnp.where(isg, lt, ninf)
    mg = jnp.max(lg, axis=0, keepdims=True)
    gsel = first_row(lg == mg)
    pg_top = 1.0 / jnp.sum(jnp.where(isg, jnp.exp(lt - mg), 0.0), axis=0, keepdims=True)
    lo = N_GROUPS + EXPERTS_PER_GROUP * gsel
    le = jnp.where((row >= lo) & (row < lo + EXPERTS_PER_GROUP), lt, ninf)
    m1 = jnp.max(le, axis=0, keepdims=True)
    i1 = first_row(le == m1)
    le2 = jnp.where(row == i1, ninf, le)
    m2 = jnp.max(le2, axis=0, keepdims=True)
    i2 = first_row(le2 == m2)
    t2 = jnp.exp(m2 - m1)
    gate1 = pg_top / (1.0 + t2)
    gate2 = pg_top * t2 / (1.0 + t2)
    e1 = i1 - lo
    e2 = i2 - lo
    ea = jnp.minimum(e1, e2)
    eb = jnp.maximum(e1, e2)
    gate_a = jnp.where(e1 < e2, gate1, gate2)
    gate_b = jnp.where(e1 < e2, gate2, gate1)
    pair = lax.shift_right_logical(ea * (2 * EXPERTS_PER_GROUP - 1 - ea), 1) + (eb - ea - 1)
    cls = gsel * N_PAIRS + pair
    gates_t = jnp.where(row == 0, gate_a, jnp.where(row == 1, gate_b, 0.0))
    row_ref[:, D_MODEL:] = gates_t.T

    hit = row == cls
    onehot = jnp.where(hit, 1.0, 0.0)
    carry = jnp.concatenate([carry_ref[...]] * (tm // LANES), axis=1)
    rank = jnp.dot(onehot.astype(bf16), tri_ref[...], preferred_element_type=f32) + carry
    pos = jnp.sum(jnp.where(hit, rank, 0.0), axis=0, keepdims=True).astype(jnp.int32)
    carry_ref[...] = carry_ref[...] + jnp.sum(onehot, axis=1, keepdims=True)
    cnt_ref[...] = carry_ref[...]
    row8 = lax.broadcasted_iota(jnp.int32, (8, tm), 0)
    meta = jnp.where(row8 == 0, cls, jnp.where(row8 == 1, pos, 0))
    for c in range(tm // IDX_TILE):
        meta_ref[c] = meta[:, c * IDX_TILE:(c + 1) * IDX_TILE]


def _outproj_call(oa, ob, ga, gb, x2d, wpa, wpb, wo, g, wr, br, cnt0, tm):
    T = x2d.shape[0]
    row = lambda i: (i, 0)
    const = lambda i: (0, 0)
    return pl.pallas_call(
        functools.partial(_outproj_kernel, tm=tm),
        grid=(T // tm,),
        in_specs=[pl.BlockSpec((tm, A_V_W), row), pl.BlockSpec((tm, R_V_W), row),
                  pl.BlockSpec((tm, D_MODEL), row), pl.BlockSpec((tm, D_MODEL), row),
                  pl.BlockSpec((tm, D_MODEL), row),
                  pl.BlockSpec((A_V_W, D_MODEL), const), pl.BlockSpec((R_V_W, D_MODEL), const),
                  pl.BlockSpec((D_MODEL, D_MODEL), const), pl.BlockSpec((1, D_MODEL), const),
                  pl.BlockSpec((D_MODEL, 2 * LANES), const), pl.BlockSpec((1, LANES), const),
                  pl.BlockSpec((LANES, LANES), const)],
        out_specs=[pl.BlockSpec((tm, D_MODEL), row), pl.BlockSpec((tm, ROW_W), row),
                   pl.BlockSpec((tm // IDX_TILE, 8, IDX_TILE), lambda i: (i, 0, 0)),
                   pl.BlockSpec((LANES, LANES), const)],
        out_shape=[jax.ShapeDtypeStruct((T, D_MODEL), f32), jax.ShapeDtypeStruct((T, ROW_W), f32),
                   jax.ShapeDtypeStruct((T // IDX_TILE, 8, IDX_TILE), jnp.int32),
                   jax.ShapeDtypeStruct((LANES, LANES), f32)],
        scratch_shapes=[pltpu.VMEM((tm, tm), bf16), pltpu.VMEM((LANES, LANES), f32)],
        compiler_params=_cparams(("arbitrary",)),
        name="outproj",
    )(oa, ob, ga, gb, x2d, wpa, wpb, wo, g, wr, br, cnt0)


def _dest_kernel(pstart_ref, meta_ref, o_ref):
    cls = meta_ref[:, 0, :]
    rank = meta_ref[:, 1, :]

    def body(c, acc):
        return acc + jnp.where(cls == c, pstart_ref[c], 0)

    o_ref[:, 0, :] = lax.fori_loop(0, N_CLASSES, body, rank)


def _dest_call(pstart, meta):
    nt = meta.shape[0]
    blk = min(nt, DEST_TILES)
    return pl.pallas_call(
        _dest_kernel,
        grid=(nt // blk,),
        in_specs=[pl.BlockSpec(memory_space=pltpu.SMEM),
                  pl.BlockSpec((blk, 8, IDX_TILE), lambda i: (i, 0, 0))],
        out_specs=pl.BlockSpec((blk, 1, IDX_TILE), lambda i: (i, 0, 0)),
        out_shape=jax.ShapeDtypeStruct((nt, 1, IDX_TILE), jnp.int32),
        compiler_params=_cparams(("arbitrary",)),
        name="dest",
    )(pstart, meta)


def _dispatch_kernel(lastblk_ref, dest_ref, *refs, tiles):
    rows_refs = refs[:len(tiles)]
    xs_ref, buf_ref, zero_ref, sem_in, sem_out, zsem = refs[len(tiles):]
    i = pl.program_id(0)
    n = sum(tiles)
    slot = i % 2

    def tile_copy(rows_ref, t, s):
        r0 = pl.multiple_of(t * IDX_TILE, IDX_TILE)
        return pltpu.make_async_copy(rows_ref.at[pl.ds(r0, IDX_TILE), :], buf_ref.at[s], sem_in.at[s])

    def load_start(t, s):
        t = jnp.asarray(t, jnp.int32)
        first = 0
        for rows_ref, cnt in zip(rows_refs, tiles):
            @pl.when((t >= first) & (t < first + cnt))
            def _(rows_ref=rows_ref, first=first):
                tile_copy(rows_ref, t - first, s).start()
            first += cnt

    def load_wait(s):
        tile_copy(rows_refs[0], 0, s).wait()

    def scatter_wait(s):
        pltpu.make_async_copy(buf_ref.at[s], xs_ref.at[pl.ds(0, IDX_TILE), :], sem_out.at[s]).wait()

    @pl.when(i == 0)
    def _():
        load_start(0, 0)
        zero_ref[...] = jnp.zeros_like(zero_ref)

        def zero_block(j):
            off = pl.multiple_of(j * MOE_BLK, MOE_BLK)
            pltpu.make_async_copy(zero_ref, xs_ref.at[pl.ds(off, MOE_BLK), :], zsem).start()

        def zfill_last(c, carry):
            j = lastblk_ref[c]

            @pl.when(j >= 0)
            def _():
                zero_block(j)

            return carry

        def zfill_unused(j, carry):
            zero_block(j)
            return carry

        def zwait(c, carry):
            pltpu.make_async_copy(zero_ref, xs_ref.at[pl.ds(0, MOE_BLK), :], zsem).wait()
            return carry

        n_used = lastblk_ref[N_CLASSES]
        n_blocks = xs_ref.shape[0] // MOE_BLK
        lax.fori_loop(0, N_CLASSES, zfill_last, 0)
        lax.fori_loop(n_used, n_blocks, zfill_unused, 0)
        lax.fori_loop(0, lastblk_ref[N_CLASSES + 1] + n_blocks - n_used, zwait, 0)

    @pl.when(i >= 1)
    def _():
        scatter_wait(1 - slot)

    @pl.when(i + 1 < n)
    def _():
        load_start(i + 1, 1 - slot)

    load_wait(slot)
    for r in range(IDX_TILE):
        d = dest_ref[0, 0, r]
        pltpu.make_async_copy(buf_ref.at[slot, pl.ds(r, 1), :], xs_ref.at[pl.ds(d, 1), :],
                              sem_out.at[slot]).start(priority=r % 2)

    @pl.when(i == n - 1)
    def _():
        scatter_wait(slot)


def _dispatch_call(lastblk, dests, rows, P):
    tiles = tuple(r.shape[0] // IDX_TILE for r in rows)
    any_spec = pl.BlockSpec(memory_space=pl.ANY)
    return pl.pallas_call(
        functools.partial(_dispatch_kernel, tiles=tiles),
        grid=(sum(tiles),),
        in_specs=[pl.BlockSpec(memory_space=pltpu.SMEM),
                  pl.BlockSpec((1, 1, IDX_TILE), lambda i: (i, 0, 0), memory_space=pltpu.SMEM)]
                 + [any_spec] * len(rows),
        out_specs=any_spec,
        scratch_shapes=[pltpu.VMEM((2, IDX_TILE, ROW_W), f32), pltpu.VMEM((MOE_BLK, ROW_W), f32),
                        pltpu.SemaphoreType.DMA((2,)), pltpu.SemaphoreType.DMA((2,)),
                        pltpu.SemaphoreType.DMA(())],
        out_shape=jax.ShapeDtypeStruct((P, ROW_W), f32),
        compiler_params=_cparams(("arbitrary",), has_side_effects=True),
        name="dispatch",
    )(lastblk, jnp.concatenate(dests, axis=0), *rows)


def _expert_kernel(ea_ref, eb_ref, nb_ref, xs_ref, w1a_ref, w3a_ref, w2a_ref, w1b_ref, w3b_ref, w2b_ref,
                   y_ref):
    j = pl.program_id(0)

    @pl.when(j < nb_ref[0])
    def _():
        x = xs_ref[:, :D_MODEL].astype(bf16)
        gates = xs_ref[:, D_MODEL:]
        ga = gates[:, 0:1]
        gb = gates[:, 1:2]

        def ffn(w1_ref, w3_ref, w2_ref, gate):
            a = jnp.dot(x, w1_ref[0], preferred_element_type=f32)
            b = jnp.dot(x, w3_ref[0], preferred_element_type=f32)
            h = ((a * _sigmoid(a)) * b) * gate
            return jnp.dot(h.astype(bf16), w2_ref[0], preferred_element_type=f32)

        y_ref[...] = ffn(w1a_ref, w3a_ref, w2a_ref, ga) + ffn(w1b_ref, w3b_ref, w2b_ref, gb)

    @pl.when(j >= nb_ref[0])
    def _():
        y_ref[...] = jnp.zeros_like(y_ref)


def _expert_call(blk_ea, blk_eb, n_used, xs, w1, w3, w2):
    P = xs.shape[0]
    nb = P // MOE_BLK
    wa = lambda j, ea, eb, n: (ea[j], 0, 0)
    wb = lambda j, ea, eb, n: (eb[j], 0, 0)
    up = (1, D_MODEL, D_EXPERT)
    down = (1, D_EXPERT, D_MODEL)
    return pl.pallas_call(
        _expert_kernel,
        grid_spec=pltpu.PrefetchScalarGridSpec(
            num_scalar_prefetch=3,
            grid=(nb,),
            in_specs=[pl.BlockSpec((MOE_BLK, ROW_W), lambda j, ea, eb, n: (jnp.minimum(j, n[0] - 1), 0)),
                      pl.BlockSpec(up, wa), pl.BlockSpec(up, wa), pl.BlockSpec(down, wa),
                      pl.BlockSpec(up, wb), pl.BlockSpec(up, wb), pl.BlockSpec(down, wb)],
            out_specs=pl.BlockSpec((MOE_BLK, D_MODEL), lambda j, ea, eb, n: (j, 0)),
        ),
        out_shape=jax.ShapeDtypeStruct((P, D_MODEL), f32),
        compiler_params=_cparams(("arbitrary",)),
        name="experts",
    )(blk_ea, blk_eb, n_used, xs, w1, w3, w2, w1, w3, w2)


def _combine_kernel(dest_ref, x1_ref, g_ref, y_ref, o_ref, buf_ref, sems, *, n):
    i = pl.program_id(0)
    slot = i % 2

    @pl.when(i < n)
    def _():
        for r in range(IDX_TILE):
            d = dest_ref[0, 0, r]
            pltpu.make_async_copy(y_ref.at[pl.ds(d, 1), :], buf_ref.at[slot, pl.ds(r, 1), :],
                                  sems.at[slot]).start(priority=r % 2)

    @pl.when(i >= 1)
    def _():
        prev = 1 - slot
        pltpu.make_async_copy(y_ref.at[pl.ds(0, IDX_TILE), :], buf_ref.at[prev], sems.at[prev]).wait()
        x = x1_ref[...] + buf_ref[prev]
        y = x * lax.rsqrt(jnp.mean(x * x, axis=-1, keepdims=True) + EPS)
        o_ref[...] = y * g_ref[...]


def _combine_call(dest, x1, g, y):
    T = x1.shape[0]
    n = T // IDX_TILE
    done = lambda i: (jnp.maximum(i - 1, 0), 0)
    return pl.pallas_call(
        functools.partial(_combine_kernel, n=n),
        grid=(n + 1,),
        in_specs=[pl.BlockSpec((1, 1, IDX_TILE), lambda i: (jnp.minimum(i, n - 1), 0, 0), memory_space=pltpu.SMEM),
                  pl.BlockSpec((IDX_TILE, D_MODEL), done),
                  pl.BlockSpec((1, D_MODEL), lambda i: (0, 0)),
                  pl.BlockSpec(memory_space=pl.ANY)],
        out_specs=pl.BlockSpec((IDX_TILE, D_MODEL), done),
        scratch_shapes=[pltpu.VMEM((2, IDX_TILE, D_MODEL), f32), pltpu.SemaphoreType.DMA((2,))],
        out_shape=jax.ShapeDtypeStruct((T, D_MODEL), f32),
        compiler_params=_cparams(("arbitrary",)),
        name="combine",
    )(dest, x1, g, y)


def _pair_table():
    pa, pb = [], []
    for a in range(EXPERTS_PER_GROUP):
        for b in range(a + 1, EXPERTS_PER_GROUP):
            pa.append(a)
            pb.append(b)
    return np.asarray(pa, np.int32), np.asarray(pb, np.int32)


def _route_plan(counts, T):
    nb = T // MOE_BLK + N_CLASSES
    cnt = counts[:N_CLASSES, 0].astype(jnp.int32)
    nblk = (cnt + MOE_BLK - 1) // MOE_BLK
    cum = jnp.cumsum(nblk)
    pstart = jnp.pad((cum - nblk) * MOE_BLK, (0, LANES - N_CLASSES)).astype(jnp.int32)
    blk = jnp.minimum(jnp.arange(nb, dtype=jnp.int32), cum[-1] - 1)
    blk_cls = jnp.minimum(jnp.sum((cum[None, :] <= blk[:, None]).astype(jnp.int32), axis=1), N_CLASSES - 1)
    pa, pb = _pair_table()
    grp = blk_cls // N_PAIRS
    pr = blk_cls % N_PAIRS
    blk_ea = grp * EXPERTS_PER_GROUP + jnp.asarray(pa)[pr]
    blk_eb = grp * EXPERTS_PER_GROUP + jnp.asarray(pb)[pr]
    n_used = cum[-1:].astype(jnp.int32)
    lastblk = jnp.pad(jnp.concatenate([jnp.where(cnt > 0, cum - 1, -1), cum[-1:], jnp.sum(cnt > 0)[None]]),
                      (0, LANES - N_CLASSES - 2)).astype(jnp.int32)
    return pstart, lastblk, blk_ea.astype(jnp.int32), blk_eb.astype(jnp.int32), n_used, nb * MOE_BLK


ATTN_SCORE_ELEMS = 1 << 21


def _mixer(x, lam, wts, tabs, counts0, tm, tq):
    B, S, _ = x.shape
    tq = tq or min(S, ATTN_SCORE_ELEMS // S)
    T = B * S
    x2d = x.reshape(T, D_MODEL)
    tabs_a, tabs_r = tabs
    qa, ka, va, qr, kr, vr, gr, ga, gb, qn2, kn2 = _inproj_call(
        x2d, S, wts['attn_norm_g'], wts['w_in'], [t[:S] for t in tabs_a], [t[:S] for t in tabs_r], tm)
    oa = _attn_call(lam, qa, ka, va, qn2, kn2, wts['subln_g'], B, S, tq)
    ob = _ret_call(wts['ret_decay_f'], wts['ret_decay_b'], qr, kr, vr, gr, B, S)
    return _outproj_call(oa, ob, ga, gb, x2d, wts['w_pa'], wts['w_pb'], wts['w_out'], wts['ffn_norm_g'],
                         wts['wr'], wts['b_r'], counts0, tm)


def _layer(xs_in, lam, wts, tabs, tm=512, tq=None):
    counts = jnp.zeros((LANES, LANES), f32)
    mixed = []
    for x in xs_in:
        x1, rows, meta, counts = _mixer(x, lam, wts, tabs, counts, tm, tq)
        mixed.append((x1, rows, meta))
    total = sum(x.shape[0] * x.shape[1] for x in xs_in)
    pstart, lastblk, blk_ea, blk_eb, n_used, P = _route_plan(counts, total)
    dests = [_dest_call(pstart, meta) for _, _, meta in mixed]
    xs = _dispatch_call(lastblk, dests, [rows for _, rows, _ in mixed], P)
    y = _expert_call(blk_ea, blk_eb, n_used, xs, wts['w1'], wts['w3'], wts['w2'])
    return tuple(_combine_call(dest, x1, wts['final_norm_g'], y).reshape(x.shape)
                 for (x1, _, _), dest, x in zip(mixed, dests, xs_in))


def kernel(x_prompt, x_sample, attn_norm_g, w_in, lam_q1, lam_k1, lam_q2, lam_k2, subln_g,
           ret_decay_f, ret_decay_b, w_pa, w_pb, w_out, ffn_norm_g, w_rg, b_rg, w_re, b_re,
           w1, w3, w2, final_norm_g):
    wr = jnp.concatenate([w_rg[0], jnp.transpose(w_re[0], (1, 0, 2)).reshape(D_MODEL, N_EXPERTS)], axis=1)
    wr = jnp.pad(wr, ((0, 0), (0, LANES - wr.shape[1])))
    wr_hi = wr.astype(bf16)
    b_r = jnp.pad(jnp.concatenate([b_rg[0], b_re[0].reshape(-1)]), (0, LANES - N_GROUPS - N_EXPERTS))
    wts = dict(
        attn_norm_g=attn_norm_g[0][None, :], w_in=w_in[0].astype(bf16), subln_g=subln_g[0][None, :],
        ret_decay_f=ret_decay_f[0], ret_decay_b=ret_decay_b[0],
        w_pa=w_pa[0].astype(bf16), w_pb=w_pb[0].astype(bf16), w_out=w_out[0].astype(bf16),
        ffn_norm_g=ffn_norm_g[0][None, :],
        wr=jnp.concatenate([wr_hi, (wr - wr_hi.astype(f32)).astype(bf16)], axis=1),
        b_r=b_r[None, :], w1=w1[0].astype(bf16), w3=w3[0].astype(bf16), w2=w2[0].astype(bf16),
        final_norm_g=final_norm_g[None, :])
    lam = _lam_call(lam_q1, lam_k1, lam_q2, lam_k2)[0, :1]
    s_max = max(x_prompt.shape[1], x_sample.shape[1])
    tabs = (_rot_tables(s_max, A_ROT, A_THETA, A_HD), _rot_tables(s_max, R_DK, R_THETA, R_DK))
    return _layer((x_prompt, x_sample), lam, wts, tabs)
```

```python
import functools
import math

import numpy as np
import jax
import jax.numpy as jnp
from jax import lax
from jax.experimental import pallas as pl
from jax.experimental.pallas import tpu as pltpu

f32 = jnp.float32
bf16 = jnp.bfloat16

D_MODEL = 1024
A_HEADS = 4
A_HD = 64
A_VD = 128
A_ROT = 16
A_THETA = 500000.0
R_HEADS = 4
R_DK = 64
R_DV = 128
R_CHUNK = 128
R_THETA = 10000.0
N_GROUPS = 4
EXPERTS_PER_GROUP = 8
N_EXPERTS = 32
D_EXPERT = 512
EPS = 1e-6
LAM_INIT = 0.8 - 0.6 * math.exp(-0.3 * 0)

A_QK_W = 512
A_V_W = 512
R_QK_W = 256
R_V_W = 512
D_IN = 5120
OFF_QA, OFF_KA, OFF_VA, OFF_QR, OFF_KR, OFF_VR, OFF_GR, OFF_GA, OFF_GB = (
    0, 512, 1024, 1536, 1792, 2048, 2560, 3072, 4096)

LANES = 128
N_PAIRS = 28
N_CLASSES = N_GROUPS * N_PAIRS
HALF = D_MODEL // 2
ROW_W = HALF + LANES
u32 = jnp.uint32
MOE_BLK = 256
IDX_TILE = 256
DEST_TILES = 128
RET_UNROLL = 8
VMEM_LIMIT = 56 * 1024 * 1024


def _sigmoid(x):
    return 0.5 * jnp.tanh(0.5 * x) + 0.5


def _pack_halves(x):
    xb = x.astype(jnp.bfloat16).astype(f32)
    hi = lax.bitcast_convert_type(xb[:, :HALF], u32)
    lo = lax.bitcast_convert_type(xb[:, HALF:], u32)
    return hi | lax.shift_right_logical(lo, jnp.full_like(lo, 16))


def _unpack_halves(w):
    hi = lax.bitcast_convert_type(w & jnp.full_like(w, 0xFFFF0000), f32)
    lo = lax.bitcast_convert_type(lax.shift_left(w, jnp.full_like(w, 16)), f32)
    return jnp.concatenate([hi, lo], axis=1)


def _cparams(sem, **kw):
    return pltpu.CompilerParams(dimension_semantics=sem, vmem_limit_bytes=VMEM_LIMIT, **kw)


def _lam_kernel(q1_ref, k1_ref, q2_ref, k2_ref, o_ref):
    a = jnp.sum(q1_ref[...] * k1_ref[...], axis=-1, keepdims=True)
    b = jnp.sum(q2_ref[...] * k2_ref[...], axis=-1, keepdims=True)
    lam = jnp.exp(a) - jnp.exp(b) + LAM_INIT
    o_ref[...] = jnp.broadcast_to(lam, o_ref.shape)


def _lam_call(q1, k1, q2, k2):
    return pl.pallas_call(
        _lam_kernel,
        out_shape=jax.ShapeDtypeStruct((8, LANES), f32),
        name="lam",
    )(q1, k1, q2, k2)


def _rotary(z, c, s1, s2, shift):
    outs = []
    for j in range(z.shape[1] // LANES):
        zz = z[:, j * LANES:(j + 1) * LANES]
        outs.append(zz * c + pltpu.roll(zz, LANES - shift, 1) * s1 + pltpu.roll(zz, shift, 1) * s2)
    return jnp.concatenate(outs, axis=1)


def _inproj_kernel(x_ref, g_ref, w_ref, ca_ref, sa1_ref, sa2_ref, cr_ref, sr1_ref, sr2_ref,
                   qa_ref, ka_ref, va_ref, qr_ref, kr_ref, vr_ref, gr_ref, ga_ref, gb_ref,
                   qn_ref, kn_ref, *, nps):
    x = x_ref[...]
    xn = x * lax.rsqrt(jnp.mean(x * x, axis=-1, keepdims=True) + EPS)
    xn = (xn * g_ref[...]).astype(bf16)

    def proj(off, width):
        return jnp.dot(xn, w_ref[:, off:off + width], preferred_element_type=f32)

    ca, sa1, sa2 = ca_ref[...], sa1_ref[...], sa2_ref[...]
    cr, sr1, sr2 = cr_ref[...], sr1_ref[...], sr2_ref[...]
    qa = (_rotary(proj(OFF_QA, A_QK_W), ca, sa1, sa2, A_ROT // 2) * (A_HD ** -0.5)).astype(bf16)
    ka = _rotary(proj(OFF_KA, A_QK_W), ca, sa1, sa2, A_ROT // 2).astype(bf16)
    qa_ref[...] = qa
    ka_ref[...] = ka

    seg = (lax.broadcasted_iota(jnp.int32, (A_QK_W, LANES), 0) // A_HD
           == lax.broadcasted_iota(jnp.int32, (A_QK_W, LANES), 1))
    ind = jnp.where(seg, 1.0, 0.0).astype(bf16)

    def norm2max(z):
        n2 = jnp.dot(z * z, ind, preferred_element_type=f32)
        return jnp.max(n2, axis=0, keepdims=True)

    qmax, kmax = norm2max(qa), norm2max(ka)
    first = pl.program_id(0) % nps == 0

    @pl.when(first)
    def _():
        qn_ref[0] = qmax
        kn_ref[0] = kmax

    @pl.when(jnp.logical_not(first))
    def _():
        qn_ref[0] = jnp.maximum(qn_ref[0], qmax)
        kn_ref[0] = jnp.maximum(kn_ref[0], kmax)

    va_ref[...] = proj(OFF_VA, A_V_W).astype(bf16)
    qr_ref[...] = _rotary(proj(OFF_QR, R_QK_W), cr, sr1, sr2, R_DK // 2)
    kr_ref[...] = _rotary(proj(OFF_KR, R_QK_W), cr, sr1, sr2, R_DK // 2) * (R_DK ** -0.5)
    vr_ref[...] = proj(OFF_VR, R_V_W).astype(bf16)
    gr = proj(OFF_GR, R_V_W)
    gr_ref[...] = (gr * _sigmoid(gr)).astype(bf16)
    ga_ref[...] = _sigmoid(proj(OFF_GA, D_MODEL)).astype(bf16)
    gb_ref[...] = _sigmoid(proj(OFF_GB, D_MODEL)).astype(bf16)


def _rot_tables(S, rot_dim, theta, head_dim):
    inv = theta ** (-jnp.arange(0, rot_dim, 2, dtype=f32) / rot_dim)
    pos = jnp.arange(S, dtype=f32)
    ang = pos[:, None] * inv[None, :]
    cos, sin = jnp.cos(ang), jnp.sin(ang)
    half = rot_dim // 2
    pad = head_dim - rot_dim
    c = jnp.concatenate([cos, cos, jnp.ones((S, pad), f32)], axis=1)
    s1 = jnp.concatenate([-sin, jnp.zeros((S, half + pad), f32)], axis=1)
    s2 = jnp.concatenate([jnp.zeros((S, half), f32), sin, jnp.zeros((S, pad), f32)], axis=1)
    rep = LANES // head_dim
    return tuple(jnp.tile(t, (1, rep)) for t in (c, s1, s2))


def _inproj_call(x2d, S, g, w_in_bf, tabs_a, tabs_r, tm):
    T = x2d.shape[0]
    nps = S // tm
    row = lambda i: (i, 0)
    posmap = lambda i: (i % nps, 0)
    const = lambda i: (0, 0)
    tab_spec = pl.BlockSpec((tm, LANES), posmap)
    outs = [
        (A_QK_W, bf16), (A_QK_W, bf16), (A_V_W, bf16),
        (R_QK_W, f32), (R_QK_W, f32), (R_V_W, bf16), (R_V_W, bf16),
        (D_MODEL, bf16), (D_MODEL, bf16),
    ]
    nrm_spec = pl.BlockSpec((1, 1, LANES), lambda i: (i // nps, 0, 0))
    nrm_shape = jax.ShapeDtypeStruct((T // S, 1, LANES), f32)
    return pl.pallas_call(
        functools.partial(_inproj_kernel, nps=nps),
        grid=(T // tm,),
        in_specs=[pl.BlockSpec((tm, D_MODEL), row),
                  pl.BlockSpec((1, D_MODEL), const),
                  pl.BlockSpec((D_MODEL, D_IN), const)] + [tab_spec] * 6,
        out_specs=[pl.BlockSpec((tm, w), row) for w, _ in outs] + [nrm_spec, nrm_spec],
        out_shape=[jax.ShapeDtypeStruct((T, w), dt) for w, dt in outs] + [nrm_shape, nrm_shape],
        compiler_params=_cparams(("arbitrary",)),
        name="inproj",
    )(x2d, g, w_in_bf, *tabs_a, *tabs_r)


def _attn_kernel(kmax_ref, safe_ref, lam_ref, q_ref, k_ref, v_ref, g_ref, o_ref, vaug_ref, *, tq):
    b, h, qi = pl.program_id(0), pl.program_id(1), pl.program_id(2)
    S = v_ref.shape[0]

    @pl.when(qi == 0)
    def _():
        vaug_ref[:, :LANES] = v_ref[...]
        vaug_ref[:, LANES:] = jnp.ones((S, LANES), bf16)

    def stacked_q():
        q = q_ref[...]
        lane = lax.broadcasted_iota(jnp.int32, q.shape, 1)
        zero = jnp.zeros_like(q)
        return jnp.concatenate([jnp.where(lane < A_HD, q, zero), jnp.where(lane >= A_HD, q, zero)], axis=0)

    def scores(qs):
        return lax.dot_general(qs, k_ref[...], (((1,), (1,)), ((), ())), preferred_element_type=f32)

    def finish(e):
        oa = jnp.dot(e.astype(bf16), vaug_ref[...], preferred_element_type=f32)
        o = oa[:, :LANES] / oa[:, LANES:]
        a = o[:tq] - lam_ref[0] * o[tq:]
        y = a * lax.rsqrt(jnp.mean(a * a, axis=-1, keepdims=True) + EPS)
        o_ref[...] = ((y * g_ref[...]) * (1.0 - LAM_INIT)).astype(o_ref.dtype)

    safe = safe_ref[b * A_HEADS + h] == 1

    @pl.when(safe)
    def _():
        qs = stacked_q()
        qf = qs.astype(f32)
        qn = jnp.sqrt(jnp.sum(qf * qf, axis=-1, keepdims=True))
        row = lax.broadcasted_iota(jnp.int32, qn.shape, 0)
        kbase = (b * A_HEADS + h) * 2
        shift = qn * jnp.where(row < tq, kmax_ref[kbase], kmax_ref[kbase + 1])
        finish(jnp.exp(scores(qs) - shift))

    @pl.when(jnp.logical_not(safe))
    def _():
        s = scores(stacked_q())
        finish(jnp.exp(s - jnp.max(s, axis=-1, keepdims=True)))


ATTN_SAFE_BOUND = 40.0


def _attn_call(lam, qa, ka, va, qn2, kn2, subln_g, B, S, tq):
    T = qa.shape[0]
    nq = S // tq
    nsub = 2 * A_HEADS
    qmax = jnp.sqrt(qn2[:, 0, :nsub])
    kmax = jnp.sqrt(kn2[:, 0, :nsub])
    safe = jnp.all((qmax * kmax).reshape(B, A_HEADS, 2) < ATTN_SAFE_BOUND, axis=-1)
    return pl.pallas_call(
        functools.partial(_attn_kernel, tq=tq),
        grid_spec=pltpu.PrefetchScalarGridSpec(
            num_scalar_prefetch=2,
            grid=(B, A_HEADS, nq),
            in_specs=[pl.BlockSpec(memory_space=pltpu.SMEM),
                      pl.BlockSpec((tq, LANES), lambda b, h, i, km, sf: (b * nq + i, h)),
                      pl.BlockSpec((S, LANES), lambda b, h, i, km, sf: (b, h)),
                      pl.BlockSpec((S, LANES), lambda b, h, i, km, sf: (b, h)),
                      pl.BlockSpec((1, LANES), lambda b, h, i, km, sf: (0, 0))],
            out_specs=pl.BlockSpec((tq, LANES), lambda b, h, i, km, sf: (b * nq + i, h)),
            scratch_shapes=[pltpu.VMEM((S, 2 * LANES), bf16)],
        ),
        out_shape=jax.ShapeDtypeStruct((T, A_V_W), bf16),
        compiler_params=_cparams(("arbitrary", "arbitrary", "arbitrary")),
        name="attn",
    )(kmax.reshape(-1), safe.reshape(-1).astype(jnp.int32), lam, qa, ka, va, subln_g)


def _ret_kernel(df_ref, db_ref, q_ref, k_ref, v_ref, gr_ref, o_ref, uf_ref, ub_ref):
    p = pl.program_id(1)
    C = R_CHUNK
    S = q_ref.shape[0]
    n = S // C
    lane = lax.broadcasted_iota(jnp.int32, (C, LANES), 1)
    rowi = lax.broadcasted_iota(jnp.int32, (C, LANES), 0)
    first = lane < R_DK
    ones = jnp.ones((C, LANES), f32)
    lgf_a, lgf_b = -jnp.exp(ones * df_ref[2 * p]), -jnp.exp(ones * df_ref[2 * p + 1])
    lgb_a, lgb_b = -jnp.exp(ones * db_ref[2 * p]), -jnp.exp(ones * db_ref[2 * p + 1])
    lgf = jnp.where(first, lgf_a, lgf_b)
    lgb = jnp.where(first, lgb_a, lgb_b)
    idx = rowi.astype(f32)
    k_f = jnp.exp(lgf * (C - 1 - idx))
    k_b = jnp.exp(lgb * idx)
    q_f = jnp.exp(lgf * (idx + 1))
    q_b = jnp.exp(lgb * (C - idx))
    top = rowi < R_DK
    g_f = jnp.where(top, jnp.exp(lgf_a * C), jnp.exp(lgf_b * C))
    g_b = jnp.where(top, jnp.exp(lgb_a * C), jnp.exp(lgb_b * C))
    g_f = jnp.concatenate([g_f, g_f], axis=1)
    g_b = jnp.concatenate([g_b, g_b], axis=1)
    rel = (rowi - lane).astype(f32)

    def decay_mask(lf, lb):
        return jnp.where(rel >= 0, jnp.exp(lf * jnp.maximum(rel, 0.0)), jnp.exp(lb * jnp.maximum(-rel, 0.0)))

    mask = jnp.concatenate([decay_mask(lgf_a, lgb_a), decay_mask(lgf_b, lgb_b)], axis=0)

    def summaries(c, carry):
        r0 = pl.multiple_of(c * C, C)
        kc = k_ref[pl.ds(r0, C), :]
        vc = v_ref[pl.ds(r0, C), :]
        kt = jnp.concatenate([(kc * k_f).T, (kc * k_b).T], axis=0).astype(bf16)
        uu = jnp.dot(kt, vc, preferred_element_type=f32)
        uf_ref[c] = uu[:LANES]
        ub_ref[c] = uu[LANES:]
        return carry

    lax.fori_loop(0, n, summaries, 0, unroll=RET_UNROLL)

    def scan_f(c, st):
        u = uf_ref[c]
        uf_ref[c] = st
        return g_f * st + u

    lax.fori_loop(0, n, scan_f, jnp.zeros((LANES, 2 * LANES), f32))

    def scan_b(j, st):
        c = n - 1 - j
        u = ub_ref[c]
        ub_ref[c] = st
        return g_b * st + u

    lax.fori_loop(0, n, scan_b, jnp.zeros((LANES, 2 * LANES), f32))

    def outputs(c, carry):
        r0 = pl.multiple_of(c * C, C)
        qc = q_ref[pl.ds(r0, C), :]
        kc = k_ref[pl.ds(r0, C), :]
        vc = v_ref[pl.ds(r0, C), :]
        zero = jnp.zeros_like(qc)
        q2 = jnp.concatenate([jnp.where(first, qc, zero), jnp.where(first, zero, qc)], axis=0)
        s = lax.dot_general(q2.astype(bf16), kc.astype(bf16), (((1,), (1,)), ((), ())),
                            preferred_element_type=f32)
        qf2 = jnp.concatenate([q_f, q_f], axis=0)
        qb2 = jnp.concatenate([q_b, q_b], axis=0)
        lhs = jnp.concatenate([s * mask, q2 * qf2, q2 * qb2], axis=1).astype(bf16)
        rhs = jnp.concatenate([vc, uf_ref[c].astype(bf16), ub_ref[c].astype(bf16)], axis=0)
        yy = jnp.dot(lhs, rhs, preferred_element_type=f32)
        ya = yy[:C, :LANES]
        yb = yy[C:, LANES:]
        ya = ya * lax.rsqrt(jnp.mean(ya * ya, axis=-1, keepdims=True) + EPS)
        yb = yb * lax.rsqrt(jnp.mean(yb * yb, axis=-1, keepdims=True) + EPS)
        y = jnp.concatenate([ya, yb], axis=1)
        o_ref[pl.ds(r0, C), :] = (gr_ref[pl.ds(r0, C), :].astype(f32) * y).astype(o_ref.dtype)
        return carry

    lax.fori_loop(0, n, outputs, 0, unroll=RET_UNROLL)


def _ret_call(decay_f, decay_b, qr, kr, vr, gr, B, S):
    T = qr.shape[0]
    n = S // R_CHUNK
    blk = lambda w: pl.BlockSpec((S, w), lambda b, p: (b, p))
    return pl.pallas_call(
        _ret_kernel,
        grid=(B, R_HEADS // 2),
        in_specs=[pl.BlockSpec(memory_space=pltpu.SMEM), pl.BlockSpec(memory_space=pltpu.SMEM),
                  blk(LANES), blk(LANES), blk(2 * LANES), blk(2 * LANES)],
        out_specs=blk(2 * LANES),
        out_shape=jax.ShapeDtypeStruct((T, R_V_W), bf16),
        scratch_shapes=[pltpu.VMEM((n, LANES, 2 * LANES), f32), pltpu.VMEM((n, LANES, 2 * LANES), f32)],
        compiler_params=_cparams(("arbitrary", "arbitrary")),
        name="ret",
    )(decay_f, decay_b, qr, kr, vr, gr)


def _outproj_kernel(oa_ref, ob_ref, ga_ref, gb_ref, x_ref, wpa_ref, wpb_ref, wo_ref, g_ref,
                    wr_ref, br_ref, cnt0_ref,
                    x1_ref, row_ref, meta_ref, cnt_ref, tri_ref, carry_ref, *, tm):
    i = pl.program_id(0)

    @pl.when(i == 0)
    def _():
        r = lax.broadcasted_iota(jnp.int32, (tm, tm), 0)
        c = lax.broadcasted_iota(jnp.int32, (tm, tm), 1)
        tri_ref[...] = jnp.where(r < c, 1.0, 0.0).astype(bf16)
        carry_ref[...] = cnt0_ref[...]

    pa = jnp.dot(oa_ref[...], wpa_ref[...], preferred_element_type=f32)
    pb = jnp.dot(ob_ref[...], wpb_ref[...], preferred_element_type=f32)
    merged = ga_ref[...].astype(f32) * pa + gb_ref[...].astype(f32) * pb
    x1 = x_ref[...] + jnp.dot(merged.astype(bf16), wo_ref[...], preferred_element_type=f32)
    x1_ref[...] = x1
    xn = x1 * lax.rsqrt(jnp.mean(x1 * x1, axis=-1, keepdims=True) + EPS)
    xn = xn * g_ref[...]
    row_ref[:, :HALF] = _pack_halves(xn)

    xh = xn.astype(bf16)
    xl = (xn - xh.astype(f32)).astype(bf16)
    parts = jnp.dot(jnp.concatenate([xh, xl], axis=0), wr_ref[...], preferred_element_type=f32)
    logits = parts[:tm, :LANES] + parts[:tm, LANES:] + parts[tm:, :LANES] + br_ref[...]

    lt = logits.T
    row = lax.broadcasted_iota(jnp.int32, (LANES, tm), 0)
    row_f = row.astype(f32)
    ninf = jnp.full((LANES, tm), -jnp.inf, f32)
    big = jnp.full((LANES, tm), float(LANES - 1), f32)

    def first_row(hit):
        return jnp.min(jnp.where(hit, row_f, big), axis=0, keepdims=True).astype(jnp.int32)

    isg = row < N_GROUPS
    lg = jnp.where(isg, lt, ninf)
    mg = jnp.max(lg, axis=0, keepdims=True)
    gsel = first_row(lg == mg)
    pg_top = 1.0 / jnp.sum(jnp.where(isg, jnp.exp(lt - mg), 0.0), axis=0, keepdims=True)
    lo = N_GROUPS + EXPERTS_PER_GROUP * gsel
    le = jnp.where((row >= lo) & (row < lo + EXPERTS_PER_GROUP), lt, ninf)
    m1 = jnp.max(le, axis=0, keepdims=True)
    i1 = first_row(le == m1)
    le2 = jnp.where(row == i1, ninf, le)
    m2 = jnp.max(le2, axis=0, keepdims=True)
    i2 = first_row(le2 == m2)
    t2 = jnp.exp(m2 - m1)
    gate1 = pg_top / (1.0 + t2)
    gate2 = pg_top * t2 / (1.0 + t2)
    e1 = i1 - lo
    e2 = i2 - lo
    ea = jnp.minimum(e1, e2)
    eb = jnp.maximum(e1, e2)
    gate_a = jnp.where(e1 < e2, gate1, gate2)
    gate_b = jnp.where(e1 < e2, gate2, gate1)
    pair = lax.shift_right_logical(ea * (2 * EXPERTS_PER_GROUP - 1 - ea), 1) + (eb - ea - 1)
    cls = gsel * N_PAIRS + pair
    gates_t = jnp.where(row == 0, gate_a, jnp.where(row == 1, gate_b, 0.0))
    row_ref[:, HALF:] = lax.bitcast_convert_type(gates_t.T, u32)

    hit = row == cls
    onehot = jnp.where(hit, 1.0, 0.0)
    carry = jnp.concatenate([carry_ref[...]] * (tm // LANES), axis=1)
    rank = jnp.dot(onehot.astype(bf16), tri_ref[...], preferred_element_type=f32) + carry
    pos = jnp.sum(jnp.where(hit, rank, 0.0), axis=0, keepdims=True).astype(jnp.int32)
    carry_ref[...] = carry_ref[...] + jnp.sum(onehot, axis=1, keepdims=True)
    cnt_ref[...] = carry_ref[...]
    row8 = lax.broadcasted_iota(jnp.int32, (8, tm), 0)
    meta = jnp.where(row8 == 0, cls, jnp.where(row8 == 1, pos, 0))
    for c in range(tm // IDX_TILE):
        meta_ref[c] = meta[:, c * IDX_TILE:(c + 1) * IDX_TILE]


def _outproj_call(oa, ob, ga, gb, x2d, wpa, wpb, wo, g, wr, br, cnt0, tm):
    T = x2d.shape[0]
    row = lambda i: (i, 0)
    const = lambda i: (0, 0)
    return pl.pallas_call(
        functools.partial(_outproj_kernel, tm=tm),
        grid=(T // tm,),
        in_specs=[pl.BlockSpec((tm, A_V_W), row), pl.BlockSpec((tm, R_V_W), row),
                  pl.BlockSpec((tm, D_MODEL), row), pl.BlockSpec((tm, D_MODEL), row),
                  pl.BlockSpec((tm, D_MODEL), row),
                  pl.BlockSpec((A_V_W, D_MODEL), const), pl.BlockSpec((R_V_W, D_MODEL), const),
                  pl.BlockSpec((D_MODEL, D_MODEL), const), pl.BlockSpec((1, D_MODEL), const),
                  pl.BlockSpec((D_MODEL, 2 * LANES), const), pl.BlockSpec((1, LANES), const),
                  pl.BlockSpec((LANES, LANES), const)],
        out_specs=[pl.BlockSpec((tm, D_MODEL), row), pl.BlockSpec((tm, ROW_W), row),
                   pl.BlockSpec((tm // IDX_TILE, 8, IDX_TILE), lambda i: (i, 0, 0)),
                   pl.BlockSpec((LANES, LANES), const)],
        out_shape=[jax.ShapeDtypeStruct((T, D_MODEL), f32), jax.ShapeDtypeStruct((T, ROW_W), u32),
                   jax.ShapeDtypeStruct((T // IDX_TILE, 8, IDX_TILE), jnp.int32),
                   jax.ShapeDtypeStruct((LANES, LANES), f32)],
        scratch_shapes=[pltpu.VMEM((tm, tm), bf16), pltpu.VMEM((LANES, LANES), f32)],
        compiler_params=_cparams(("arbitrary",)),
        name="outproj",
    )(oa, ob, ga, gb, x2d, wpa, wpb, wo, g, wr, br, cnt0)


def _dest_kernel(pstart_ref, meta_ref, o_ref):
    cls = meta_ref[:, 0, :]
    rank = meta_ref[:, 1, :]

    def body(c, acc):
        return acc + jnp.where(cls == c, pstart_ref[c], 0)

    o_ref[:, 0, :] = lax.fori_loop(0, N_CLASSES, body, rank)


def _dest_call(pstart, meta):
    nt = meta.shape[0]
    blk = min(nt, DEST_TILES)
    return pl.pallas_call(
        _dest_kernel,
        grid=(nt // blk,),
        in_specs=[pl.BlockSpec(memory_space=pltpu.SMEM),
                  pl.BlockSpec((blk, 8, IDX_TILE), lambda i: (i, 0, 0))],
        out_specs=pl.BlockSpec((blk, 1, IDX_TILE), lambda i: (i, 0, 0)),
        out_shape=jax.ShapeDtypeStruct((nt, 1, IDX_TILE), jnp.int32),
        compiler_params=_cparams(("arbitrary",)),
        name="dest",
    )(pstart, meta)


def _dispatch_kernel(lastblk_ref, dest_ref, *refs, tiles):
    rows_refs = refs[:len(tiles)]
    xs_ref, buf_ref, zero_ref, sem_in, sem_out, zsem = refs[len(tiles):]
    i = pl.program_id(0)
    n = sum(tiles)
    slot = i % 2

    def tile_copy(rows_ref, t, s):
        r0 = pl.multiple_of(t * IDX_TILE, IDX_TILE)
        return pltpu.make_async_copy(rows_ref.at[pl.ds(r0, IDX_TILE), :], buf_ref.at[s], sem_in.at[s])

    def load_start(t, s):
        t = jnp.asarray(t, jnp.int32)
        first = 0
        for rows_ref, cnt in zip(rows_refs, tiles):
            @pl.when((t >= first) & (t < first + cnt))
            def _(rows_ref=rows_ref, first=first):
                tile_copy(rows_ref, t - first, s).start()
            first += cnt

    def load_wait(s):
        tile_copy(rows_refs[0], 0, s).wait()

    def scatter_wait(s):
        pltpu.make_async_copy(buf_ref.at[s], xs_ref.at[pl.ds(0, IDX_TILE), :], sem_out.at[s]).wait()

    @pl.when(i == 0)
    def _():
        load_start(0, 0)
        zero_ref[...] = jnp.zeros_like(zero_ref)

        def zero_block(j):
            off = pl.multiple_of(j * MOE_BLK, MOE_BLK)
            pltpu.make_async_copy(zero_ref, xs_ref.at[pl.ds(off, MOE_BLK), :], zsem).start()

        def zfill_last(c, carry):
            j = lastblk_ref[c]

            @pl.when(j >= 0)
            def _():
                zero_block(j)

            return carry

        def zfill_unused(j, carry):
            zero_block(j)
            return carry

        def zwait(c, carry):
            pltpu.make_async_copy(zero_ref, xs_ref.at[pl.ds(0, MOE_BLK), :], zsem).wait()
            return carry

        n_used = lastblk_ref[N_CLASSES]
        n_blocks = xs_ref.shape[0] // MOE_BLK
        lax.fori_loop(0, N_CLASSES, zfill_last, 0)
        lax.fori_loop(n_used, n_blocks, zfill_unused, 0)
        lax.fori_loop(0, lastblk_ref[N_CLASSES + 1] + n_blocks - n_used, zwait, 0)

    @pl.when(i >= 1)
    def _():
        scatter_wait(1 - slot)

    @pl.when(i + 1 < n)
    def _():
        load_start(i + 1, 1 - slot)

    load_wait(slot)
    for r in range(IDX_TILE):
        d = dest_ref[0, 0, r]
        pltpu.make_async_copy(buf_ref.at[slot, pl.ds(r, 1), :], xs_ref.at[pl.ds(d, 1), :],
                              sem_out.at[slot]).start(priority=r % 2)

    @pl.when(i == n - 1)
    def _():
        scatter_wait(slot)


def _dispatch_call(lastblk, dests, rows, P):
    tiles = tuple(r.shape[0] // IDX_TILE for r in rows)
    any_spec = pl.BlockSpec(memory_space=pl.ANY)
    return pl.pallas_call(
        functools.partial(_dispatch_kernel, tiles=tiles),
        grid=(sum(tiles),),
        in_specs=[pl.BlockSpec(memory_space=pltpu.SMEM),
                  pl.BlockSpec((1, 1, IDX_TILE), lambda i: (i, 0, 0), memory_space=pltpu.SMEM)]
                 + [any_spec] * len(rows),
        out_specs=any_spec,
        scratch_shapes=[pltpu.VMEM((2, IDX_TILE, ROW_W), u32), pltpu.VMEM((MOE_BLK, ROW_W), u32),
                        pltpu.SemaphoreType.DMA((2,)), pltpu.SemaphoreType.DMA((2,)),
                        pltpu.SemaphoreType.DMA(())],
        out_shape=jax.ShapeDtypeStruct((P, ROW_W), u32),
        compiler_params=_cparams(("arbitrary",), has_side_effects=True),
        name="dispatch",
    )(lastblk, jnp.concatenate(dests, axis=0), *rows)


def _expert_kernel(ea_ref, eb_ref, nb_ref, xs_ref, w1a_ref, w3a_ref, w2a_ref, w1b_ref, w3b_ref, w2b_ref,
                   y_ref):
    j = pl.program_id(0)

    @pl.when(j < nb_ref[0])
    def _():
        x = _unpack_halves(xs_ref[:, :HALF]).astype(bf16)
        gates = lax.bitcast_convert_type(xs_ref[:, HALF:], f32)
        ga = gates[:, 0:1]
        gb = gates[:, 1:2]

        def ffn(w1_ref, w3_ref, w2_ref, gate):
            a = jnp.dot(x, w1_ref[0], preferred_element_type=f32)
            b = jnp.dot(x, w3_ref[0], preferred_element_type=f32)
            h = ((a * _sigmoid(a)) * b) * gate
            return jnp.dot(h.astype(bf16), w2_ref[0], preferred_element_type=f32)

        y_ref[...] = _pack_halves(ffn(w1a_ref, w3a_ref, w2a_ref, ga) + ffn(w1b_ref, w3b_ref, w2b_ref, gb))

    @pl.when(j >= nb_ref[0])
    def _():
        y_ref[...] = jnp.zeros_like(y_ref)


def _expert_call(blk_ea, blk_eb, n_used, xs, w1, w3, w2):
    P = xs.shape[0]
    nb = P // MOE_BLK
    wa = lambda j, ea, eb, n: (ea[j], 0, 0)
    wb = lambda j, ea, eb, n: (eb[j], 0, 0)
    up = (1, D_MODEL, D_EXPERT)
    down = (1, D_EXPERT, D_MODEL)
    return pl.pallas_call(
        _expert_kernel,
        grid_spec=pltpu.PrefetchScalarGridSpec(
            num_scalar_prefetch=3,
            grid=(nb,),
            in_specs=[pl.BlockSpec((MOE_BLK, ROW_W), lambda j, ea, eb, n: (jnp.minimum(j, n[0] - 1), 0)),
                      pl.BlockSpec(up, wa), pl.BlockSpec(up, wa), pl.BlockSpec(down, wa),
                      pl.BlockSpec(up, wb), pl.BlockSpec(up, wb), pl.BlockSpec(down, wb)],
            out_specs=pl.BlockSpec((MOE_BLK, HALF), lambda j, ea, eb, n: (j, 0)),
        ),
        out_shape=jax.ShapeDtypeStruct((P, HALF), u32),
        compiler_params=_cparams(("arbitrary",)),
        name="experts",
    )(blk_ea, blk_eb, n_used, xs, w1, w3, w2, w1, w3, w2)


def _combine_kernel(dest_ref, x1_ref, g_ref, y_ref, o_ref, buf_ref, sems, *, n):
    i = pl.program_id(0)
    slot = i % 2

    @pl.when(i < n)
    def _():
        for r in range(IDX_TILE):
            d = dest_ref[0, 0, r]
            pltpu.make_async_copy(y_ref.at[pl.ds(d, 1), :], buf_ref.at[slot, pl.ds(r, 1), :],
                                  sems.at[slot]).start(priority=r % 2)

    @pl.when(i >= 1)
    def _():
        prev = 1 - slot
        pltpu.make_async_copy(y_ref.at[pl.ds(0, IDX_TILE), :], buf_ref.at[prev], sems.at[prev]).wait()
        x = x1_ref[...] + _unpack_halves(buf_ref[prev])
        y = x * lax.rsqrt(jnp.mean(x * x, axis=-1, keepdims=True) + EPS)
        o_ref[...] = y * g_ref[...]


def _combine_call(dest, x1, g, y):
    T = x1.shape[0]
    n = T // IDX_TILE
    done = lambda i: (jnp.maximum(i - 1, 0), 0)
    return pl.pallas_call(
        functools.partial(_combine_kernel, n=n),
        grid=(n + 1,),
        in_specs=[pl.BlockSpec((1, 1, IDX_TILE), lambda i: (jnp.minimum(i, n - 1), 0, 0), memory_space=pltpu.SMEM),
                  pl.BlockSpec((IDX_TILE, D_MODEL), done),
                  pl.BlockSpec((1, D_MODEL), lambda i: (0, 0)),
                  pl.BlockSpec(memory_space=pl.ANY)],
        out_specs=pl.BlockSpec((IDX_TILE, D_MODEL), done),
        scratch_shapes=[pltpu.VMEM((2, IDX_TILE, HALF), u32), pltpu.SemaphoreType.DMA((2,))],
        out_shape=jax.ShapeDtypeStruct((T, D_MODEL), f32),
        compiler_params=_cparams(("arbitrary",)),
        name="combine",
    )(dest, x1, g, y)


def _pair_table():
    pa, pb = [], []
    for a in range(EXPERTS_PER_GROUP):
        for b in range(a + 1, EXPERTS_PER_GROUP):
            pa.append(a)
            pb.append(b)
    return np.asarray(pa, np.int32), np.asarray(pb, np.int32)


def _route_plan(counts, T):
    nb = T // MOE_BLK + N_CLASSES
    cnt = counts[:N_CLASSES, 0].astype(jnp.int32)
    nblk = (cnt + MOE_BLK - 1) // MOE_BLK
    cum = jnp.cumsum(nblk)
    pstart = jnp.pad((cum - nblk) * MOE_BLK, (0, LANES - N_CLASSES)).astype(jnp.int32)
    blk = jnp.minimum(jnp.arange(nb, dtype=jnp.int32), cum[-1] - 1)
    blk_cls = jnp.minimum(jnp.sum((cum[None, :] <= blk[:, None]).astype(jnp.int32), axis=1), N_CLASSES - 1)
    pa, pb = _pair_table()
    grp = blk_cls // N_PAIRS
    pr = blk_cls % N_PAIRS
    blk_ea = grp * EXPERTS_PER_GROUP + jnp.asarray(pa)[pr]
    blk_eb = grp * EXPERTS_PER_GROUP + jnp.asarray(pb)[pr]
    n_used = cum[-1:].astype(jnp.int32)
    lastblk = jnp.pad(jnp.concatenate([jnp.where(cnt > 0, cum - 1, -1), cum[-1:], jnp.sum(cnt > 0)[None]]),
                      (0, LANES - N_CLASSES - 2)).astype(jnp.int32)
    return pstart, lastblk, blk_ea.astype(jnp.int32), blk_eb.astype(jnp.int32), n_used, nb * MOE_BLK


ATTN_SCORE_ELEMS = 1 << 21


def _mixer(x, lam, wts, tabs, counts0, tm, tq):
    B, S, _ = x.shape
    tq = tq or min(S, ATTN_SCORE_ELEMS // S)
    T = B * S
    x2d = x.reshape(T, D_MODEL)
    tabs_a, tabs_r = tabs
    qa, ka, va, qr, kr, vr, gr, ga, gb, qn2, kn2 = _inproj_call(
        x2d, S, wts['attn_norm_g'], wts['w_in'], [t[:S] for t in tabs_a], [t[:S] for t in tabs_r], tm)
    oa = _attn_call(lam, qa, ka, va, qn2, kn2, wts['subln_g'], B, S, tq)
    ob = _ret_call(wts['ret_decay_f'], wts['ret_decay_b'], qr, kr, vr, gr, B, S)
    return _outproj_call(oa, ob, ga, gb, x2d, wts['w_pa'], wts['w_pb'], wts['w_out'], wts['ffn_norm_g'],
                         wts['wr'], wts['b_r'], counts0, tm)


def _layer(xs_in, lam, wts, tabs, tm=512, tq=None):
    counts = jnp.zeros((LANES, LANES), f32)
    mixed = []
    for x in xs_in:
        x1, rows, meta, counts = _mixer(x, lam, wts, tabs, counts, tm, tq)
        mixed.append((x1, rows, meta))
    total = sum(x.shape[0] * x.shape[1] for x in xs_in)
    pstart, lastblk, blk_ea, blk_eb, n_used, P = _route_plan(counts, total)
    dests = [_dest_call(pstart, meta) for _, _, meta in mixed]
    xs = _dispatch_call(lastblk, dests, [rows for _, rows, _ in mixed], P)
    y = _expert_call(blk_ea, blk_eb, n_used, xs, wts['w1'], wts['w3'], wts['w2'])
    return tuple(_combine_call(dest, x1, wts['final_norm_g'], y).reshape(x.shape)
                 for (x1, _, _), dest, x in zip(mixed, dests, xs_in))


def kernel(x_prompt, x_sample, attn_norm_g, w_in, lam_q1, lam_k1, lam_q2, lam_k2, subln_g,
           ret_decay_f, ret_decay_b, w_pa, w_pb, w_out, ffn_norm_g, w_rg, b_rg, w_re, b_re,
           w1, w3, w2, final_norm_g):
    wr = jnp.concatenate([w_rg[0], jnp.transpose(w_re[0], (1, 0, 2)).reshape(D_MODEL, N_EXPERTS)], axis=1)
    wr = jnp.pad(wr, ((0, 0), (0, LANES - wr.shape[1])))
    wr_hi = wr.astype(bf16)
    b_r = jnp.pad(jnp.concatenate([b_rg[0], b_re[0].reshape(-1)]), (0, LANES - N_GROUPS - N_EXPERTS))
    wts = dict(
        attn_norm_g=attn_norm_g[0][None, :], w_in=w_in[0].astype(bf16), subln_g=subln_g[0][None, :],
        ret_decay_f=ret_decay_f[0], ret_decay_b=ret_decay_b[0],
        w_pa=w_pa[0].astype(bf16), w_pb=w_pb[0].astype(bf16), w_out=w_out[0].astype(bf16),
        ffn_norm_g=ffn_norm_g[0][None, :],
        wr=jnp.concatenate([wr_hi, (wr - wr_hi.astype(f32)).astype(bf16)], axis=1),
        b_r=b_r[None, :], w1=w1[0].astype(bf16), w3=w3[0].astype(bf16), w2=w2[0].astype(bf16),
        final_norm_g=final_norm_g[None, :])
    lam = _lam_call(lam_q1, lam_k1, lam_q2, lam_k2)[0, :1]
    s_max = max(x_prompt.shape[1], x_sample.shape[1])
    tabs = (_rot_tables(s_max, A_ROT, A_THETA, A_HD), _rot_tables(s_max, R_DK, R_THETA, R_DK))
    return _layer((x_prompt, x_sample), lam, wts, tabs)
```

```python
import functools
import math

import numpy as np
import jax
import jax.numpy as jnp
from jax import lax
from jax.experimental import pallas as pl
from jax.experimental.pallas import tpu as pltpu

f32 = jnp.float32
bf16 = jnp.bfloat16

D_MODEL = 1024
A_HEADS = 4
A_HD = 64
A_VD = 128
A_ROT = 16
A_THETA = 500000.0
R_HEADS = 4
R_DK = 64
R_DV = 128
R_CHUNK = 128
R_THETA = 10000.0
N_GROUPS = 4
EXPERTS_PER_GROUP = 8
N_EXPERTS = 32
D_EXPERT = 512
EPS = 1e-6
LAM_INIT = 0.8 - 0.6 * math.exp(-0.3 * 0)

A_QK_W = 512
A_V_W = 512
R_QK_W = 256
R_V_W = 512
D_IN = 5120
OFF_QA, OFF_KA, OFF_VA, OFF_QR, OFF_KR, OFF_VR, OFF_GR, OFF_GA, OFF_GB = (
    0, 512, 1024, 1536, 1792, 2048, 2560, 3072, 4096)

LANES = 128
N_PAIRS = 28
N_CLASSES = N_GROUPS * N_PAIRS
HALF = D_MODEL // 2
ROW_W = HALF + LANES
u32 = jnp.uint32
MOE_BLK = 256
IDX_TILE = 512
DEST_TILES = 128
RET_UNROLL = 8
VMEM_LIMIT = 56 * 1024 * 1024


def _sigmoid(x):
    return 0.5 * jnp.tanh(0.5 * x) + 0.5


def _pack_halves(x):
    xb = x.astype(jnp.bfloat16).astype(f32)
    hi = lax.bitcast_convert_type(xb[:, :HALF], u32)
    lo = lax.bitcast_convert_type(xb[:, HALF:], u32)
    return hi | lax.shift_right_logical(lo, jnp.full_like(lo, 16))


def _unpack_halves(w):
    hi = lax.bitcast_convert_type(w & jnp.full_like(w, 0xFFFF0000), f32)
    lo = lax.bitcast_convert_type(lax.shift_left(w, jnp.full_like(w, 16)), f32)
    return jnp.concatenate([hi, lo], axis=1)


def _cparams(sem, **kw):
    return pltpu.CompilerParams(dimension_semantics=sem, vmem_limit_bytes=VMEM_LIMIT, **kw)


def _lam_kernel(q1_ref, k1_ref, q2_ref, k2_ref, o_ref):
    a = jnp.sum(q1_ref[...] * k1_ref[...], axis=-1, keepdims=True)
    b = jnp.sum(q2_ref[...] * k2_ref[...], axis=-1, keepdims=True)
    lam = jnp.exp(a) - jnp.exp(b) + LAM_INIT
    o_ref[...] = jnp.broadcast_to(lam, o_ref.shape)


def _lam_call(q1, k1, q2, k2):
    return pl.pallas_call(
        _lam_kernel,
        out_shape=jax.ShapeDtypeStruct((8, LANES), f32),
        name="lam",
    )(q1, k1, q2, k2)


def _rotary(z, c, s1, s2, shift):
    outs = []
    for j in range(z.shape[1] // LANES):
        zz = z[:, j * LANES:(j + 1) * LANES]
        outs.append(zz * c + pltpu.roll(zz, LANES - shift, 1) * s1 + pltpu.roll(zz, shift, 1) * s2)
    return jnp.concatenate(outs, axis=1)


def _inproj_kernel(x_ref, g_ref, w_ref, ca_ref, sa1_ref, sa2_ref, cr_ref, sr1_ref, sr2_ref,
                   qa_ref, ka_ref, va_ref, qr_ref, kr_ref, vr_ref, gr_ref, ga_ref, gb_ref,
                   qn_ref, kn_ref, *, nps):
    x = x_ref[...]
    xn = x * lax.rsqrt(jnp.mean(x * x, axis=-1, keepdims=True) + EPS)
    xn = (xn * g_ref[...]).astype(bf16)

    def proj(off, width):
        return jnp.dot(xn, w_ref[:, off:off + width], preferred_element_type=f32)

    ca, sa1, sa2 = ca_ref[...], sa1_ref[...], sa2_ref[...]
    cr, sr1, sr2 = cr_ref[...], sr1_ref[...], sr2_ref[...]
    qa = (_rotary(proj(OFF_QA, A_QK_W), ca, sa1, sa2, A_ROT // 2) * (A_HD ** -0.5)).astype(bf16)
    ka = _rotary(proj(OFF_KA, A_QK_W), ca, sa1, sa2, A_ROT // 2).astype(bf16)
    qa_ref[...] = qa
    ka_ref[...] = ka

    seg = (lax.broadcasted_iota(jnp.int32, (A_QK_W, LANES), 0) // A_HD
           == lax.broadcasted_iota(jnp.int32, (A_QK_W, LANES), 1))
    ind = jnp.where(seg, 1.0, 0.0).astype(bf16)

    def norm2max(z):
        n2 = jnp.dot(z * z, ind, preferred_element_type=f32)
        return jnp.max(n2, axis=0, keepdims=True)

    qmax, kmax = norm2max(qa), norm2max(ka)
    first = pl.program_id(0) % nps == 0

    @pl.when(first)
    def _():
        qn_ref[0] = qmax
        kn_ref[0] = kmax

    @pl.when(jnp.logical_not(first))
    def _():
        qn_ref[0] = jnp.maximum(qn_ref[0], qmax)
        kn_ref[0] = jnp.maximum(kn_ref[0], kmax)

    va_ref[...] = proj(OFF_VA, A_V_W).astype(bf16)
    qr_ref[...] = _rotary(proj(OFF_QR, R_QK_W), cr, sr1, sr2, R_DK // 2)
    kr_ref[...] = _rotary(proj(OFF_KR, R_QK_W), cr, sr1, sr2, R_DK // 2) * (R_DK ** -0.5)
    vr_ref[...] = proj(OFF_VR, R_V_W).astype(bf16)
    gr = proj(OFF_GR, R_V_W)
    gr_ref[...] = (gr * _sigmoid(gr)).astype(bf16)
    ga_ref[...] = _sigmoid(proj(OFF_GA, D_MODEL)).astype(bf16)
    gb_ref[...] = _sigmoid(proj(OFF_GB, D_MODEL)).astype(bf16)


def _rot_tables(S, rot_dim, theta, head_dim):
    inv = theta ** (-jnp.arange(0, rot_dim, 2, dtype=f32) / rot_dim)
    pos = jnp.arange(S, dtype=f32)
    ang = pos[:, None] * inv[None, :]
    cos, sin = jnp.cos(ang), jnp.sin(ang)
    half = rot_dim // 2
    pad = head_dim - rot_dim
    c = jnp.concatenate([cos, cos, jnp.ones((S, pad), f32)], axis=1)
    s1 = jnp.concatenate([-sin, jnp.zeros((S, half + pad), f32)], axis=1)
    s2 = jnp.concatenate([jnp.zeros((S, half), f32), sin, jnp.zeros((S, pad), f32)], axis=1)
    rep = LANES // head_dim
    return tuple(jnp.tile(t, (1, rep)) for t in (c, s1, s2))


def _inproj_call(x2d, S, g, w_in_bf, tabs_a, tabs_r, tm):
    T = x2d.shape[0]
    nps = S // tm
    row = lambda i: (i, 0)
    posmap = lambda i: (i % nps, 0)
    const = lambda i: (0, 0)
    tab_spec = pl.BlockSpec((tm, LANES), posmap)
    outs = [
        (A_QK_W, bf16), (A_QK_W, bf16), (A_V_W, bf16),
        (R_QK_W, f32), (R_QK_W, f32), (R_V_W, bf16), (R_V_W, bf16),
        (D_MODEL, bf16), (D_MODEL, bf16),
    ]
    nrm_spec = pl.BlockSpec((1, 1, LANES), lambda i: (i // nps, 0, 0))
    nrm_shape = jax.ShapeDtypeStruct((T // S, 1, LANES), f32)
    return pl.pallas_call(
        functools.partial(_inproj_kernel, nps=nps),
        grid=(T // tm,),
        in_specs=[pl.BlockSpec((tm, D_MODEL), row),
                  pl.BlockSpec((1, D_MODEL), const),
                  pl.BlockSpec((D_MODEL, D_IN), const)] + [tab_spec] * 6,
        out_specs=[pl.BlockSpec((tm, w), row) for w, _ in outs] + [nrm_spec, nrm_spec],
        out_shape=[jax.ShapeDtypeStruct((T, w), dt) for w, dt in outs] + [nrm_shape, nrm_shape],
        compiler_params=_cparams(("arbitrary",)),
        name="inproj",
    )(x2d, g, w_in_bf, *tabs_a, *tabs_r)


def _attn_kernel(kmax_ref, safe_ref, lam_ref, q_ref, k_ref, v_ref, g_ref, o_ref, vaug_ref, *, tq):
    b, h, qi = pl.program_id(0), pl.program_id(1), pl.program_id(2)
    S = v_ref.shape[0]

    @pl.when(qi == 0)
    def _():
        vaug_ref[:, :LANES] = v_ref[...]
        vaug_ref[:, LANES:] = jnp.ones((S, LANES), bf16)

    def stacked_q():
        q = q_ref[...]
        lane = lax.broadcasted_iota(jnp.int32, q.shape, 1)
        zero = jnp.zeros_like(q)
        return jnp.concatenate([jnp.where(lane < A_HD, q, zero), jnp.where(lane >= A_HD, q, zero)], axis=0)

    def scores(qs):
        return lax.dot_general(qs, k_ref[...], (((1,), (1,)), ((), ())), preferred_element_type=f32)

    def finish(e):
        oa = jnp.dot(e.astype(bf16), vaug_ref[...], preferred_element_type=f32)
        o = oa[:, :LANES] / oa[:, LANES:]
        a = o[:tq] - lam_ref[0] * o[tq:]
        y = a * lax.rsqrt(jnp.mean(a * a, axis=-1, keepdims=True) + EPS)
        o_ref[...] = ((y * g_ref[...]) * (1.0 - LAM_INIT)).astype(o_ref.dtype)

    safe = safe_ref[b * A_HEADS + h] == 1

    @pl.when(safe)
    def _():
        qs = stacked_q()
        qf = qs.astype(f32)
        qn = jnp.sqrt(jnp.sum(qf * qf, axis=-1, keepdims=True))
        row = lax.broadcasted_iota(jnp.int32, qn.shape, 0)
        kbase = (b * A_HEADS + h) * 2
        shift = qn * jnp.where(row < tq, kmax_ref[kbase], kmax_ref[kbase + 1])
        finish(jnp.exp(scores(qs) - shift))

    @pl.when(jnp.logical_not(safe))
    def _():
        s = scores(stacked_q())
        finish(jnp.exp(s - jnp.max(s, axis=-1, keepdims=True)))


ATTN_SAFE_BOUND = 40.0


def _attn_call(lam, qa, ka, va, qn2, kn2, subln_g, B, S, tq):
    T = qa.shape[0]
    nq = S // tq
    nsub = 2 * A_HEADS
    qmax = jnp.sqrt(qn2[:, 0, :nsub])
    kmax = jnp.sqrt(kn2[:, 0, :nsub])
    safe = jnp.all((qmax * kmax).reshape(B, A_HEADS, 2) < ATTN_SAFE_BOUND, axis=-1)
    return pl.pallas_call(
        functools.partial(_attn_kernel, tq=tq),
        grid_spec=pltpu.PrefetchScalarGridSpec(
            num_scalar_prefetch=2,
            grid=(B, A_HEADS, nq),
            in_specs=[pl.BlockSpec(memory_space=pltpu.SMEM),
                      pl.BlockSpec((tq, LANES), lambda b, h, i, km, sf: (b * nq + i, h)),
                      pl.BlockSpec((S, LANES), lambda b, h, i, km, sf: (b, h)),
                      pl.BlockSpec((S, LANES), lambda b, h, i, km, sf: (b, h)),
                      pl.BlockSpec((1, LANES), lambda b, h, i, km, sf: (0, 0))],
            out_specs=pl.BlockSpec((tq, LANES), lambda b, h, i, km, sf: (b * nq + i, h)),
            scratch_shapes=[pltpu.VMEM((S, 2 * LANES), bf16)],
        ),
        out_shape=jax.ShapeDtypeStruct((T, A_V_W), bf16),
        compiler_params=_cparams(("arbitrary", "arbitrary", "arbitrary")),
        name="attn",
    )(kmax.reshape(-1), safe.reshape(-1).astype(jnp.int32), lam, qa, ka, va, subln_g)


def _ret_kernel(df_ref, db_ref, q_ref, k_ref, v_ref, gr_ref, o_ref, uf_ref, ub_ref):
    p = pl.program_id(1)
    C = R_CHUNK
    S = q_ref.shape[0]
    n = S // C
    lane = lax.broadcasted_iota(jnp.int32, (C, LANES), 1)
    rowi = lax.broadcasted_iota(jnp.int32, (C, LANES), 0)
    first = lane < R_DK
    ones = jnp.ones((C, LANES), f32)
    lgf_a, lgf_b = -jnp.exp(ones * df_ref[2 * p]), -jnp.exp(ones * df_ref[2 * p + 1])
    lgb_a, lgb_b = -jnp.exp(ones * db_ref[2 * p]), -jnp.exp(ones * db_ref[2 * p + 1])
    lgf = jnp.where(first, lgf_a, lgf_b)
    lgb = jnp.where(first, lgb_a, lgb_b)
    idx = rowi.astype(f32)
    k_f = jnp.exp(lgf * (C - 1 - idx))
    k_b = jnp.exp(lgb * idx)
    q_f = jnp.exp(lgf * (idx + 1))
    q_b = jnp.exp(lgb * (C - idx))
    top = rowi < R_DK
    g_f = jnp.where(top, jnp.exp(lgf_a * C), jnp.exp(lgf_b * C))
    g_b = jnp.where(top, jnp.exp(lgb_a * C), jnp.exp(lgb_b * C))
    g_f = jnp.concatenate([g_f, g_f], axis=1)
    g_b = jnp.concatenate([g_b, g_b], axis=1)
    rel = (rowi - lane).astype(f32)

    def decay_mask(lf, lb):
        return jnp.where(rel >= 0, jnp.exp(lf * jnp.maximum(rel, 0.0)), jnp.exp(lb * jnp.maximum(-rel, 0.0)))

    mask = jnp.concatenate([decay_mask(lgf_a, lgb_a), decay_mask(lgf_b, lgb_b)], axis=0)

    def summaries(c, carry):
        r0 = pl.multiple_of(c * C, C)
        kc = k_ref[pl.ds(r0, C), :]
        vc = v_ref[pl.ds(r0, C), :]
        kt = jnp.concatenate([(kc * k_f).T, (kc * k_b).T], axis=0).astype(bf16)
        uu = jnp.dot(kt, vc, preferred_element_type=f32)
        uf_ref[c] = uu[:LANES]
        ub_ref[c] = uu[LANES:]
        return carry

    lax.fori_loop(0, n, summaries, 0, unroll=RET_UNROLL)

    def scan_f(c, st):
        u = uf_ref[c]
        uf_ref[c] = st
        return g_f * st + u

    lax.fori_loop(0, n, scan_f, jnp.zeros((LANES, 2 * LANES), f32))

    def scan_b(j, st):
        c = n - 1 - j
        u = ub_ref[c]
        ub_ref[c] = st
        return g_b * st + u

    lax.fori_loop(0, n, scan_b, jnp.zeros((LANES, 2 * LANES), f32))

    def outputs(c, carry):
        r0 = pl.multiple_of(c * C, C)
        qc = q_ref[pl.ds(r0, C), :]
        kc = k_ref[pl.ds(r0, C), :]
        vc = v_ref[pl.ds(r0, C), :]
        zero = jnp.zeros_like(qc)
        q2 = jnp.concatenate([jnp.where(first, qc, zero), jnp.where(first, zero, qc)], axis=0)
        s = lax.dot_general(q2.astype(bf16), kc.astype(bf16), (((1,), (1,)), ((), ())),
                            preferred_element_type=f32)
        qf2 = jnp.concatenate([q_f, q_f], axis=0)
        qb2 = jnp.concatenate([q_b, q_b], axis=0)
        state_q = q2 * qf2 + pltpu.roll(q2 * qb2, R_DK, 1)
        lhs = jnp.concatenate([s * mask, state_q], axis=1).astype(bf16)
        uf, ub = uf_ref[c], ub_ref[c]
        states = jnp.concatenate([
            jnp.concatenate([uf[:R_DK, :LANES], ub[R_DK:, LANES:]], axis=1),
            jnp.concatenate([ub[:R_DK, :LANES], uf[R_DK:, LANES:]], axis=1)], axis=0)
        rhs = jnp.concatenate([vc, states.astype(bf16)], axis=0)
        yy = jnp.dot(lhs, rhs, preferred_element_type=f32)
        ya = yy[:C, :LANES]
        yb = yy[C:, LANES:]
        ya = ya * lax.rsqrt(jnp.mean(ya * ya, axis=-1, keepdims=True) + EPS)
        yb = yb * lax.rsqrt(jnp.mean(yb * yb, axis=-1, keepdims=True) + EPS)
        y = jnp.concatenate([ya, yb], axis=1)
        o_ref[pl.ds(r0, C), :] = (gr_ref[pl.ds(r0, C), :].astype(f32) * y).astype(o_ref.dtype)
        return carry

    lax.fori_loop(0, n, outputs, 0, unroll=RET_UNROLL)


def _ret_call(decay_f, decay_b, qr, kr, vr, gr, B, S):
    T = qr.shape[0]
    n = S // R_CHUNK
    blk = lambda w: pl.BlockSpec((S, w), lambda b, p: (b, p))
    return pl.pallas_call(
        _ret_kernel,
        grid=(B, R_HEADS // 2),
        in_specs=[pl.BlockSpec(memory_space=pltpu.SMEM), pl.BlockSpec(memory_space=pltpu.SMEM),
                  blk(LANES), blk(LANES), blk(2 * LANES), blk(2 * LANES)],
        out_specs=blk(2 * LANES),
        out_shape=jax.ShapeDtypeStruct((T, R_V_W), bf16),
        scratch_shapes=[pltpu.VMEM((n, LANES, 2 * LANES), f32), pltpu.VMEM((n, LANES, 2 * LANES), f32)],
        compiler_params=_cparams(("arbitrary", "arbitrary")),
        name="ret",
    )(decay_f, decay_b, qr, kr, vr, gr)


def _outproj_kernel(oa_ref, ob_ref, ga_ref, gb_ref, x_ref, wpa_ref, wpb_ref, wo_ref, g_ref,
                    wr_ref, br_ref, cnt0_ref,
                    x1_ref, row_ref, meta_ref, cnt_ref, tri_ref, carry_ref, *, tm):
    i = pl.program_id(0)

    @pl.when(i == 0)
    def _():
        r = lax.broadcasted_iota(jnp.int32, (tm, tm), 0)
        c = lax.broadcasted_iota(jnp.int32, (tm, tm), 1)
        tri_ref[...] = jnp.where(r < c, 1.0, 0.0).astype(bf16)
        carry_ref[...] = cnt0_ref[...]

    pa = jnp.dot(oa_ref[...], wpa_ref[...], preferred_element_type=f32)
    pb = jnp.dot(ob_ref[...], wpb_ref[...], preferred_element_type=f32)
    merged = ga_ref[...].astype(f32) * pa + gb_ref[...].astype(f32) * pb
    x1 = x_ref[...] + jnp.dot(merged.astype(bf16), wo_ref[...], preferred_element_type=f32)
    x1_ref[...] = x1
    xn = x1 * lax.rsqrt(jnp.mean(x1 * x1, axis=-1, keepdims=True) + EPS)
    xn = xn * g_ref[...]
    row_ref[:, :HALF] = _pack_halves(xn)

    xh = xn.astype(bf16)
    xl = (xn - xh.astype(f32)).astype(bf16)
    parts = jnp.dot(jnp.concatenate([xh, xl], axis=0), wr_ref[...], preferred_element_type=f32)
    logits = parts[:tm, :LANES] + parts[:tm, LANES:] + parts[tm:, :LANES] + br_ref[...]

    lt = logits.T
    row = lax.broadcasted_iota(jnp.int32, (LANES, tm), 0)
    row_f = row.astype(f32)
    ninf = jnp.full((LANES, tm), -jnp.inf, f32)
    big = jnp.full((LANES, tm), float(LANES - 1), f32)

    def first_row(hit):
        return jnp.min(jnp.where(hit, row_f, big), axis=0, keepdims=True).astype(jnp.int32)

    isg = row < N_GROUPS
    lg = jnp.where(isg, lt, ninf)
    mg = jnp.max(lg, axis=0, keepdims=True)
    gsel = first_row(lg == mg)
    pg_top = 1.0 / jnp.sum(jnp.where(isg, jnp.exp(lt - mg), 0.0), axis=0, keepdims=True)
    lo = N_GROUPS + EXPERTS_PER_GROUP * gsel
    le = jnp.where((row >= lo) & (row < lo + EXPERTS_PER_GROUP), lt, ninf)
    m1 = jnp.max(le, axis=0, keepdims=True)
    i1 = first_row(le == m1)
    le2 = jnp.where(row == i1, ninf, le)
    m2 = jnp.max(le2, axis=0, keepdims=True)
    i2 = first_row(le2 == m2)
    t2 = jnp.exp(m2 - m1)
    gate1 = pg_top / (1.0 + t2)
    gate2 = pg_top * t2 / (1.0 + t2)
    e1 = i1 - lo
    e2 = i2 - lo
    ea = jnp.minimum(e1, e2)
    eb = jnp.maximum(e1, e2)
    gate_a = jnp.where(e1 < e2, gate1, gate2)
    gate_b = jnp.where(e1 < e2, gate2, gate1)
    pair = lax.shift_right_logical(ea * (2 * EXPERTS_PER_GROUP - 1 - ea), 1) + (eb - ea - 1)
    cls = gsel * N_PAIRS + pair
    gates_t = jnp.where(row == 0, gate_a, jnp.where(row == 1, gate_b, 0.0))
    row_ref[:, HALF:] = lax.bitcast_convert_type(gates_t.T, u32)

    hit = row == cls
    onehot = jnp.where(hit, 1.0, 0.0)
    carry = jnp.concatenate([carry_ref[...]] * (tm // LANES), axis=1)
    rank = jnp.dot(onehot.astype(bf16), tri_ref[...], preferred_element_type=f32) + carry
    pos = jnp.sum(jnp.where(hit, rank, 0.0), axis=0, keepdims=True).astype(jnp.int32)
    carry_ref[...] = carry_ref[...] + jnp.sum(onehot, axis=1, keepdims=True)
    cnt_ref[...] = carry_ref[...]
    row8 = lax.broadcasted_iota(jnp.int32, (8, tm), 0)
    meta = jnp.where(row8 == 0, cls, jnp.where(row8 == 1, pos, 0))
    for c in range(tm // IDX_TILE):
        meta_ref[c] = meta[:, c * IDX_TILE:(c + 1) * IDX_TILE]


def _outproj_call(oa, ob, ga, gb, x2d, wpa, wpb, wo, g, wr, br, cnt0, tm):
    T = x2d.shape[0]
    row = lambda i: (i, 0)
    const = lambda i: (0, 0)
    return pl.pallas_call(
        functools.partial(_outproj_kernel, tm=tm),
        grid=(T // tm,),
        in_specs=[pl.BlockSpec((tm, A_V_W), row), pl.BlockSpec((tm, R_V_W), row),
                  pl.BlockSpec((tm, D_MODEL), row), pl.BlockSpec((tm, D_MODEL), row),
                  pl.BlockSpec((tm, D_MODEL), row),
                  pl.BlockSpec((A_V_W, D_MODEL), const), pl.BlockSpec((R_V_W, D_MODEL), const),
                  pl.BlockSpec((D_MODEL, D_MODEL), const), pl.BlockSpec((1, D_MODEL), const),
                  pl.BlockSpec((D_MODEL, 2 * LANES), const), pl.BlockSpec((1, LANES), const),
                  pl.BlockSpec((LANES, LANES), const)],
        out_specs=[pl.BlockSpec((tm, D_MODEL), row), pl.BlockSpec((tm, ROW_W), row),
                   pl.BlockSpec((tm // IDX_TILE, 8, IDX_TILE), lambda i: (i, 0, 0)),
                   pl.BlockSpec((LANES, LANES), const)],
        out_shape=[jax.ShapeDtypeStruct((T, D_MODEL), f32), jax.ShapeDtypeStruct((T, ROW_W), u32),
                   jax.ShapeDtypeStruct((T // IDX_TILE, 8, IDX_TILE), jnp.int32),
                   jax.ShapeDtypeStruct((LANES, LANES), f32)],
        scratch_shapes=[pltpu.VMEM((tm, tm), bf16), pltpu.VMEM((LANES, LANES), f32)],
        compiler_params=_cparams(("arbitrary",)),
        name="outproj",
    )(oa, ob, ga, gb, x2d, wpa, wpb, wo, g, wr, br, cnt0)


def _dest_kernel(pstart_ref, meta_ref, o_ref):
    cls = meta_ref[:, 0, :]
    rank = meta_ref[:, 1, :]

    def body(c, acc):
        return acc + jnp.where(cls == c, pstart_ref[c], 0)

    o_ref[:, 0, :] = lax.fori_loop(0, N_CLASSES, body, rank)


def _dest_call(pstart, meta):
    nt = meta.shape[0]
    blk = min(nt, DEST_TILES)
    return pl.pallas_call(
        _dest_kernel,
        grid=(nt // blk,),
        in_specs=[pl.BlockSpec(memory_space=pltpu.SMEM),
                  pl.BlockSpec((blk, 8, IDX_TILE), lambda i: (i, 0, 0))],
        out_specs=pl.BlockSpec((blk, 1, IDX_TILE), lambda i: (i, 0, 0)),
        out_shape=jax.ShapeDtypeStruct((nt, 1, IDX_TILE), jnp.int32),
        compiler_params=_cparams(("arbitrary",)),
        name="dest",
    )(pstart, meta)


def _dispatch_kernel(lastblk_ref, dest_ref, *refs, tiles):
    rows_refs = refs[:len(tiles)]
    xs_ref, buf_ref, zero_ref, sem_in, sem_out, zsem = refs[len(tiles):]
    i = pl.program_id(0)
    n = sum(tiles)
    slot = i % 2

    def tile_copy(rows_ref, t, s):
        r0 = pl.multiple_of(t * IDX_TILE, IDX_TILE)
        return pltpu.make_async_copy(rows_ref.at[pl.ds(r0, IDX_TILE), :], buf_ref.at[s], sem_in.at[s])

    def load_start(t, s):
        t = jnp.asarray(t, jnp.int32)
        first = 0
        for rows_ref, cnt in zip(rows_refs, tiles):
            @pl.when((t >= first) & (t < first + cnt))
            def _(rows_ref=rows_ref, first=first):
                tile_copy(rows_ref, t - first, s).start()
            first += cnt

    def load_wait(s):
        tile_copy(rows_refs[0], 0, s).wait()

    def scatter_wait(s):
        pltpu.make_async_copy(buf_ref.at[s], xs_ref.at[pl.ds(0, IDX_TILE), :], sem_out.at[s]).wait()

    @pl.when(i == 0)
    def _():
        load_start(0, 0)
        zero_ref[...] = jnp.zeros_like(zero_ref)

        def zero_block(j):
            off = pl.multiple_of(j * MOE_BLK, MOE_BLK)
            pltpu.make_async_copy(zero_ref, xs_ref.at[pl.ds(off, MOE_BLK), :], zsem).start()

        def zfill_last(c, carry):
            j = lastblk_ref[c]

            @pl.when(j >= 0)
            def _():
                zero_block(j)

            return carry

        def zfill_unused(j, carry):
            zero_block(j)
            return carry

        def zwait(c, carry):
            pltpu.make_async_copy(zero_ref, xs_ref.at[pl.ds(0, MOE_BLK), :], zsem).wait()
            return carry

        n_used = lastblk_ref[N_CLASSES]
        n_blocks = xs_ref.shape[0] // MOE_BLK
        lax.fori_loop(0, N_CLASSES, zfill_last, 0)
        lax.fori_loop(n_used, n_blocks, zfill_unused, 0)
        lax.fori_loop(0, lastblk_ref[N_CLASSES + 1] + n_blocks - n_used, zwait, 0)

    @pl.when(i >= 1)
    def _():
        scatter_wait(1 - slot)

    @pl.when(i + 1 < n)
    def _():
        load_start(i + 1, 1 - slot)

    load_wait(slot)
    for r in range(IDX_TILE):
        d = dest_ref[0, 0, r]
        pltpu.make_async_copy(buf_ref.at[slot, pl.ds(r, 1), :], xs_ref.at[pl.ds(d, 1), :],
                              sem_out.at[slot]).start(priority=r % 2)

    @pl.when(i == n - 1)
    def _():
        scatter_wait(slot)


def _dispatch_call(lastblk, dests, rows, P):
    tiles = tuple(r.shape[0] // IDX_TILE for r in rows)
    any_spec = pl.BlockSpec(memory_space=pl.ANY)
    return pl.pallas_call(
        functools.partial(_dispatch_kernel, tiles=tiles),
        grid=(sum(tiles),),
        in_specs=[pl.BlockSpec(memory_space=pltpu.SMEM),
                  pl.BlockSpec((1, 1, IDX_TILE), lambda i: (i, 0, 0), memory_space=pltpu.SMEM)]
                 + [any_spec] * len(rows),
        out_specs=any_spec,
        scratch_shapes=[pltpu.VMEM((2, IDX_TILE, ROW_W), u32), pltpu.VMEM((MOE_BLK, ROW_W), u32),
                        pltpu.SemaphoreType.DMA((2,)), pltpu.SemaphoreType.DMA((2,)),
                        pltpu.SemaphoreType.DMA(())],
        out_shape=jax.ShapeDtypeStruct((P, ROW_W), u32),
        compiler_params=_cparams(("arbitrary",), has_side_effects=True),
        name="dispatch",
    )(lastblk, jnp.concatenate(dests, axis=0), *rows)


def _expert_kernel(ea_ref, eb_ref, nb_ref, xs_ref, w1a_ref, w3a_ref, w2a_ref, w1b_ref, w3b_ref, w2b_ref,
                   y_ref):
    j = pl.program_id(0)

    @pl.when(j < nb_ref[0])
    def _():
        x = _unpack_halves(xs_ref[:, :HALF]).astype(bf16)
        gates = lax.bitcast_convert_type(xs_ref[:, HALF:], f32)
        ga = gates[:, 0:1]
        gb = gates[:, 1:2]

        def ffn(w1_ref, w3_ref, w2_ref, gate):
            a = jnp.dot(x, w1_ref[0], preferred_element_type=f32)
            b = jnp.dot(x, w3_ref[0], preferred_element_type=f32)
            h = ((a * _sigmoid(a)) * b) * gate
            return jnp.dot(h.astype(bf16), w2_ref[0], preferred_element_type=f32)

        y_ref[...] = _pack_halves(ffn(w1a_ref, w3a_ref, w2a_ref, ga) + ffn(w1b_ref, w3b_ref, w2b_ref, gb))

    @pl.when(j >= nb_ref[0])
    def _():
        y_ref[...] = jnp.zeros_like(y_ref)


def _expert_call(blk_ea, blk_eb, n_used, xs, w1, w3, w2):
    P = xs.shape[0]
    nb = P // MOE_BLK
    wa = lambda j, ea, eb, n: (ea[j], 0, 0)
    wb = lambda j, ea, eb, n: (eb[j], 0, 0)
    up = (1, D_MODEL, D_EXPERT)
    down = (1, D_EXPERT, D_MODEL)
    return pl.pallas_call(
        _expert_kernel,
        grid_spec=pltpu.PrefetchScalarGridSpec(
            num_scalar_prefetch=3,
            grid=(nb,),
            in_specs=[pl.BlockSpec((MOE_BLK, ROW_W), lambda j, ea, eb, n: (jnp.minimum(j, n[0] - 1), 0)),
                      pl.BlockSpec(up, wa), pl.BlockSpec(up, wa), pl.BlockSpec(down, wa),
                      pl.BlockSpec(up, wb), pl.BlockSpec(up, wb), pl.BlockSpec(down, wb)],
            out_specs=pl.BlockSpec((MOE_BLK, HALF), lambda j, ea, eb, n: (j, 0)),
        ),
        out_shape=jax.ShapeDtypeStruct((P, HALF), u32),
        compiler_params=_cparams(("arbitrary",)),
        name="experts",
    )(blk_ea, blk_eb, n_used, xs, w1, w3, w2, w1, w3, w2)


def _combine_kernel(dest_ref, x1_ref, g_ref, y_ref, o_ref, buf_ref, sems, *, n):
    i = pl.program_id(0)
    slot = i % 2

    @pl.when(i < n)
    def _():
        for r in range(IDX_TILE):
            d = dest_ref[0, 0, r]
            pltpu.make_async_copy(y_ref.at[pl.ds(d, 1), :], buf_ref.at[slot, pl.ds(r, 1), :],
                                  sems.at[slot]).start(priority=r % 2)

    @pl.when(i >= 1)
    def _():
        prev = 1 - slot
        pltpu.make_async_copy(y_ref.at[pl.ds(0, IDX_TILE), :], buf_ref.at[prev], sems.at[prev]).wait()
        x = x1_ref[...] + _unpack_halves(buf_ref[prev])
        y = x * lax.rsqrt(jnp.mean(x * x, axis=-1, keepdims=True) + EPS)
        o_ref[...] = y * g_ref[...]


def _combine_call(dest, x1, g, y):
    T = x1.shape[0]
    n = T // IDX_TILE
    done = lambda i: (jnp.maximum(i - 1, 0), 0)
    return pl.pallas_call(
        functools.partial(_combine_kernel, n=n),
        grid=(n + 1,),
        in_specs=[pl.BlockSpec((1, 1, IDX_TILE), lambda i: (jnp.minimum(i, n - 1), 0, 0), memory_space=pltpu.SMEM),
                  pl.BlockSpec((IDX_TILE, D_MODEL), done),
                  pl.BlockSpec((1, D_MODEL), lambda i: (0, 0)),
                  pl.BlockSpec(memory_space=pl.ANY)],
        out_specs=pl.BlockSpec((IDX_TILE, D_MODEL), done),
        scratch_shapes=[pltpu.VMEM((2, IDX_TILE, HALF), u32), pltpu.SemaphoreType.DMA((2,))],
        out_shape=jax.ShapeDtypeStruct((T, D_MODEL), f32),
        compiler_params=_cparams(("arbitrary",)),
        name="combine",
    )(dest, x1, g, y)


def _pair_table():
    pa, pb = [], []
    for a in range(EXPERTS_PER_GROUP):
        for b in range(a + 1, EXPERTS_PER_GROUP):
            pa.append(a)
            pb.append(b)
    return np.asarray(pa, np.int32), np.asarray(pb, np.int32)


def _route_plan(counts, T):
    nb = T // MOE_BLK + N_CLASSES
    cnt = counts[:N_CLASSES, 0].astype(jnp.int32)
    nblk = (cnt + MOE_BLK - 1) // MOE_BLK
    cum = jnp.cumsum(nblk)
    pstart = jnp.pad((cum - nblk) * MOE_BLK, (0, LANES - N_CLASSES)).astype(jnp.int32)
    blk = jnp.minimum(jnp.arange(nb, dtype=jnp.int32), cum[-1] - 1)
    blk_cls = jnp.minimum(jnp.sum((cum[None, :] <= blk[:, None]).astype(jnp.int32), axis=1), N_CLASSES - 1)
    pa, pb = _pair_table()
    grp = blk_cls // N_PAIRS
    pr = blk_cls % N_PAIRS
    blk_ea = grp * EXPERTS_PER_GROUP + jnp.asarray(pa)[pr]
    blk_eb = grp * EXPERTS_PER_GROUP + jnp.asarray(pb)[pr]
    n_used = cum[-1:].astype(jnp.int32)
    lastblk = jnp.pad(jnp.concatenate([jnp.where(cnt > 0, cum - 1, -1), cum[-1:], jnp.sum(cnt > 0)[None]]),
                      (0, LANES - N_CLASSES - 2)).astype(jnp.int32)
    return pstart, lastblk, blk_ea.astype(jnp.int32), blk_eb.astype(jnp.int32), n_used, nb * MOE_BLK


ATTN_SCORE_ELEMS = 1 << 21


def _mixer(x, lam, wts, tabs, counts0, tm, tq):
    B, S, _ = x.shape
    tq = tq or min(S, ATTN_SCORE_ELEMS // S)
    T = B * S
    x2d = x.reshape(T, D_MODEL)
    tabs_a, tabs_r = tabs
    qa, ka, va, qr, kr, vr, gr, ga, gb, qn2, kn2 = _inproj_call(
        x2d, S, wts['attn_norm_g'], wts['w_in'], [t[:S] for t in tabs_a], [t[:S] for t in tabs_r], tm)
    oa = _attn_call(lam, qa, ka, va, qn2, kn2, wts['subln_g'], B, S, tq)
    ob = _ret_call(wts['ret_decay_f'], wts['ret_decay_b'], qr, kr, vr, gr, B, S)
    return _outproj_call(oa, ob, ga, gb, x2d, wts['w_pa'], wts['w_pb'], wts['w_out'], wts['ffn_norm_g'],
                         wts['wr'], wts['b_r'], counts0, tm)


def _layer(xs_in, lam, wts, tabs, tm=512, tq=None):
    counts = jnp.zeros((LANES, LANES), f32)
    mixed = []
    for x in xs_in:
        x1, rows, meta, counts = _mixer(x, lam, wts, tabs, counts, tm, tq)
        mixed.append((x1, rows, meta))
    total = sum(x.shape[0] * x.shape[1] for x in xs_in)
    pstart, lastblk, blk_ea, blk_eb, n_used, P = _route_plan(counts, total)
    dests = [_dest_call(pstart, meta) for _, _, meta in mixed]
    xs = _dispatch_call(lastblk, dests, [rows for _, rows, _ in mixed], P)
    y = _expert_call(blk_ea, blk_eb, n_used, xs, wts['w1'], wts['w3'], wts['w2'])
    return tuple(_combine_call(dest, x1, wts['final_norm_g'], y).reshape(x.shape)
                 for (x1, _, _), dest, x in zip(mixed, dests, xs_in))


def kernel(x_prompt, x_sample, attn_norm_g, w_in, lam_q1, lam_k1, lam_q2, lam_k2, subln_g,
           ret_decay_f, ret_decay_b, w_pa, w_pb, w_out, ffn_norm_g, w_rg, b_rg, w_re, b_re,
           w1, w3, w2, final_norm_g):
    wr = jnp.concatenate([w_rg[0], jnp.transpose(w_re[0], (1, 0, 2)).reshape(D_MODEL, N_EXPERTS)], axis=1)
    wr = jnp.pad(wr, ((0, 0), (0, LANES - wr.shape[1])))
    wr_hi = wr.astype(bf16)
    b_r = jnp.pad(jnp.concatenate([b_rg[0], b_re[0].reshape(-1)]), (0, LANES - N_GROUPS - N_EXPERTS))
    wts = dict(
        attn_norm_g=attn_norm_g[0][None, :], w_in=w_in[0].astype(bf16), subln_g=subln_g[0][None, :],
        ret_decay_f=ret_decay_f[0], ret_decay_b=ret_decay_b[0],
        w_pa=w_pa[0].astype(bf16), w_pb=w_pb[0].astype(bf16), w_out=w_out[0].astype(bf16),
        ffn_norm_g=ffn_norm_g[0][None, :],
        wr=jnp.concatenate([wr_hi, (wr - wr_hi.astype(f32)).astype(bf16)], axis=1),
        b_r=b_r[None, :], w1=w1[0].astype(bf16), w3=w3[0].astype(bf16), w2=w2[0].astype(bf16),
        final_norm_g=final_norm_g[None, :])
    lam = _lam_call(lam_q1, lam_k1, lam_q2, lam_k2)[0, :1]
    s_max = max(x_prompt.shape[1], x_sample.shape[1])
    tabs = (_rot_tables(s_max, A_ROT, A_THETA, A_HD), _rot_tables(s_max, R_DK, R_THETA, R_DK))
    return _layer((x_prompt, x_sample), lam, wts, tabs)
```

```python
import functools
import math

import numpy as np
import jax
import jax.numpy as jnp
from jax import lax
from jax.experimental import pallas as pl
from jax.experimental.pallas import tpu as pltpu

f32 = jnp.float32
bf16 = jnp.bfloat16

D_MODEL = 1024
A_HEADS = 4
A_HD = 64
A_VD = 128
A_ROT = 16
A_THETA = 500000.0
R_HEADS = 4
R_DK = 64
R_DV = 128
R_CHUNK = 128
R_THETA = 10000.0
N_GROUPS = 4
EXPERTS_PER_GROUP = 8
N_EXPERTS = 32
D_EXPERT = 512
EPS = 1e-6
LAM_INIT = 0.8 - 0.6 * math.exp(-0.3 * 0)

A_QK_W = 512
A_V_W = 512
R_QK_W = 256
R_V_W = 512
D_IN = 5120
OFF_QA, OFF_KA, OFF_VA, OFF_QR, OFF_KR, OFF_VR, OFF_GR, OFF_GA, OFF_GB = (
    0, 512, 1024, 1536, 1792, 2048, 2560, 3072, 4096)

LANES = 128
N_PAIRS = 28
N_CLASSES = N_GROUPS * N_PAIRS
HALF = D_MODEL // 2
ROW_W = HALF + LANES
u32 = jnp.uint32
BF16_BITS = 16
HIGH_HALF = 0xFFFF0000
MOE_BLK = 256
IDX_TILE = 512
DEST_TILES = 128
RET_UNROLL = 8
VMEM_LIMIT = 56 * 1024 * 1024


def _sigmoid(x):
    return 0.5 * jnp.tanh(0.5 * x) + 0.5


def _pack_halves(x):
    xb = x.astype(jnp.bfloat16).astype(f32)
    hi = lax.bitcast_convert_type(xb[:, :HALF], u32)
    lo = lax.bitcast_convert_type(xb[:, HALF:], u32)
    return hi | lax.shift_right_logical(lo, jnp.full_like(lo, BF16_BITS))


def _unpack_halves(w):
    hi = lax.bitcast_convert_type(w & jnp.full_like(w, HIGH_HALF), f32)
    lo = lax.bitcast_convert_type(lax.shift_left(w, jnp.full_like(w, BF16_BITS)), f32)
    return jnp.concatenate([hi, lo], axis=1)


def _cparams(sem, **kw):
    return pltpu.CompilerParams(dimension_semantics=sem, vmem_limit_bytes=VMEM_LIMIT, **kw)


def _lam_kernel(q1_ref, k1_ref, q2_ref, k2_ref, o_ref):
    a = jnp.sum(q1_ref[...] * k1_ref[...], axis=-1, keepdims=True)
    b = jnp.sum(q2_ref[...] * k2_ref[...], axis=-1, keepdims=True)
    lam = jnp.exp(a) - jnp.exp(b) + LAM_INIT
    o_ref[...] = jnp.broadcast_to(lam, o_ref.shape)


def _lam_call(q1, k1, q2, k2):
    return pl.pallas_call(
        _lam_kernel,
        out_shape=jax.ShapeDtypeStruct((8, LANES), f32),
        name="lam",
    )(q1, k1, q2, k2)


def _rotary(z, c, s1, s2, shift):
    outs = []
    for j in range(z.shape[1] // LANES):
        zz = z[:, j * LANES:(j + 1) * LANES]
        outs.append(zz * c + pltpu.roll(zz, LANES - shift, 1) * s1 + pltpu.roll(zz, shift, 1) * s2)
    return jnp.concatenate(outs, axis=1)


def _inproj_kernel(x_ref, g_ref, w_ref, ca_ref, sa1_ref, sa2_ref, cr_ref, sr1_ref, sr2_ref,
                   qa_ref, ka_ref, va_ref, qr_ref, kr_ref, vr_ref, gr_ref, ga_ref, gb_ref,
                   qn_ref, kn_ref, *, nps):
    x = x_ref[...]
    xn = x * lax.rsqrt(jnp.mean(x * x, axis=-1, keepdims=True) + EPS)
    xn = (xn * g_ref[...]).astype(bf16)

    def proj(off, width):
        return jnp.dot(xn, w_ref[:, off:off + width], preferred_element_type=f32)

    ca, sa1, sa2 = ca_ref[...], sa1_ref[...], sa2_ref[...]
    cr, sr1, sr2 = cr_ref[...], sr1_ref[...], sr2_ref[...]
    qa = (_rotary(proj(OFF_QA, A_QK_W), ca, sa1, sa2, A_ROT // 2) * (A_HD ** -0.5)).astype(bf16)
    ka = _rotary(proj(OFF_KA, A_QK_W), ca, sa1, sa2, A_ROT // 2).astype(bf16)
    qa_ref[...] = qa
    ka_ref[...] = ka

    seg = (lax.broadcasted_iota(jnp.int32, (A_QK_W, LANES), 0) // A_HD
           == lax.broadcasted_iota(jnp.int32, (A_QK_W, LANES), 1))
    ind = jnp.where(seg, 1.0, 0.0).astype(bf16)

    def norm2max(z):
        n2 = jnp.dot(z * z, ind, preferred_element_type=f32)
        return jnp.max(n2, axis=0, keepdims=True)

    qmax, kmax = norm2max(qa), norm2max(ka)

    va_ref[...] = proj(OFF_VA, A_V_W).astype(bf16)
    qr_ref[...] = _rotary(proj(OFF_QR, R_QK_W), cr, sr1, sr2, R_DK // 2)
    kr_ref[...] = _rotary(proj(OFF_KR, R_QK_W), cr, sr1, sr2, R_DK // 2) * (R_DK ** -0.5)
    vr_ref[...] = proj(OFF_VR, R_V_W).astype(bf16)
    gr = proj(OFF_GR, R_V_W)
    gr_ref[...] = (gr * _sigmoid(gr)).astype(bf16)
    ga_ref[...] = _sigmoid(proj(OFF_GA, D_MODEL)).astype(bf16)
    gb_ref[...] = _sigmoid(proj(OFF_GB, D_MODEL)).astype(bf16)

    first = pl.program_id(0) % nps == 0

    @pl.when(first)
    def _():
        qn_ref[0] = qmax
        kn_ref[0] = kmax

    @pl.when(jnp.logical_not(first))
    def _():
        qn_ref[0] = jnp.maximum(qn_ref[0], qmax)
        kn_ref[0] = jnp.maximum(kn_ref[0], kmax)


def _rot_tables(S, rot_dim, theta, head_dim):
    inv = theta ** (-jnp.arange(0, rot_dim, 2, dtype=f32) / rot_dim)
    pos = jnp.arange(S, dtype=f32)
    ang = pos[:, None] * inv[None, :]
    cos, sin = jnp.cos(ang), jnp.sin(ang)
    half = rot_dim // 2
    pad = head_dim - rot_dim
    c = jnp.concatenate([cos, cos, jnp.ones((S, pad), f32)], axis=1)
    s1 = jnp.concatenate([-sin, jnp.zeros((S, half + pad), f32)], axis=1)
    s2 = jnp.concatenate([jnp.zeros((S, half), f32), sin, jnp.zeros((S, pad), f32)], axis=1)
    rep = LANES // head_dim
    return tuple(jnp.tile(t, (1, rep)) for t in (c, s1, s2))


def _inproj_call(x2d, S, g, w_in_bf, tabs_a, tabs_r, tm):
    T = x2d.shape[0]
    nps = S // tm
    row = lambda i: (i, 0)
    posmap = lambda i: (i % nps, 0)
    const = lambda i: (0, 0)
    tab_spec = pl.BlockSpec((tm, LANES), posmap)
    outs = [
        (A_QK_W, bf16), (A_QK_W, bf16), (A_V_W, bf16),
        (R_QK_W, f32), (R_QK_W, f32), (R_V_W, bf16), (R_V_W, bf16),
        (D_MODEL, bf16), (D_MODEL, bf16),
    ]
    nrm_spec = pl.BlockSpec((1, 1, LANES), lambda i: (i // nps, 0, 0))
    nrm_shape = jax.ShapeDtypeStruct((T // S, 1, LANES), f32)
    return pl.pallas_call(
        functools.partial(_inproj_kernel, nps=nps),
        grid=(T // tm,),
        in_specs=[pl.BlockSpec((tm, D_MODEL), row),
                  pl.BlockSpec((1, D_MODEL), const),
                  pl.BlockSpec((D_MODEL, D_IN), const)] + [tab_spec] * 6,
        out_specs=[pl.BlockSpec((tm, w), row) for w, _ in outs] + [nrm_spec, nrm_spec],
        out_shape=[jax.ShapeDtypeStruct((T, w), dt) for w, dt in outs] + [nrm_shape, nrm_shape],
        compiler_params=_cparams(("arbitrary",)),
        name="inproj",
    )(x2d, g, w_in_bf, *tabs_a, *tabs_r)


def _attn_kernel(kmax_ref, safe_ref, lam_ref, q_ref, k_ref, v_ref, g_ref, o_ref, vaug_ref, *, tq):
    b, h, qi = pl.program_id(0), pl.program_id(1), pl.program_id(2)
    S = v_ref.shape[0]

    @pl.when(qi == 0)
    def _():
        vaug_ref[:, :LANES] = v_ref[...]
        vaug_ref[:, LANES:] = jnp.ones((S, LANES), bf16)

    def stacked_q():
        q = q_ref[...]
        lane = lax.broadcasted_iota(jnp.int32, q.shape, 1)
        zero = jnp.zeros_like(q)
        return jnp.concatenate([jnp.where(lane < A_HD, q, zero), jnp.where(lane >= A_HD, q, zero)], axis=0)

    def scores(qs):
        return lax.dot_general(qs, k_ref[...], (((1,), (1,)), ((), ())), preferred_element_type=f32)

    def finish(e):
        oa = jnp.dot(e.astype(bf16), vaug_ref[...], preferred_element_type=f32)
        o = oa[:, :LANES] / oa[:, LANES:]
        a = o[:tq] - lam_ref[0] * o[tq:]
        y = a * lax.rsqrt(jnp.mean(a * a, axis=-1, keepdims=True) + EPS)
        o_ref[...] = ((y * g_ref[...]) * (1.0 - LAM_INIT)).astype(o_ref.dtype)

    safe = safe_ref[b * A_HEADS + h] == 1

    @pl.when(safe)
    def _():
        qs = stacked_q()
        qf = qs.astype(f32)
        qn = jnp.sqrt(jnp.sum(qf * qf, axis=-1, keepdims=True))
        row = lax.broadcasted_iota(jnp.int32, qn.shape, 0)
        kbase = (b * A_HEADS + h) * 2
        shift = qn * jnp.where(row < tq, kmax_ref[kbase], kmax_ref[kbase + 1])
        finish(jnp.exp(scores(qs) - shift))

    @pl.when(jnp.logical_not(safe))
    def _():
        s = scores(stacked_q())
        finish(jnp.exp(s - jnp.max(s, axis=-1, keepdims=True)))


ATTN_SAFE_BOUND = 40.0


def _attn_call(lam, qa, ka, va, qn2, kn2, subln_g, B, S, tq):
    T = qa.shape[0]
    nq = S // tq
    nsub = 2 * A_HEADS
    qmax = jnp.sqrt(qn2[:, 0, :nsub])
    kmax = jnp.sqrt(kn2[:, 0, :nsub])
    safe = jnp.all((qmax * kmax).reshape(B, A_HEADS, 2) < ATTN_SAFE_BOUND, axis=-1)
    return pl.pallas_call(
        functools.partial(_attn_kernel, tq=tq),
        grid_spec=pltpu.PrefetchScalarGridSpec(
            num_scalar_prefetch=2,
            grid=(B, A_HEADS, nq),
            in_specs=[pl.BlockSpec(memory_space=pltpu.SMEM),
                      pl.BlockSpec((tq, LANES), lambda b, h, i, km, sf: (b * nq + i, h)),
                      pl.BlockSpec((S, LANES), lambda b, h, i, km, sf: (b, h)),
                      pl.BlockSpec((S, LANES), lambda b, h, i, km, sf: (b, h)),
                      pl.BlockSpec((1, LANES), lambda b, h, i, km, sf: (0, 0))],
            out_specs=pl.BlockSpec((tq, LANES), lambda b, h, i, km, sf: (b * nq + i, h)),
            scratch_shapes=[pltpu.VMEM((S, 2 * LANES), bf16)],
        ),
        out_shape=jax.ShapeDtypeStruct((T, A_V_W), bf16),
        compiler_params=_cparams(("arbitrary", "arbitrary", "arbitrary")),
        name="attn",
    )(kmax.reshape(-1), safe.reshape(-1).astype(jnp.int32), lam, qa, ka, va, subln_g)


def _ret_kernel(df_ref, db_ref, q_ref, k_ref, v_ref, gr_ref, o_ref, uf_ref, ub_ref):
    p = pl.program_id(1)
    C = R_CHUNK
    S = q_ref.shape[0]
    n = S // C
    lane = lax.broadcasted_iota(jnp.int32, (C, LANES), 1)
    rowi = lax.broadcasted_iota(jnp.int32, (C, LANES), 0)
    first = lane < R_DK
    ones = jnp.ones((C, LANES), f32)
    lgf_a, lgf_b = -jnp.exp(ones * df_ref[2 * p]), -jnp.exp(ones * df_ref[2 * p + 1])
    lgb_a, lgb_b = -jnp.exp(ones * db_ref[2 * p]), -jnp.exp(ones * db_ref[2 * p + 1])
    lgf = jnp.where(first, lgf_a, lgf_b)
    lgb = jnp.where(first, lgb_a, lgb_b)
    idx = rowi.astype(f32)
    k_f = jnp.exp(lgf * (C - 1 - idx))
    k_b = jnp.exp(lgb * idx)
    q_f = jnp.exp(lgf * (idx + 1))
    q_b = jnp.exp(lgb * (C - idx))
    top = rowi < R_DK
    g_f = jnp.where(top, jnp.exp(lgf_a * C), jnp.exp(lgf_b * C))
    g_b = jnp.where(top, jnp.exp(lgb_a * C), jnp.exp(lgb_b * C))
    g_f = jnp.concatenate([g_f, g_f], axis=1)
    g_b = jnp.concatenate([g_b, g_b], axis=1)
    rel = (rowi - lane).astype(f32)

    def decay_mask(lf, lb):
        return jnp.where(rel >= 0, jnp.exp(lf * jnp.maximum(rel, 0.0)), jnp.exp(lb * jnp.maximum(-rel, 0.0)))

    mask = jnp.concatenate([decay_mask(lgf_a, lgb_a), decay_mask(lgf_b, lgb_b)], axis=0)

    def summaries(c, carry):
        r0 = pl.multiple_of(c * C, C)
        kc = k_ref[pl.ds(r0, C), :]
        vc = v_ref[pl.ds(r0, C), :]
        kt = jnp.concatenate([(kc * k_f).T, (kc * k_b).T], axis=0).astype(bf16)
        uu = jnp.dot(kt, vc, preferred_element_type=f32)
        uf_ref[c] = uu[:LANES]
        ub_ref[c] = uu[LANES:]
        return carry

    lax.fori_loop(0, n, summaries, 0, unroll=RET_UNROLL)

    def scan_f(c, st):
        u = uf_ref[c]
        uf_ref[c] = st
        return g_f * st + u

    lax.fori_loop(0, n, scan_f, jnp.zeros((LANES, 2 * LANES), f32))

    def scan_b(j, st):
        c = n - 1 - j
        u = ub_ref[c]
        ub_ref[c] = st
        return g_b * st + u

    lax.fori_loop(0, n, scan_b, jnp.zeros((LANES, 2 * LANES), f32))

    def outputs(c, carry):
        r0 = pl.multiple_of(c * C, C)
        qc = q_ref[pl.ds(r0, C), :]
        kc = k_ref[pl.ds(r0, C), :]
        vc = v_ref[pl.ds(r0, C), :]
        zero = jnp.zeros_like(qc)
        q2 = jnp.concatenate([jnp.where(first, qc, zero), jnp.where(first, zero, qc)], axis=0)
        s = lax.dot_general(q2.astype(bf16), kc.astype(bf16), (((1,), (1,)), ((), ())),
                            preferred_element_type=f32)
        qf2 = jnp.concatenate([q_f, q_f], axis=0)
        qb2 = jnp.concatenate([q_b, q_b], axis=0)
        state_q = q2 * qf2 + pltpu.roll(q2 * qb2, R_DK, 1)
        lhs = jnp.concatenate([s * mask, state_q], axis=1).astype(bf16)
        uf, ub = uf_ref[c], ub_ref[c]
        states = jnp.concatenate([
            jnp.concatenate([uf[:R_DK, :LANES], ub[R_DK:, LANES:]], axis=1),
            jnp.concatenate([ub[:R_DK, :LANES], uf[R_DK:, LANES:]], axis=1)], axis=0)
        rhs = jnp.concatenate([vc, states.astype(bf16)], axis=0)
        yy = jnp.dot(lhs, rhs, preferred_element_type=f32)
        ya = yy[:C, :LANES]
        yb = yy[C:, LANES:]
        ya = ya * lax.rsqrt(jnp.mean(ya * ya, axis=-1, keepdims=True) + EPS)
        yb = yb * lax.rsqrt(jnp.mean(yb * yb, axis=-1, keepdims=True) + EPS)
        y = jnp.concatenate([ya, yb], axis=1)
        o_ref[pl.ds(r0, C), :] = (gr_ref[pl.ds(r0, C), :].astype(f32) * y).astype(o_ref.dtype)
        return carry

    lax.fori_loop(0, n, outputs, 0, unroll=RET_UNROLL)


def _ret_call(decay_f, decay_b, qr, kr, vr, gr, B, S):
    T = qr.shape[0]
    n = S // R_CHUNK
    blk = lambda w: pl.BlockSpec((S, w), lambda b, p: (b, p))
    return pl.pallas_call(
        _ret_kernel,
        grid=(B, R_HEADS // 2),
        in_specs=[pl.BlockSpec(memory_space=pltpu.SMEM), pl.BlockSpec(memory_space=pltpu.SMEM),
                  blk(LANES), blk(LANES), blk(2 * LANES), blk(2 * LANES)],
        out_specs=blk(2 * LANES),
        out_shape=jax.ShapeDtypeStruct((T, R_V_W), bf16),
        scratch_shapes=[pltpu.VMEM((n, LANES, 2 * LANES), f32), pltpu.VMEM((n, LANES, 2 * LANES), f32)],
        compiler_params=_cparams(("arbitrary", "arbitrary")),
        name="ret",
    )(decay_f, decay_b, qr, kr, vr, gr)


def _outproj_kernel(oa_ref, ob_ref, ga_ref, gb_ref, x_ref, wpa_ref, wpb_ref, wo_ref, g_ref,
                    wr_ref, br_ref, cnt0_ref,
                    x1_ref, row_ref, meta_ref, cnt_ref, tri_ref, carry_ref, *, tm):
    i = pl.program_id(0)

    @pl.when(i == 0)
    def _():
        r = lax.broadcasted_iota(jnp.int32, (tm, tm), 0)
        c = lax.broadcasted_iota(jnp.int32, (tm, tm), 1)
        tri_ref[...] = jnp.where(r < c, 1.0, 0.0).astype(bf16)
        carry_ref[...] = cnt0_ref[...]

    pa = jnp.dot(oa_ref[...], wpa_ref[...], preferred_element_type=f32)
    pb = jnp.dot(ob_ref[...], wpb_ref[...], preferred_element_type=f32)
    merged = ga_ref[...].astype(f32) * pa + gb_ref[...].astype(f32) * pb
    x1 = x_ref[...] + jnp.dot(merged.astype(bf16), wo_ref[...], preferred_element_type=f32)
    x1_ref[...] = x1
    xn = x1 * lax.rsqrt(jnp.mean(x1 * x1, axis=-1, keepdims=True) + EPS)
    xn = xn * g_ref[...]
    row_ref[:, :HALF] = _pack_halves(xn)

    xh = xn.astype(bf16)
    xl = (xn - xh.astype(f32)).astype(bf16)
    parts = jnp.dot(jnp.concatenate([xh, xl], axis=0), wr_ref[...], preferred_element_type=f32)
    logits = parts[:tm, :LANES] + parts[:tm, LANES:] + parts[tm:, :LANES] + br_ref[...]

    lt = logits.T
    row = lax.broadcasted_iota(jnp.int32, (LANES, tm), 0)
    row_f = row.astype(f32)
    ninf = jnp.full((LANES, tm), -jnp.inf, f32)
    big = jnp.full((LANES, tm), float(LANES - 1), f32)

    def first_row(hit):
        return jnp.min(jnp.where(hit, row_f, big), axis=0, keepdims=True).astype(jnp.int32)

    isg = row < N_GROUPS
    lg = jnp.where(isg, lt, ninf)
    mg = jnp.max(lg, axis=0, keepdims=True)
    gsel = first_row(lg == mg)
    pg_top = 1.0 / jnp.sum(jnp.where(isg, jnp.exp(lt - mg), 0.0), axis=0, keepdims=True)
    lo = N_GROUPS + EXPERTS_PER_GROUP * gsel
    le = jnp.where((row >= lo) & (row < lo + EXPERTS_PER_GROUP), lt, ninf)
    m1 = jnp.max(le, axis=0, keepdims=True)
    i1 = first_row(le == m1)
    le2 = jnp.where(row == i1, ninf, le)
    m2 = jnp.max(le2, axis=0, keepdims=True)
    i2 = first_row(le2 == m2)
    t2 = jnp.exp(m2 - m1)
    gate1 = pg_top / (1.0 + t2)
    gate2 = pg_top * t2 / (1.0 + t2)
    e1 = i1 - lo
    e2 = i2 - lo
    ea = jnp.minimum(e1, e2)
    eb = jnp.maximum(e1, e2)
    gate_a = jnp.where(e1 < e2, gate1, gate2)
    gate_b = jnp.where(e1 < e2, gate2, gate1)
    pair = lax.shift_right_logical(ea * (2 * EXPERTS_PER_GROUP - 1 - ea), 1) + (eb - ea - 1)
    cls = gsel * N_PAIRS + pair
    gates_t = jnp.where(row == 0, gate_a, jnp.where(row == 1, gate_b, 0.0))
    row_ref[:, HALF:] = lax.bitcast_convert_type(gates_t.T, u32)

    hit = row == cls
    onehot = jnp.where(hit, 1.0, 0.0)
    carry = jnp.concatenate([carry_ref[...]] * (tm // LANES), axis=1)
    rank = jnp.dot(onehot.astype(bf16), tri_ref[...], preferred_element_type=f32) + carry
    pos = jnp.sum(jnp.where(hit, rank, 0.0), axis=0, keepdims=True).astype(jnp.int32)
    carry_ref[...] = carry_ref[...] + jnp.sum(onehot, axis=1, keepdims=True)
    cnt_ref[...] = carry_ref[...]
    row8 = lax.broadcasted_iota(jnp.int32, (8, tm), 0)
    meta = jnp.where(row8 == 0, cls, jnp.where(row8 == 1, pos, 0))
    for c in range(tm // IDX_TILE):
        meta_ref[c] = meta[:, c * IDX_TILE:(c + 1) * IDX_TILE]


def _outproj_call(oa, ob, ga, gb, x2d, wpa, wpb, wo, g, wr, br, cnt0, tm):
    T = x2d.shape[0]
    row = lambda i: (i, 0)
    const = lambda i: (0, 0)
    return pl.pallas_call(
        functools.partial(_outproj_kernel, tm=tm),
        grid=(T // tm,),
        in_specs=[pl.BlockSpec((tm, A_V_W), row), pl.BlockSpec((tm, R_V_W), row),
                  pl.BlockSpec((tm, D_MODEL), row), pl.BlockSpec((tm, D_MODEL), row),
                  pl.BlockSpec((tm, D_MODEL), row),
                  pl.BlockSpec((A_V_W, D_MODEL), const), pl.BlockSpec((R_V_W, D_MODEL), const),
                  pl.BlockSpec((D_MODEL, D_MODEL), const), pl.BlockSpec((1, D_MODEL), const),
                  pl.BlockSpec((D_MODEL, 2 * LANES), const), pl.BlockSpec((1, LANES), const),
                  pl.BlockSpec((LANES, LANES), const)],
        out_specs=[pl.BlockSpec((tm, D_MODEL), row), pl.BlockSpec((tm, ROW_W), row),
                   pl.BlockSpec((tm // IDX_TILE, 8, IDX_TILE), lambda i: (i, 0, 0)),
                   pl.BlockSpec((LANES, LANES), const)],
        out_shape=[jax.ShapeDtypeStruct((T, D_MODEL), f32), jax.ShapeDtypeStruct((T, ROW_W), u32),
                   jax.ShapeDtypeStruct((T // IDX_TILE, 8, IDX_TILE), jnp.int32),
                   jax.ShapeDtypeStruct((LANES, LANES), f32)],
        scratch_shapes=[pltpu.VMEM((tm, tm), bf16), pltpu.VMEM((LANES, LANES), f32)],
        compiler_params=_cparams(("arbitrary",)),
        name="outproj",
    )(oa, ob, ga, gb, x2d, wpa, wpb, wo, g, wr, br, cnt0)


def _dest_kernel(pstart_ref, meta_ref, o_ref):
    cls = meta_ref[:, 0, :]
    rank = meta_ref[:, 1, :]

    def body(c, acc):
        return acc + jnp.where(cls == c, pstart_ref[c], 0)

    o_ref[:, 0, :] = lax.fori_loop(0, N_CLASSES, body, rank)


def _dest_call(pstart, meta):
    nt = meta.shape[0]
    blk = min(nt, DEST_TILES)
    return pl.pallas_call(
        _dest_kernel,
        grid=(nt // blk,),
        in_specs=[pl.BlockSpec(memory_space=pltpu.SMEM),
                  pl.BlockSpec((blk, 8, IDX_TILE), lambda i: (i, 0, 0))],
        out_specs=pl.BlockSpec((blk, 1, IDX_TILE), lambda i: (i, 0, 0)),
        out_shape=jax.ShapeDtypeStruct((nt, 1, IDX_TILE), jnp.int32),
        compiler_params=_cparams(("arbitrary",)),
        name="dest",
    )(pstart, meta)


def _dispatch_kernel(lastblk_ref, dest_ref, *refs, tiles):
    rows_refs = refs[:len(tiles)]
    xs_ref, buf_ref, zero_ref, sem_in, sem_out, zsem = refs[len(tiles):]
    i = pl.program_id(0)
    n = sum(tiles)
    slot = i % 2

    def tile_copy(rows_ref, t, s):
        r0 = pl.multiple_of(t * IDX_TILE, IDX_TILE)
        return pltpu.make_async_copy(rows_ref.at[pl.ds(r0, IDX_TILE), :], buf_ref.at[s], sem_in.at[s])

    def load_start(t, s):
        t = jnp.asarray(t, jnp.int32)
        first = 0
        for rows_ref, cnt in zip(rows_refs, tiles):
            @pl.when((t >= first) & (t < first + cnt))
            def _(rows_ref=rows_ref, first=first):
                tile_copy(rows_ref, t - first, s).start()
            first += cnt

    def load_wait(s):
        tile_copy(rows_refs[0], 0, s).wait()

    def scatter_wait(s):
        pltpu.make_async_copy(buf_ref.at[s], xs_ref.at[pl.ds(0, IDX_TILE), :], sem_out.at[s]).wait()

    @pl.when(i == 0)
    def _():
        load_start(0, 0)
        zero_ref[...] = jnp.zeros_like(zero_ref)

        def zero_block(j):
            off = pl.multiple_of(j * MOE_BLK, MOE_BLK)
            pltpu.make_async_copy(zero_ref, xs_ref.at[pl.ds(off, MOE_BLK), :], zsem).start()

        def zfill_last(c, carry):
            j = lastblk_ref[c]

            @pl.when(j >= 0)
            def _():
                zero_block(j)

            return carry

        def zfill_unused(j, carry):
            zero_block(j)
            return carry

        def zwait(c, carry):
            pltpu.make_async_copy(zero_ref, xs_ref.at[pl.ds(0, MOE_BLK), :], zsem).wait()
            return carry

        n_used = lastblk_ref[N_CLASSES]
        n_blocks = xs_ref.shape[0] // MOE_BLK
        lax.fori_loop(0, N_CLASSES, zfill_last, 0)
        lax.fori_loop(n_used, n_blocks, zfill_unused, 0)
        lax.fori_loop(0, lastblk_ref[N_CLASSES + 1] + n_blocks - n_used, zwait, 0)

    @pl.when(i >= 1)
    def _():
        scatter_wait(1 - slot)

    @pl.when(i + 1 < n)
    def _():
        load_start(i + 1, 1 - slot)

    load_wait(slot)
    for r in range(IDX_TILE):
        d = dest_ref[0, 0, r]
        pltpu.make_async_copy(buf_ref.at[slot, pl.ds(r, 1), :], xs_ref.at[pl.ds(d, 1), :],
                              sem_out.at[slot]).start(priority=r % 2)

    @pl.when(i == n - 1)
    def _():
        scatter_wait(slot)


def _dispatch_call(lastblk, dests, rows, P):
    tiles = tuple(r.shape[0] // IDX_TILE for r in rows)
    any_spec = pl.BlockSpec(memory_space=pl.ANY)
    return pl.pallas_call(
        functools.partial(_dispatch_kernel, tiles=tiles),
        grid=(sum(tiles),),
        in_specs=[pl.BlockSpec(memory_space=pltpu.SMEM),
                  pl.BlockSpec((1, 1, IDX_TILE), lambda i: (i, 0, 0), memory_space=pltpu.SMEM)]
                 + [any_spec] * len(rows),
        out_specs=any_spec,
        scratch_shapes=[pltpu.VMEM((2, IDX_TILE, ROW_W), u32), pltpu.VMEM((MOE_BLK, ROW_W), u32),
                        pltpu.SemaphoreType.DMA((2,)), pltpu.SemaphoreType.DMA((2,)),
                        pltpu.SemaphoreType.DMA(())],
        out_shape=jax.ShapeDtypeStruct((P, ROW_W), u32),
        compiler_params=_cparams(("arbitrary",), has_side_effects=True),
        name="dispatch",
    )(lastblk, jnp.concatenate(dests, axis=0), *rows)


def _expert_kernel(ea_ref, eb_ref, nb_ref, xs_ref, w1a_ref, w3a_ref, w2a_ref, w1b_ref, w3b_ref, w2b_ref,
                   y_ref):
    j = pl.program_id(0)

    @pl.when(j < nb_ref[0])
    def _():
        x = _unpack_halves(xs_ref[:, :HALF]).astype(bf16)
        gates = lax.bitcast_convert_type(xs_ref[:, HALF:], f32)
        ga = gates[:, 0:1]
        gb = gates[:, 1:2]

        def ffn(w1_ref, w3_ref, w2_ref, gate):
            a = jnp.dot(x, w1_ref[0], preferred_element_type=f32)
            b = jnp.dot(x, w3_ref[0], preferred_element_type=f32)
            h = ((a * _sigmoid(a)) * b) * gate
            return jnp.dot(h.astype(bf16), w2_ref[0], preferred_element_type=f32)

        y_ref[...] = _pack_halves(ffn(w1a_ref, w3a_ref, w2a_ref, ga) + ffn(w1b_ref, w3b_ref, w2b_ref, gb))

    @pl.when(j >= nb_ref[0])
    def _():
        y_ref[...] = jnp.zeros_like(y_ref)


def _expert_call(blk_ea, blk_eb, n_used, xs, w1, w3, w2):
    P = xs.shape[0]
    nb = P // MOE_BLK
    wa = lambda j, ea, eb, n: (ea[j], 0, 0)
    wb = lambda j, ea, eb, n: (eb[j], 0, 0)
    up = (1, D_MODEL, D_EXPERT)
    down = (1, D_EXPERT, D_MODEL)
    return pl.pallas_call(
        _expert_kernel,
        grid_spec=pltpu.PrefetchScalarGridSpec(
            num_scalar_prefetch=3,
            grid=(nb,),
            in_specs=[pl.BlockSpec((MOE_BLK, ROW_W), lambda j, ea, eb, n: (jnp.minimum(j, n[0] - 1), 0)),
                      pl.BlockSpec(up, wa), pl.BlockSpec(up, wa), pl.BlockSpec(down, wa),
                      pl.BlockSpec(up, wb), pl.BlockSpec(up, wb), pl.BlockSpec(down, wb)],
            out_specs=pl.BlockSpec((MOE_BLK, HALF), lambda j, ea, eb, n: (j, 0)),
        ),
        out_shape=jax.ShapeDtypeStruct((P, HALF), u32),
        compiler_params=_cparams(("arbitrary",)),
        name="experts",
    )(blk_ea, blk_eb, n_used, xs, w1, w3, w2, w1, w3, w2)


def _combine_kernel(dest_ref, x1_ref, g_ref, y_ref, o_ref, buf_ref, sems, *, n):
    i = pl.program_id(0)
    slot = i % 2

    @pl.when(i < n)
    def _():
        for r in range(IDX_TILE):
            d = dest_ref[0, 0, r]
            pltpu.make_async_copy(y_ref.at[pl.ds(d, 1), :], buf_ref.at[slot, pl.ds(r, 1), :],
                                  sems.at[slot]).start(priority=r % 2)

    @pl.when(i >= 1)
    def _():
        prev = 1 - slot
        pltpu.make_async_copy(y_ref.at[pl.ds(0, IDX_TILE), :], buf_ref.at[prev], sems.at[prev]).wait()
        x = x1_ref[...] + _unpack_halves(buf_ref[prev])
        y = x * lax.rsqrt(jnp.mean(x * x, axis=-1, keepdims=True) + EPS)
        o_ref[...] = y * g_ref[...]


def _combine_call(dest, x1, g, y):
    T = x1.shape[0]
    n = T // IDX_TILE
    done = lambda i: (jnp.maximum(i - 1, 0), 0)
    return pl.pallas_call(
        functools.partial(_combine_kernel, n=n),
        grid=(n + 1,),
        in_specs=[pl.BlockSpec((1, 1, IDX_TILE), lambda i: (jnp.minimum(i, n - 1), 0, 0), memory_space=pltpu.SMEM),
                  pl.BlockSpec((IDX_TILE, D_MODEL), done),
                  pl.BlockSpec((1, D_MODEL), lambda i: (0, 0)),
                  pl.BlockSpec(memory_space=pl.ANY)],
        out_specs=pl.BlockSpec((IDX_TILE, D_MODEL), done),
        scratch_shapes=[pltpu.VMEM((2, IDX_TILE, HALF), u32), pltpu.SemaphoreType.DMA((2,))],
        out_shape=jax.ShapeDtypeStruct((T, D_MODEL), f32),
        compiler_params=_cparams(("arbitrary",)),
        name="combine",
    )(dest, x1, g, y)


def _pair_table():
    pa, pb = [], []
    for a in range(EXPERTS_PER_GROUP):
        for b in range(a + 1, EXPERTS_PER_GROUP):
            pa.append(a)
            pb.append(b)
    return np.asarray(pa, np.int32), np.asarray(pb, np.int32)


def _route_plan(counts, T):
    nb = T // MOE_BLK + N_CLASSES
    cnt = counts[:N_CLASSES, 0].astype(jnp.int32)
    nblk = (cnt + MOE_BLK - 1) // MOE_BLK
    cum = jnp.cumsum(nblk)
    pstart = jnp.pad((cum - nblk) * MOE_BLK, (0, LANES - N_CLASSES)).astype(jnp.int32)
    blk = jnp.minimum(jnp.arange(nb, dtype=jnp.int32), cum[-1] - 1)
    blk_cls = jnp.minimum(jnp.sum((cum[None, :] <= blk[:, None]).astype(jnp.int32), axis=1), N_CLASSES - 1)
    pa, pb = _pair_table()
    grp = blk_cls // N_PAIRS
    pr = blk_cls % N_PAIRS
    blk_ea = grp * EXPERTS_PER_GROUP + jnp.asarray(pa)[pr]
    blk_eb = grp * EXPERTS_PER_GROUP + jnp.asarray(pb)[pr]
    n_used = cum[-1:].astype(jnp.int32)
    lastblk = jnp.pad(jnp.concatenate([jnp.where(cnt > 0, cum - 1, -1), cum[-1:], jnp.sum(cnt > 0)[None]]),
                      (0, LANES - N_CLASSES - 2)).astype(jnp.int32)
    return pstart, lastblk, blk_ea.astype(jnp.int32), blk_eb.astype(jnp.int32), n_used, nb * MOE_BLK


ATTN_SCORE_ELEMS = 1 << 21


def _mixer(x, lam, wts, tabs, counts0, tm, tq):
    B, S, _ = x.shape
    tq = tq or min(S, ATTN_SCORE_ELEMS // S)
    T = B * S
    x2d = x.reshape(T, D_MODEL)
    tabs_a, tabs_r = tabs
    qa, ka, va, qr, kr, vr, gr, ga, gb, qn2, kn2 = _inproj_call(
        x2d, S, wts['attn_norm_g'], wts['w_in'], [t[:S] for t in tabs_a], [t[:S] for t in tabs_r], tm)
    oa = _attn_call(lam, qa, ka, va, qn2, kn2, wts['subln_g'], B, S, tq)
    ob = _ret_call(wts['ret_decay_f'], wts['ret_decay_b'], qr, kr, vr, gr, B, S)
    return _outproj_call(oa, ob, ga, gb, x2d, wts['w_pa'], wts['w_pb'], wts['w_out'], wts['ffn_norm_g'],
                         wts['wr'], wts['b_r'], counts0, tm)


def _layer(xs_in, lam, wts, tabs, tm=512, tq=None):
    counts = jnp.zeros((LANES, LANES), f32)
    mixed = []
    for x in xs_in:
        x1, rows, meta, counts = _mixer(x, lam, wts, tabs, counts, tm, tq)
        mixed.append((x1, rows, meta))
    total = sum(x.shape[0] * x.shape[1] for x in xs_in)
    pstart, lastblk, blk_ea, blk_eb, n_used, P = _route_plan(counts, total)
    dests = [_dest_call(pstart, meta) for _, _, meta in mixed]
    xs = _dispatch_call(lastblk, dests, [rows for _, rows, _ in mixed], P)
    y = _expert_call(blk_ea, blk_eb, n_used, xs, wts['w1'], wts['w3'], wts['w2'])
    return tuple(_combine_call(dest, x1, wts['final_norm_g'], y).reshape(x.shape)
                 for (x1, _, _), dest, x in zip(mixed, dests, xs_in))


def kernel(x_prompt, x_sample, attn_norm_g, w_in, lam_q1, lam_k1, lam_q2, lam_k2, subln_g,
           ret_decay_f, ret_decay_b, w_pa, w_pb, w_out, ffn_norm_g, w_rg, b_rg, w_re, b_re,
           w1, w3, w2, final_norm_g):
    wr = jnp.concatenate([w_rg[0], jnp.transpose(w_re[0], (1, 0, 2)).reshape(D_MODEL, N_EXPERTS)], axis=1)
    wr = jnp.pad(wr, ((0, 0), (0, LANES - wr.shape[1])))
    wr_hi = wr.astype(bf16)
    b_r = jnp.pad(jnp.concatenate([b_rg[0], b_re[0].reshape(-1)]), (0, LANES - N_GROUPS - N_EXPERTS))
    wts = dict(
        attn_norm_g=attn_norm_g[0][None, :], w_in=w_in[0].astype(bf16), subln_g=subln_g[0][None, :],
        ret_decay_f=ret_decay_f[0], ret_decay_b=ret_decay_b[0],
        w_pa=w_pa[0].astype(bf16), w_pb=w_pb[0].astype(bf16), w_out=w_out[0].astype(bf16),
        ffn_norm_g=ffn_norm_g[0][None, :],
        wr=jnp.concatenate([wr_hi, (wr - wr_hi.astype(f32)).astype(bf16)], axis=1),
        b_r=b_r[None, :], w1=w1[0].astype(bf16), w3=w3[0].astype(bf16), w2=w2[0].astype(bf16),
        final_norm_g=final_norm_g[None, :])
    lam = _lam_call(lam_q1, lam_k1, lam_q2, lam_k2)[0, :1]
    s_max = max(x_prompt.shape[1], x_sample.shape[1])
    tabs = (_rot_tables(s_max, A_ROT, A_THETA, A_HD), _rot_tables(s_max, R_DK, R_THETA, R_DK))
    return _layer((x_prompt, x_sample), lam, wts, tabs)
```

```python
import functools
import math

import numpy as np
import jax
import jax.numpy as jnp
from jax import lax
from jax.experimental import pallas as pl
from jax.experimental.pallas import tpu as pltpu

f32 = jnp.float32
bf16 = jnp.bfloat16

D_MODEL = 1024
A_HEADS = 4
A_HD = 64
A_VD = 128
A_ROT = 16
A_THETA = 500000.0
R_HEADS = 4
R_DK = 64
R_DV = 128
R_CHUNK = 128
R_THETA = 10000.0
N_GROUPS = 4
EXPERTS_PER_GROUP = 8
N_EXPERTS = 32
D_EXPERT = 512
EPS = 1e-6
LAM_INIT = 0.8 - 0.6 * math.exp(-0.3 * 0)

A_QK_W = 512
A_V_W = 512
R_QK_W = 256
R_V_W = 512
D_IN = 5120
OFF_QA, OFF_KA, OFF_VA, OFF_QR, OFF_KR, OFF_VR, OFF_GR, OFF_GA, OFF_GB = (
    0, 512, 1024, 1536, 1792, 2048, 2560, 3072, 4096)

LANES = 128
SUB_RUN = 32
N_PAIRS = 28
N_CLASSES = N_GROUPS * N_PAIRS
HALF = D_MODEL // 2
ROW_W = HALF + LANES
u32 = jnp.uint32
BF16_BITS = 16
HIGH_HALF = 0xFFFF0000
MOE_BLK = 256
IDX_TILE = 512
DEST_TILES = 128
RET_UNROLL = 8
VMEM_LIMIT = 56 * 1024 * 1024


def _sigmoid(x):
    return 0.5 * jnp.tanh(0.5 * x) + 0.5


def _pack_halves(x):
    xb = x.astype(jnp.bfloat16).astype(f32)
    hi = lax.bitcast_convert_type(xb[:, :HALF], u32)
    lo = lax.bitcast_convert_type(xb[:, HALF:], u32)
    return hi | lax.shift_right_logical(lo, jnp.full_like(lo, BF16_BITS))


def _unpack_halves(w):
    hi = lax.bitcast_convert_type(w & jnp.full_like(w, HIGH_HALF), f32)
    lo = lax.bitcast_convert_type(lax.shift_left(w, jnp.full_like(w, BF16_BITS)), f32)
    return jnp.concatenate([hi, lo], axis=1)


def _cparams(sem, **kw):
    return pltpu.CompilerParams(dimension_semantics=sem, vmem_limit_bytes=VMEM_LIMIT, **kw)


def _lam_kernel(q1_ref, k1_ref, q2_ref, k2_ref, o_ref):
    a = jnp.sum(q1_ref[...] * k1_ref[...], axis=-1, keepdims=True)
    b = jnp.sum(q2_ref[...] * k2_ref[...], axis=-1, keepdims=True)
    lam = jnp.exp(a) - jnp.exp(b) + LAM_INIT
    o_ref[...] = jnp.broadcast_to(lam, o_ref.shape)


def _lam_call(q1, k1, q2, k2):
    return pl.pallas_call(
        _lam_kernel,
        out_shape=jax.ShapeDtypeStruct((8, LANES), f32),
        name="lam",
    )(q1, k1, q2, k2)


def _rotary(z, c, s1, s2, shift):
    outs = []
    for j in range(z.shape[1] // LANES):
        zz = z[:, j * LANES:(j + 1) * LANES]
        outs.append(zz * c + pltpu.roll(zz, LANES - shift, 1) * s1 + pltpu.roll(zz, shift, 1) * s2)
    return jnp.concatenate(outs, axis=1)


def _rotary_half(z, c, s):
    outs = []
    for j in range(z.shape[1] // LANES):
        zz = z[:, j * LANES:(j + 1) * LANES]
        outs.append(zz * c + pltpu.roll(zz, LANES // 2, 1) * s)
    return jnp.concatenate(outs, axis=1)


def _attn_lane_perm():
    cols = []
    for lane in range(LANES):
        half, c, m = lane // (LANES // 2), (lane % (LANES // 2)) // SUB_RUN, lane % SUB_RUN
        d = m + (A_ROT // 2) * half if m < A_ROT // 2 else A_ROT + (m - A_ROT // 2) + (SUB_RUN - A_ROT // 2) * half
        cols.append(c * A_HD + d)
    return np.asarray(cols, np.int32)


def _in_columns():
    cols = np.arange(D_IN, dtype=np.int32)
    slab = _attn_lane_perm()
    for off in (OFF_QA, OFF_KA):
        for h in range(A_HEADS):
            cols[off + h * LANES:off + (h + 1) * LANES] = off + h * LANES + slab
    return cols


def _in_sub0(lane):
    return (lane // SUB_RUN) % 2 == 0


def _inproj_kernel(x_ref, g_ref, w_ref, ca_ref, sa_ref, cr_ref, sr1_ref, sr2_ref,
                   qa_ref, ka_ref, va_ref, qr_ref, kr_ref, vr_ref, gr_ref, ga_ref, gb_ref,
                   qn_ref, kn_ref, *, nps):
    x = x_ref[...]
    xn = x * lax.rsqrt(jnp.mean(x * x, axis=-1, keepdims=True) + EPS)
    xn = (xn * g_ref[...]).astype(bf16)

    def proj(off, width):
        return jnp.dot(xn, w_ref[:, off:off + width], preferred_element_type=f32)

    ca, sa = ca_ref[...], sa_ref[...]
    cr, sr1, sr2 = cr_ref[...], sr1_ref[...], sr2_ref[...]
    qa = (_rotary_half(proj(OFF_QA, A_QK_W), ca, sa) * (A_HD ** -0.5)).astype(bf16)
    ka = _rotary_half(proj(OFF_KA, A_QK_W), ca, sa).astype(bf16)
    qa_ref[...] = qa
    ka_ref[...] = ka

    col = lax.broadcasted_iota(jnp.int32, (A_QK_W, LANES), 0)
    sub = jnp.where(_in_sub0(col), 0, 1)
    seg = (col // LANES) * 2 + sub == lax.broadcasted_iota(jnp.int32, (A_QK_W, LANES), 1)
    ind = jnp.where(seg, 1.0, 0.0).astype(bf16)

    def norm2max(z):
        n2 = jnp.dot(z * z, ind, preferred_element_type=f32)
        return jnp.max(n2, axis=0, keepdims=True)

    qmax, kmax = norm2max(qa), norm2max(ka)

    va_ref[...] = proj(OFF_VA, A_V_W).astype(bf16)
    qr_ref[...] = _rotary(proj(OFF_QR, R_QK_W), cr, sr1, sr2, R_DK // 2)
    kr_ref[...] = _rotary(proj(OFF_KR, R_QK_W), cr, sr1, sr2, R_DK // 2) * (R_DK ** -0.5)
    vr_ref[...] = proj(OFF_VR, R_V_W).astype(bf16)
    gr = proj(OFF_GR, R_V_W)
    gr_ref[...] = (gr * _sigmoid(gr)).astype(bf16)
    ga_ref[...] = _sigmoid(proj(OFF_GA, D_MODEL)).astype(bf16)
    gb_ref[...] = _sigmoid(proj(OFF_GB, D_MODEL)).astype(bf16)

    first = pl.program_id(0) % nps == 0

    @pl.when(first)
    def _():
        qn_ref[0] = qmax
        kn_ref[0] = kmax

    @pl.when(jnp.logical_not(first))
    def _():
        qn_ref[0] = jnp.maximum(qn_ref[0], qmax)
        kn_ref[0] = jnp.maximum(kn_ref[0], kmax)


def _rot_tables(S, rot_dim, theta, head_dim):
    inv = theta ** (-jnp.arange(0, rot_dim, 2, dtype=f32) / rot_dim)
    pos = jnp.arange(S, dtype=f32)
    ang = pos[:, None] * inv[None, :]
    cos, sin = jnp.cos(ang), jnp.sin(ang)
    half = rot_dim // 2
    pad = head_dim - rot_dim
    c = jnp.concatenate([cos, cos, jnp.ones((S, pad), f32)], axis=1)
    s1 = jnp.concatenate([-sin, jnp.zeros((S, half + pad), f32)], axis=1)
    s2 = jnp.concatenate([jnp.zeros((S, half), f32), sin, jnp.zeros((S, pad), f32)], axis=1)
    rep = LANES // head_dim
    return tuple(jnp.tile(t, (1, rep)) for t in (c, s1, s2))


def _rot_tables_half(S, rot_dim, theta):
    inv = theta ** (-jnp.arange(0, rot_dim, 2, dtype=f32) / rot_dim)
    ang = jnp.arange(S, dtype=f32)[:, None] * inv[None, :]
    cos, sin = jnp.cos(ang), jnp.sin(ang)
    rest = SUB_RUN - rot_dim // 2
    run_c = jnp.concatenate([cos, jnp.ones((S, rest), f32)], axis=1)
    run_s = jnp.concatenate([sin, jnp.zeros((S, rest), f32)], axis=1)
    return (jnp.tile(run_c, (1, LANES // SUB_RUN)),
            jnp.concatenate([-run_s, -run_s, run_s, run_s], axis=1))


def _inproj_call(x2d, S, g, w_in_bf, tabs_a, tabs_r, tm):
    T = x2d.shape[0]
    nps = S // tm
    row = lambda i: (i, 0)
    posmap = lambda i: (i % nps, 0)
    const = lambda i: (0, 0)
    tab_spec = pl.BlockSpec((tm, LANES), posmap)
    outs = [
        (A_QK_W, bf16), (A_QK_W, bf16), (A_V_W, bf16),
        (R_QK_W, f32), (R_QK_W, f32), (R_V_W, bf16), (R_V_W, bf16),
        (D_MODEL, bf16), (D_MODEL, bf16),
    ]
    nrm_spec = pl.BlockSpec((1, 1, LANES), lambda i: (i // nps, 0, 0))
    nrm_shape = jax.ShapeDtypeStruct((T // S, 1, LANES), f32)
    return pl.pallas_call(
        functools.partial(_inproj_kernel, nps=nps),
        grid=(T // tm,),
        in_specs=[pl.BlockSpec((tm, D_MODEL), row),
                  pl.BlockSpec((1, D_MODEL), const),
                  pl.BlockSpec((D_MODEL, D_IN), const)] + [tab_spec] * (len(tabs_a) + len(tabs_r)),
        out_specs=[pl.BlockSpec((tm, w), row) for w, _ in outs] + [nrm_spec, nrm_spec],
        out_shape=[jax.ShapeDtypeStruct((T, w), dt) for w, dt in outs] + [nrm_shape, nrm_shape],
        compiler_params=_cparams(("arbitrary",)),
        name="inproj",
    )(x2d, g, w_in_bf, *tabs_a, *tabs_r)


def _attn_kernel(kmax_ref, safe_ref, lam_ref, q_ref, k_ref, v_ref, g_ref, o_ref, vaug_ref, *, tq):
    b, h, qi = pl.program_id(0), pl.program_id(1), pl.program_id(2)
    S = v_ref.shape[0]

    @pl.when(qi == 0)
    def _():
        vaug_ref[:, :LANES] = v_ref[...]
        vaug_ref[:, LANES:] = jnp.ones((S, LANES), bf16)

    def stacked_q():
        q = q_ref[...]
        sub0 = _in_sub0(lax.broadcasted_iota(jnp.int32, q.shape, 1))
        zero = jnp.zeros_like(q)
        return jnp.concatenate([jnp.where(sub0, q, zero), jnp.where(sub0, zero, q)], axis=0)

    def scores(qs):
        return lax.dot_general(qs, k_ref[...], (((1,), (1,)), ((), ())), preferred_element_type=f32)

    def finish(e):
        oa = jnp.dot(e.astype(bf16), vaug_ref[...], preferred_element_type=f32)
        o = oa[:, :LANES] / oa[:, LANES:]
        a = o[:tq] - lam_ref[0] * o[tq:]
        y = a * lax.rsqrt(jnp.mean(a * a, axis=-1, keepdims=True) + EPS)
        o_ref[...] = ((y * g_ref[...]) * (1.0 - LAM_INIT)).astype(o_ref.dtype)

    safe = safe_ref[b * A_HEADS + h] == 1

    @pl.when(safe)
    def _():
        qs = stacked_q()
        qf = qs.astype(f32)
        qn = jnp.sqrt(jnp.sum(qf * qf, axis=-1, keepdims=True))
        row = lax.broadcasted_iota(jnp.int32, qn.shape, 0)
        kbase = (b * A_HEADS + h) * 2
        shift = qn * jnp.where(row < tq, kmax_ref[kbase], kmax_ref[kbase + 1])
        finish(jnp.exp(scores(qs) - shift))

    @pl.when(jnp.logical_not(safe))
    def _():
        s = scores(stacked_q())
        finish(jnp.exp(s - jnp.max(s, axis=-1, keepdims=True)))


ATTN_SAFE_BOUND = 40.0


def _attn_call(lam, qa, ka, va, qn2, kn2, subln_g, B, S, tq):
    T = qa.shape[0]
    nq = S // tq
    nsub = 2 * A_HEADS
    qmax = jnp.sqrt(qn2[:, 0, :nsub])
    kmax = jnp.sqrt(kn2[:, 0, :nsub])
    safe = jnp.all((qmax * kmax).reshape(B, A_HEADS, 2) < ATTN_SAFE_BOUND, axis=-1)
    return pl.pallas_call(
        functools.partial(_attn_kernel, tq=tq),
        grid_spec=pltpu.PrefetchScalarGridSpec(
            num_scalar_prefetch=2,
            grid=(B, A_HEADS, nq),
            in_specs=[pl.BlockSpec(memory_space=pltpu.SMEM),
                      pl.BlockSpec((tq, LANES), lambda b, h, i, km, sf: (b * nq + i, h)),
                      pl.BlockSpec((S, LANES), lambda b, h, i, km, sf: (b, h)),
                      pl.BlockSpec((S, LANES), lambda b, h, i, km, sf: (b, h)),
                      pl.BlockSpec((1, LANES), lambda b, h, i, km, sf: (0, 0))],
            out_specs=pl.BlockSpec((tq, LANES), lambda b, h, i, km, sf: (b * nq + i, h)),
            scratch_shapes=[pltpu.VMEM((S, 2 * LANES), bf16)],
        ),
        out_shape=jax.ShapeDtypeStruct((T, A_V_W), bf16),
        compiler_params=_cparams(("arbitrary", "arbitrary", "arbitrary")),
        name="attn",
    )(kmax.reshape(-1), safe.reshape(-1).astype(jnp.int32), lam, qa, ka, va, subln_g)


def _ret_kernel(df_ref, db_ref, q_ref, k_ref, v_ref, gr_ref, o_ref, uf_ref, ub_ref):
    p = pl.program_id(1)
    C = R_CHUNK
    S = q_ref.shape[0]
    n = S // C
    lane = lax.broadcasted_iota(jnp.int32, (C, LANES), 1)
    rowi = lax.broadcasted_iota(jnp.int32, (C, LANES), 0)
    first = lane < R_DK
    ones = jnp.ones((C, LANES), f32)
    lgf_a, lgf_b = -jnp.exp(ones * df_ref[2 * p]), -jnp.exp(ones * df_ref[2 * p + 1])
    lgb_a, lgb_b = -jnp.exp(ones * db_ref[2 * p]), -jnp.exp(ones * db_ref[2 * p + 1])
    lgf = jnp.where(first, lgf_a, lgf_b)
    lgb = jnp.where(first, lgb_a, lgb_b)
    idx = rowi.astype(f32)
    k_f = jnp.exp(lgf * (C - 1 - idx))
    k_b = jnp.exp(lgb * idx)
    q_f = jnp.exp(lgf * (idx + 1))
    q_b = jnp.exp(lgb * (C - idx))
    top = rowi < R_DK
    g_f = jnp.where(top, jnp.exp(lgf_a * C), jnp.exp(lgf_b * C))
    g_b = jnp.where(top, jnp.exp(lgb_a * C), jnp.exp(lgb_b * C))
    g_f = jnp.concatenate([g_f, g_f], axis=1)
    g_b = jnp.concatenate([g_b, g_b], axis=1)
    rel = (rowi - lane).astype(f32)

    def decay_mask(lf, lb):
        return jnp.where(rel >= 0, jnp.exp(lf * jnp.maximum(rel, 0.0)), jnp.exp(lb * jnp.maximum(-rel, 0.0)))

    mask = jnp.concatenate([decay_mask(lgf_a, lgb_a), decay_mask(lgf_b, lgb_b)], axis=0)

    def summaries(c, carry):
        r0 = pl.multiple_of(c * C, C)
        kc = k_ref[pl.ds(r0, C), :]
        vc = v_ref[pl.ds(r0, C), :]
        kt = jnp.concatenate([(kc * k_f).T, (kc * k_b).T], axis=0).astype(bf16)
        uu = jnp.dot(kt, vc, preferred_element_type=f32)
        uf_ref[c] = uu[:LANES]
        ub_ref[c] = uu[LANES:]
        return carry

    lax.fori_loop(0, n, summaries, 0, unroll=RET_UNROLL)

    def scan_f(c, st):
        u = uf_ref[c]
        uf_ref[c] = st
        return g_f * st + u

    lax.fori_loop(0, n, scan_f, jnp.zeros((LANES, 2 * LANES), f32))

    def scan_b(j, st):
        c = n - 1 - j
        u = ub_ref[c]
        ub_ref[c] = st
        return g_b * st + u

    lax.fori_loop(0, n, scan_b, jnp.zeros((LANES, 2 * LANES), f32))

    def outputs(c, carry):
        r0 = pl.multiple_of(c * C, C)
        qc = q_ref[pl.ds(r0, C), :]
        kc = k_ref[pl.ds(r0, C), :]
        vc = v_ref[pl.ds(r0, C), :]
        zero = jnp.zeros_like(qc)
        q2 = jnp.concatenate([jnp.where(first, qc, zero), jnp.where(first, zero, qc)], axis=0)
        s = lax.dot_general(q2.astype(bf16), kc.astype(bf16), (((1,), (1,)), ((), ())),
                            preferred_element_type=f32)
        qf2 = jnp.concatenate([q_f, q_f], axis=0)
        qb2 = jnp.concatenate([q_b, q_b], axis=0)
        state_q = q2 * qf2 + pltpu.roll(q2 * qb2, R_DK, 1)
        lhs = jnp.concatenate([s * mask, state_q], axis=1).astype(bf16)
        uf, ub = uf_ref[c], ub_ref[c]
        states = jnp.concatenate([
            jnp.concatenate([uf[:R_DK, :LANES], ub[R_DK:, LANES:]], axis=1),
            jnp.concatenate([ub[:R_DK, :LANES], uf[R_DK:, LANES:]], axis=1)], axis=0)
        rhs = jnp.concatenate([vc, states.astype(bf16)], axis=0)
        yy = jnp.dot(lhs, rhs, preferred_element_type=f32)
        ya = yy[:C, :LANES]
        yb = yy[C:, LANES:]
        ya = ya * lax.rsqrt(jnp.mean(ya * ya, axis=-1, keepdims=True) + EPS)
        yb = yb * lax.rsqrt(jnp.mean(yb * yb, axis=-1, keepdims=True) + EPS)
        y = jnp.concatenate([ya, yb], axis=1)
        o_ref[pl.ds(r0, C), :] = (gr_ref[pl.ds(r0, C), :].astype(f32) * y).astype(o_ref.dtype)
        return carry

    lax.fori_loop(0, n, outputs, 0, unroll=RET_UNROLL)


def _ret_call(decay_f, decay_b, qr, kr, vr, gr, B, S):
    T = qr.shape[0]
    n = S // R_CHUNK
    blk = lambda w: pl.BlockSpec((S, w), lambda b, p: (b, p))
    return pl.pallas_call(
        _ret_kernel,
        grid=(B, R_HEADS // 2),
        in_specs=[pl.BlockSpec(memory_space=pltpu.SMEM), pl.BlockSpec(memory_space=pltpu.SMEM),
                  blk(LANES), blk(LANES), blk(2 * LANES), blk(2 * LANES)],
        out_specs=blk(2 * LANES),
        out_shape=jax.ShapeDtypeStruct((T, R_V_W), bf16),
        scratch_shapes=[pltpu.VMEM((n, LANES, 2 * LANES), f32), pltpu.VMEM((n, LANES, 2 * LANES), f32)],
        compiler_params=_cparams(("arbitrary", "arbitrary")),
        name="ret",
    )(decay_f, decay_b, qr, kr, vr, gr)


def _outproj_kernel(oa_ref, ob_ref, ga_ref, gb_ref, x_ref, wpa_ref, wpb_ref, wo_ref, g_ref,
                    wr_ref, br_ref, cnt0_ref,
                    x1_ref, row_ref, meta_ref, cnt_ref, tri_ref, carry_ref, *, tm):
    i = pl.program_id(0)

    @pl.when(i == 0)
    def _():
        r = lax.broadcasted_iota(jnp.int32, (tm, tm), 0)
        c = lax.broadcasted_iota(jnp.int32, (tm, tm), 1)
        tri_ref[...] = jnp.where(r < c, 1.0, 0.0).astype(bf16)
        carry_ref[...] = cnt0_ref[...]

    pa = jnp.dot(oa_ref[...], wpa_ref[...], preferred_element_type=f32)
    pb = jnp.dot(ob_ref[...], wpb_ref[...], preferred_element_type=f32)
    merged = ga_ref[...].astype(f32) * pa + gb_ref[...].astype(f32) * pb
    x1 = x_ref[...] + jnp.dot(merged.astype(bf16), wo_ref[...], preferred_element_type=f32)
    x1_ref[...] = x1
    xn = x1 * lax.rsqrt(jnp.mean(x1 * x1, axis=-1, keepdims=True) + EPS)
    xn = xn * g_ref[...]
    row_ref[:, :HALF] = _pack_halves(xn)

    xh = xn.astype(bf16)
    xl = (xn - xh.astype(f32)).astype(bf16)
    parts = jnp.dot(jnp.concatenate([xh, xl], axis=0), wr_ref[...], preferred_element_type=f32)
    logits = parts[:tm, :LANES] + parts[:tm, LANES:] + parts[tm:, :LANES] + br_ref[...]

    lt = logits.T
    row = lax.broadcasted_iota(jnp.int32, (LANES, tm), 0)
    row_f = row.astype(f32)
    ninf = jnp.full((LANES, tm), -jnp.inf, f32)
    big = jnp.full((LANES, tm), float(LANES - 1), f32)

    def first_row(hit):
        return jnp.min(jnp.where(hit, row_f, big), axis=0, keepdims=True).astype(jnp.int32)

    isg = row < N_GROUPS
    lg = jnp.where(isg, lt, ninf)
    mg = jnp.max(lg, axis=0, keepdims=True)
    gsel = first_row(lg == mg)
    pg_top = 1.0 / jnp.sum(jnp.where(isg, jnp.exp(lt - mg), 0.0), axis=0, keepdims=True)
    lo = N_GROUPS + EXPERTS_PER_GROUP * gsel
    le = jnp.where((row >= lo) & (row < lo + EXPERTS_PER_GROUP), lt, ninf)
    m1 = jnp.max(le, axis=0, keepdims=True)
    i1 = first_row(le == m1)
    le2 = jnp.where(row == i1, ninf, le)
    m2 = jnp.max(le2, axis=0, keepdims=True)
    i2 = first_row(le2 == m2)
    t2 = jnp.exp(m2 - m1)
    gate1 = pg_top / (1.0 + t2)
    gate2 = pg_top * t2 / (1.0 + t2)
    e1 = i1 - lo
    e2 = i2 - lo
    ea = jnp.minimum(e1, e2)
    eb = jnp.maximum(e1, e2)
    gate_a = jnp.where(e1 < e2, gate1, gate2)
    gate_b = jnp.where(e1 < e2, gate2, gate1)
    pair = lax.shift_right_logical(ea * (2 * EXPERTS_PER_GROUP - 1 - ea), 1) + (eb - ea - 1)
    cls = gsel * N_PAIRS + pair
    gates_t = jnp.where(row == 0, gate_a, jnp.where(row == 1, gate_b, 0.0))
    row_ref[:, HALF:] = lax.bitcast_convert_type(gates_t.T, u32)

    hit = row == cls
    onehot = jnp.where(hit, 1.0, 0.0)
    carry = jnp.concatenate([carry_ref[...]] * (tm // LANES), axis=1)
    rank = jnp.dot(onehot.astype(bf16), tri_ref[...], preferred_element_type=f32) + carry
    pos = jnp.sum(jnp.where(hit, rank, 0.0), axis=0, keepdims=True).astype(jnp.int32)
    carry_ref[...] = carry_ref[...] + jnp.sum(onehot, axis=1, keepdims=True)
    cnt_ref[...] = carry_ref[...]
    row8 = lax.broadcasted_iota(jnp.int32, (8, tm), 0)
    meta = jnp.where(row8 == 0, cls, jnp.where(row8 == 1, pos, 0))
    for c in range(tm // IDX_TILE):
        meta_ref[c] = meta[:, c * IDX_TILE:(c + 1) * IDX_TILE]


def _outproj_call(oa, ob, ga, gb, x2d, wpa, wpb, wo, g, wr, br, cnt0, tm):
    T = x2d.shape[0]
    row = lambda i: (i, 0)
    const = lambda i: (0, 0)
    return pl.pallas_call(
        functools.partial(_outproj_kernel, tm=tm),
        grid=(T // tm,),
        in_specs=[pl.BlockSpec((tm, A_V_W), row), pl.BlockSpec((tm, R_V_W), row),
                  pl.BlockSpec((tm, D_MODEL), row), pl.BlockSpec((tm, D_MODEL), row),
                  pl.BlockSpec((tm, D_MODEL), row),
                  pl.BlockSpec((A_V_W, D_MODEL), const), pl.BlockSpec((R_V_W, D_MODEL), const),
                  pl.BlockSpec((D_MODEL, D_MODEL), const), pl.BlockSpec((1, D_MODEL), const),
                  pl.BlockSpec((D_MODEL, 2 * LANES), const), pl.BlockSpec((1, LANES), const),
                  pl.BlockSpec((LANES, LANES), const)],
        out_specs=[pl.BlockSpec((tm, D_MODEL), row), pl.BlockSpec((tm, ROW_W), row),
                   pl.BlockSpec((tm // IDX_TILE, 8, IDX_TILE), lambda i: (i, 0, 0)),
                   pl.BlockSpec((LANES, LANES), const)],
        out_shape=[jax.ShapeDtypeStruct((T, D_MODEL), f32), jax.ShapeDtypeStruct((T, ROW_W), u32),
                   jax.ShapeDtypeStruct((T // IDX_TILE, 8, IDX_TILE), jnp.int32),
                   jax.ShapeDtypeStruct((LANES, LANES), f32)],
        scratch_shapes=[pltpu.VMEM((tm, tm), bf16), pltpu.VMEM((LANES, LANES), f32)],
        compiler_params=_cparams(("arbitrary",)),
        name="outproj",
    )(oa, ob, ga, gb, x2d, wpa, wpb, wo, g, wr, br, cnt0)


def _dest_kernel(pstart_ref, meta_ref, o_ref):
    cls = meta_ref[:, 0, :]
    rank = meta_ref[:, 1, :]

    def body(c, acc):
        return acc + jnp.where(cls == c, pstart_ref[c], 0)

    o_ref[:, 0, :] = lax.fori_loop(0, N_CLASSES, body, rank)


def _dest_call(pstart, meta):
    nt = meta.shape[0]
    blk = min(nt, DEST_TILES)
    return pl.pallas_call(
        _dest_kernel,
        grid=(nt // blk,),
        in_specs=[pl.BlockSpec(memory_space=pltpu.SMEM),
                  pl.BlockSpec((blk, 8, IDX_TILE), lambda i: (i, 0, 0))],
        out_specs=pl.BlockSpec((blk, 1, IDX_TILE), lambda i: (i, 0, 0)),
        out_shape=jax.ShapeDtypeStruct((nt, 1, IDX_TILE), jnp.int32),
        compiler_params=_cparams(("arbitrary",)),
        name="dest",
    )(pstart, meta)


def _dispatch_kernel(lastblk_ref, dest_ref, *refs, tiles):
    rows_refs = refs[:len(tiles)]
    xs_ref, buf_ref, zero_ref, sem_in, sem_out, zsem = refs[len(tiles):]
    i = pl.program_id(0)
    n = sum(tiles)
    slot = i % 2

    def tile_copy(rows_ref, t, s):
        r0 = pl.multiple_of(t * IDX_TILE, IDX_TILE)
        return pltpu.make_async_copy(rows_ref.at[pl.ds(r0, IDX_TILE), :], buf_ref.at[s], sem_in.at[s])

    def load_start(t, s):
        t = jnp.asarray(t, jnp.int32)
        first = 0
        for rows_ref, cnt in zip(rows_refs, tiles):
            @pl.when((t >= first) & (t < first + cnt))
            def _(rows_ref=rows_ref, first=first):
                tile_copy(rows_ref, t - first, s).start()
            first += cnt

    def load_wait(s):
        tile_copy(rows_refs[0], 0, s).wait()

    def scatter_wait(s):
        pltpu.make_async_copy(buf_ref.at[s], xs_ref.at[pl.ds(0, IDX_TILE), :], sem_out.at[s]).wait()

    @pl.when(i == 0)
    def _():
        load_start(0, 0)
        zero_ref[...] = jnp.zeros_like(zero_ref)

        def zero_block(j):
            off = pl.multiple_of(j * MOE_BLK, MOE_BLK)
            pltpu.make_async_copy(zero_ref, xs_ref.at[pl.ds(off, MOE_BLK), :], zsem).start()

        def zfill_last(c, carry):
            j = lastblk_ref[c]

            @pl.when(j >= 0)
            def _():
                zero_block(j)

            return carry

        def zfill_unused(j, carry):
            zero_block(j)
            return carry

        def zwait(c, carry):
            pltpu.make_async_copy(zero_ref, xs_ref.at[pl.ds(0, MOE_BLK), :], zsem).wait()
            return carry

        n_used = lastblk_ref[N_CLASSES]
        n_blocks = xs_ref.shape[0] // MOE_BLK
        lax.fori_loop(0, N_CLASSES, zfill_last, 0)
        lax.fori_loop(n_used, n_blocks, zfill_unused, 0)
        lax.fori_loop(0, lastblk_ref[N_CLASSES + 1] + n_blocks - n_used, zwait, 0)

    @pl.when(i >= 1)
    def _():
        scatter_wait(1 - slot)

    @pl.when(i + 1 < n)
    def _():
        load_start(i + 1, 1 - slot)

    load_wait(slot)
    for r in range(IDX_TILE):
        d = dest_ref[0, 0, r]
        pltpu.make_async_copy(buf_ref.at[slot, pl.ds(r, 1), :], xs_ref.at[pl.ds(d, 1), :],
                              sem_out.at[slot]).start(priority=r % 2)

    @pl.when(i == n - 1)
    def _():
        scatter_wait(slot)


def _dispatch_call(lastblk, dests, rows, P):
    tiles = tuple(r.shape[0] // IDX_TILE for r in rows)
    any_spec = pl.BlockSpec(memory_space=pl.ANY)
    return pl.pallas_call(
        functools.partial(_dispatch_kernel, tiles=tiles),
        grid=(sum(tiles),),
        in_specs=[pl.BlockSpec(memory_space=pltpu.SMEM),
                  pl.BlockSpec((1, 1, IDX_TILE), lambda i: (i, 0, 0), memory_space=pltpu.SMEM)]
                 + [any_spec] * len(rows),
        out_specs=any_spec,
        scratch_shapes=[pltpu.VMEM((2, IDX_TILE, ROW_W), u32), pltpu.VMEM((MOE_BLK, ROW_W), u32),
                        pltpu.SemaphoreType.DMA((2,)), pltpu.SemaphoreType.DMA((2,)),
                        pltpu.SemaphoreType.DMA(())],
        out_shape=jax.ShapeDtypeStruct((P, ROW_W), u32),
        compiler_params=_cparams(("arbitrary",), has_side_effects=True),
        name="dispatch",
    )(lastblk, jnp.concatenate(dests, axis=0), *rows)


def _expert_kernel(ea_ref, eb_ref, nb_ref, xs_ref, w1a_ref, w3a_ref, w2a_ref, w1b_ref, w3b_ref, w2b_ref,
                   y_ref):
    j = pl.program_id(0)

    @pl.when(j < nb_ref[0])
    def _():
        x = _unpack_halves(xs_ref[:, :HALF]).astype(bf16)
        gates = lax.bitcast_convert_type(xs_ref[:, HALF:], f32)
        ga = gates[:, 0:1]
        gb = gates[:, 1:2]

        def ffn(w1_ref, w3_ref, w2_ref, gate):
            a = jnp.dot(x, w1_ref[0], preferred_element_type=f32)
            b = jnp.dot(x, w3_ref[0], preferred_element_type=f32)
            h = ((a * _sigmoid(a)) * b) * gate
            return jnp.dot(h.astype(bf16), w2_ref[0], preferred_element_type=f32)

        y_ref[...] = _pack_halves(ffn(w1a_ref, w3a_ref, w2a_ref, ga) + ffn(w1b_ref, w3b_ref, w2b_ref, gb))

    @pl.when(j >= nb_ref[0])
    def _():
        y_ref[...] = jnp.zeros_like(y_ref)


def _expert_call(blk_ea, blk_eb, n_used, xs, w1, w3, w2):
    P = xs.shape[0]
    nb = P // MOE_BLK
    wa = lambda j, ea, eb, n: (ea[j], 0, 0)
    wb = lambda j, ea, eb, n: (eb[j], 0, 0)
    up = (1, D_MODEL, D_EXPERT)
    down = (1, D_EXPERT, D_MODEL)
    return pl.pallas_call(
        _expert_kernel,
        grid_spec=pltpu.PrefetchScalarGridSpec(
            num_scalar_prefetch=3,
            grid=(nb,),
            in_specs=[pl.BlockSpec((MOE_BLK, ROW_W), lambda j, ea, eb, n: (jnp.minimum(j, n[0] - 1), 0)),
                      pl.BlockSpec(up, wa), pl.BlockSpec(up, wa), pl.BlockSpec(down, wa),
                      pl.BlockSpec(up, wb), pl.BlockSpec(up, wb), pl.BlockSpec(down, wb)],
            out_specs=pl.BlockSpec((MOE_BLK, HALF), lambda j, ea, eb, n: (j, 0)),
        ),
        out_shape=jax.ShapeDtypeStruct((P, HALF), u32),
        compiler_params=_cparams(("arbitrary",)),
        name="experts",
    )(blk_ea, blk_eb, n_used, xs, w1, w3, w2, w1, w3, w2)


def _combine_kernel(dest_ref, x1_ref, g_ref, y_ref, o_ref, buf_ref, sems, *, n):
    i = pl.program_id(0)
    slot = i % 2

    @pl.when(i < n)
    def _():
        for r in range(IDX_TILE):
            d = dest_ref[0, 0, r]
            pltpu.make_async_copy(y_ref.at[pl.ds(d, 1), :], buf_ref.at[slot, pl.ds(r, 1), :],
                                  sems.at[slot]).start(priority=r % 2)

    @pl.when(i >= 1)
    def _():
        prev = 1 - slot
        pltpu.make_async_copy(y_ref.at[pl.ds(0, IDX_TILE), :], buf_ref.at[prev], sems.at[prev]).wait()
        x = x1_ref[...] + _unpack_halves(buf_ref[prev])
        y = x * lax.rsqrt(jnp.mean(x * x, axis=-1, keepdims=True) + EPS)
        o_ref[...] = y * g_ref[...]


def _combine_call(dest, x1, g, y):
    T = x1.shape[0]
    n = T // IDX_TILE
    done = lambda i: (jnp.maximum(i - 1, 0), 0)
    return pl.pallas_call(
        functools.partial(_combine_kernel, n=n),
        grid=(n + 1,),
        in_specs=[pl.BlockSpec((1, 1, IDX_TILE), lambda i: (jnp.minimum(i, n - 1), 0, 0), memory_space=pltpu.SMEM),
                  pl.BlockSpec((IDX_TILE, D_MODEL), done),
                  pl.BlockSpec((1, D_MODEL), lambda i: (0, 0)),
                  pl.BlockSpec(memory_space=pl.ANY)],
        out_specs=pl.BlockSpec((IDX_TILE, D_MODEL), done),
        scratch_shapes=[pltpu.VMEM((2, IDX_TILE, HALF), u32), pltpu.SemaphoreType.DMA((2,))],
        out_shape=jax.ShapeDtypeStruct((T, D_MODEL), f32),
        compiler_params=_cparams(("arbitrary",)),
        name="combine",
    )(dest, x1, g, y)


def _pair_table():
    pa, pb = [], []
    for a in range(EXPERTS_PER_GROUP):
        for b in range(a + 1, EXPERTS_PER_GROUP):
            pa.append(a)
            pb.append(b)
    return np.asarray(pa, np.int32), np.asarray(pb, np.int32)


def _route_plan(counts, T):
    nb = T // MOE_BLK + N_CLASSES
    cnt = counts[:N_CLASSES, 0].astype(jnp.int32)
    nblk = (cnt + MOE_BLK - 1) // MOE_BLK
    cum = jnp.cumsum(nblk)
    pstart = jnp.pad((cum - nblk) * MOE_BLK, (0, LANES - N_CLASSES)).astype(jnp.int32)
    blk = jnp.minimum(jnp.arange(nb, dtype=jnp.int32), cum[-1] - 1)
    blk_cls = jnp.minimum(jnp.sum((cum[None, :] <= blk[:, None]).astype(jnp.int32), axis=1), N_CLASSES - 1)
    pa, pb = _pair_table()
    grp = blk_cls // N_PAIRS
    pr = blk_cls % N_PAIRS
    blk_ea = grp * EXPERTS_PER_GROUP + jnp.asarray(pa)[pr]
    blk_eb = grp * EXPERTS_PER_GROUP + jnp.asarray(pb)[pr]
    n_used = cum[-1:].astype(jnp.int32)
    lastblk = jnp.pad(jnp.concatenate([jnp.where(cnt > 0, cum - 1, -1), cum[-1:], jnp.sum(cnt > 0)[None]]),
                      (0, LANES - N_CLASSES - 2)).astype(jnp.int32)
    return pstart, lastblk, blk_ea.astype(jnp.int32), blk_eb.astype(jnp.int32), n_used, nb * MOE_BLK


ATTN_SCORE_ELEMS = 1 << 21


def _mixer(x, lam, wts, tabs, counts0, tm, tq):
    B, S, _ = x.shape
    tq = tq or min(S, ATTN_SCORE_ELEMS // S)
    T = B * S
    x2d = x.reshape(T, D_MODEL)
    tabs_a, tabs_r = tabs
    qa, ka, va, qr, kr, vr, gr, ga, gb, qn2, kn2 = _inproj_call(
        x2d, S, wts['attn_norm_g'], wts['w_in'], [t[:S] for t in tabs_a], [t[:S] for t in tabs_r], tm)
    oa = _attn_call(lam, qa, ka, va, qn2, kn2, wts['subln_g'], B, S, tq)
    ob = _ret_call(wts['ret_decay_f'], wts['ret_decay_b'], qr, kr, vr, gr, B, S)
    return _outproj_call(oa, ob, ga, gb, x2d, wts['w_pa'], wts['w_pb'], wts['w_out'], wts['ffn_norm_g'],
                         wts['wr'], wts['b_r'], counts0, tm)


def _layer(xs_in, lam, wts, tabs, tm=512, tq=None):
    counts = jnp.zeros((LANES, LANES), f32)
    mixed = []
    for x in xs_in:
        x1, rows, meta, counts = _mixer(x, lam, wts, tabs, counts, tm, tq)
        mixed.append((x1, rows, meta))
    total = sum(x.shape[0] * x.shape[1] for x in xs_in)
    pstart, lastblk, blk_ea, blk_eb, n_used, P = _route_plan(counts, total)
    dests = [_dest_call(pstart, meta) for _, _, meta in mixed]
    xs = _dispatch_call(lastblk, dests, [rows for _, rows, _ in mixed], P)
    y = _expert_call(blk_ea, blk_eb, n_used, xs, wts['w1'], wts['w3'], wts['w2'])
    return tuple(_combine_call(dest, x1, wts['final_norm_g'], y).reshape(x.shape)
                 for (x1, _, _), dest, x in zip(mixed, dests, xs_in))


def kernel(x_prompt, x_sample, attn_norm_g, w_in, lam_q1, lam_k1, lam_q2, lam_k2, subln_g,
           ret_decay_f, ret_decay_b, w_pa, w_pb, w_out, ffn_norm_g, w_rg, b_rg, w_re, b_re,
           w1, w3, w2, final_norm_g):
    wr = jnp.concatenate([w_rg[0], jnp.transpose(w_re[0], (1, 0, 2)).reshape(D_MODEL, N_EXPERTS)], axis=1)
    wr = jnp.pad(wr, ((0, 0), (0, LANES - wr.shape[1])))
    wr_hi = wr.astype(bf16)
    b_r = jnp.pad(jnp.concatenate([b_rg[0], b_re[0].reshape(-1)]), (0, LANES - N_GROUPS - N_EXPERTS))
    wts = dict(
        attn_norm_g=attn_norm_g[0][None, :], w_in=w_in[0][:, _in_columns()].astype(bf16),
        subln_g=subln_g[0][None, :],
        ret_decay_f=ret_decay_f[0], ret_decay_b=ret_decay_b[0],
        w_pa=w_pa[0].astype(bf16), w_pb=w_pb[0].astype(bf16), w_out=w_out[0].astype(bf16),
        ffn_norm_g=ffn_norm_g[0][None, :],
        wr=jnp.concatenate([wr_hi, (wr - wr_hi.astype(f32)).astype(bf16)], axis=1),
        b_r=b_r[None, :], w1=w1[0].astype(bf16), w3=w3[0].astype(bf16), w2=w2[0].astype(bf16),
        final_norm_g=final_norm_g[None, :])
    lam = _lam_call(lam_q1, lam_k1, lam_q2, lam_k2)[0, :1]
    s_max = max(x_prompt.shape[1], x_sample.shape[1])
    tabs = (_rot_tables_half(s_max, A_ROT, A_THETA), _rot_tables(s_max, R_DK, R_THETA, R_DK))
    return _layer((x_prompt, x_sample), lam, wts, tabs)
```

```python
import functools
import math

import numpy as np
import jax
import jax.numpy as jnp
from jax import lax
from jax.experimental import pallas as pl
from jax.experimental.pallas import tpu as pltpu

f32 = jnp.float32
bf16 = jnp.bfloat16

D_MODEL = 1024
A_HEADS = 4
A_HD = 64
A_VD = 128
A_ROT = 16
A_THETA = 500000.0
R_HEADS = 4
R_DK = 64
R_DV = 128
R_CHUNK = 128
R_THETA = 10000.0
N_GROUPS = 4
EXPERTS_PER_GROUP = 8
N_EXPERTS = 32
D_EXPERT = 512
EPS = 1e-6
LAM_INIT = 0.8 - 0.6 * math.exp(-0.3 * 0)

A_QK_W = 512
A_V_W = 512
R_QK_W = 256
R_V_W = 512
D_IN = 5120
OFF_QA, OFF_KA, OFF_VA, OFF_QR, OFF_KR, OFF_VR, OFF_GR, OFF_GA, OFF_GB = (
    0, 512, 1024, 1536, 1792, 2048, 2560, 3072, 4096)

LANES = 128
SUB_RUN = 32
N_PAIRS = 28
N_CLASSES = N_GROUPS * N_PAIRS
HALF = D_MODEL // 2
ROW_W = HALF + LANES
u32 = jnp.uint32
BF16_BITS = 16
HIGH_HALF = 0xFFFF0000
MOE_BLK = 256
IDX_TILE = 512
DEST_TILES = 128
RET_UNROLL = 8
VMEM_LIMIT = 56 * 1024 * 1024


def _sigmoid(x):
    return 0.5 * jnp.tanh(0.5 * x) + 0.5


def _pack_halves(x):
    xb = x.astype(jnp.bfloat16).astype(f32)
    hi = lax.bitcast_convert_type(xb[:, :HALF], u32)
    lo = lax.bitcast_convert_type(xb[:, HALF:], u32)
    return hi | lax.shift_right_logical(lo, jnp.full_like(lo, BF16_BITS))


def _unpack_halves(w):
    hi = lax.bitcast_convert_type(w & jnp.full_like(w, HIGH_HALF), f32)
    lo = lax.bitcast_convert_type(lax.shift_left(w, jnp.full_like(w, BF16_BITS)), f32)
    return jnp.concatenate([hi, lo], axis=1)


def _cparams(sem, **kw):
    return pltpu.CompilerParams(dimension_semantics=sem, vmem_limit_bytes=VMEM_LIMIT, **kw)


def _lam_kernel(q1_ref, k1_ref, q2_ref, k2_ref, o_ref):
    a = jnp.sum(q1_ref[...] * k1_ref[...], axis=-1, keepdims=True)
    b = jnp.sum(q2_ref[...] * k2_ref[...], axis=-1, keepdims=True)
    lam = jnp.exp(a) - jnp.exp(b) + LAM_INIT
    o_ref[...] = jnp.broadcast_to(lam, o_ref.shape)


def _lam_call(q1, k1, q2, k2):
    return pl.pallas_call(
        _lam_kernel,
        out_shape=jax.ShapeDtypeStruct((8, LANES), f32),
        name="lam",
    )(q1, k1, q2, k2)


def _rotary(z, c, s1, s2, shift):
    outs = []
    for j in range(z.shape[1] // LANES):
        zz = z[:, j * LANES:(j + 1) * LANES]
        outs.append(zz * c + pltpu.roll(zz, LANES - shift, 1) * s1 + pltpu.roll(zz, shift, 1) * s2)
    return jnp.concatenate(outs, axis=1)


def _rotary_half(z, c, s):
    outs = []
    for j in range(z.shape[1] // LANES):
        zz = z[:, j * LANES:(j + 1) * LANES]
        outs.append(zz * c + pltpu.roll(zz, LANES // 2, 1) * s)
    return jnp.concatenate(outs, axis=1)


def _attn_lane_perm():
    cols = []
    for lane in range(LANES):
        half, c, m = lane // (LANES // 2), (lane % (LANES // 2)) // SUB_RUN, lane % SUB_RUN
        d = m + (A_ROT // 2) * half if m < A_ROT // 2 else A_ROT + (m - A_ROT // 2) + (SUB_RUN - A_ROT // 2) * half
        cols.append(c * A_HD + d)
    return np.asarray(cols, np.int32)


def _relane_in_columns(w):
    perm = _attn_lane_perm()
    starts = [0] + [i for i in range(1, LANES) if perm[i] != perm[i - 1] + 1] + [LANES]
    n_slabs = (OFF_VA - OFF_QA) // LANES
    slabs = w[:, OFF_QA:OFF_VA].reshape(w.shape[0], n_slabs, LANES)
    runs = [slabs[:, :, perm[a]:perm[a] + (b - a)] for a, b in zip(starts[:-1], starts[1:])]
    qk = jnp.concatenate(runs, axis=2).reshape(w.shape[0], OFF_VA - OFF_QA)
    return jnp.concatenate([qk, w[:, OFF_VA:]], axis=1)


def _in_sub0(lane):
    return (lane // SUB_RUN) % 2 == 0


def _inproj_kernel(x_ref, g_ref, w_ref, ca_ref, sa_ref, cr_ref, sr1_ref, sr2_ref,
                   qa_ref, ka_ref, va_ref, qr_ref, kr_ref, vr_ref, gr_ref, ga_ref, gb_ref,
                   qn_ref, kn_ref, *, nps):
    x = x_ref[...]
    xn = x * lax.rsqrt(jnp.mean(x * x, axis=-1, keepdims=True) + EPS)
    xn = (xn * g_ref[...]).astype(bf16)

    def proj(off, width):
        return jnp.dot(xn, w_ref[:, off:off + width], preferred_element_type=f32)

    ca, sa = ca_ref[...], sa_ref[...]
    cr, sr1, sr2 = cr_ref[...], sr1_ref[...], sr2_ref[...]
    qa = (_rotary_half(proj(OFF_QA, A_QK_W), ca, sa) * (A_HD ** -0.5)).astype(bf16)
    ka = _rotary_half(proj(OFF_KA, A_QK_W), ca, sa).astype(bf16)
    qa_ref[...] = qa
    ka_ref[...] = ka

    col = lax.broadcasted_iota(jnp.int32, (A_QK_W, LANES), 0)
    sub = jnp.where(_in_sub0(col), 0, 1)
    seg = (col // LANES) * 2 + sub == lax.broadcasted_iota(jnp.int32, (A_QK_W, LANES), 1)
    ind = jnp.where(seg, 1.0, 0.0).astype(bf16)

    def norm2max(z):
        n2 = jnp.dot(z * z, ind, preferred_element_type=f32)
        return jnp.max(n2, axis=0, keepdims=True)

    qmax, kmax = norm2max(qa), norm2max(ka)

    va_ref[...] = proj(OFF_VA, A_V_W).astype(bf16)
    qr_ref[...] = _rotary(proj(OFF_QR, R_QK_W), cr, sr1, sr2, R_DK // 2)
    kr_ref[...] = _rotary(proj(OFF_KR, R_QK_W), cr, sr1, sr2, R_DK // 2) * (R_DK ** -0.5)
    vr_ref[...] = proj(OFF_VR, R_V_W).astype(bf16)
    gr = proj(OFF_GR, R_V_W)
    gr_ref[...] = (gr * _sigmoid(gr)).astype(bf16)
    ga_ref[...] = _sigmoid(proj(OFF_GA, D_MODEL)).astype(bf16)
    gb_ref[...] = _sigmoid(proj(OFF_GB, D_MODEL)).astype(bf16)

    first = pl.program_id(0) % nps == 0

    @pl.when(first)
    def _():
        qn_ref[0] = qmax
        kn_ref[0] = kmax

    @pl.when(jnp.logical_not(first))
    def _():
        qn_ref[0] = jnp.maximum(qn_ref[0], qmax)
        kn_ref[0] = jnp.maximum(kn_ref[0], kmax)


def _rot_tables(S, rot_dim, theta, head_dim):
    inv = theta ** (-jnp.arange(0, rot_dim, 2, dtype=f32) / rot_dim)
    pos = jnp.arange(S, dtype=f32)
    ang = pos[:, None] * inv[None, :]
    cos, sin = jnp.cos(ang), jnp.sin(ang)
    half = rot_dim // 2
    pad = head_dim - rot_dim
    c = jnp.concatenate([cos, cos, jnp.ones((S, pad), f32)], axis=1)
    s1 = jnp.concatenate([-sin, jnp.zeros((S, half + pad), f32)], axis=1)
    s2 = jnp.concatenate([jnp.zeros((S, half), f32), sin, jnp.zeros((S, pad), f32)], axis=1)
    rep = LANES // head_dim
    return tuple(jnp.tile(t, (1, rep)) for t in (c, s1, s2))


def _rot_tables_half(S, rot_dim, theta):
    inv = theta ** (-jnp.arange(0, rot_dim, 2, dtype=f32) / rot_dim)
    ang = jnp.arange(S, dtype=f32)[:, None] * inv[None, :]
    cos, sin = jnp.cos(ang), jnp.sin(ang)
    rest = SUB_RUN - rot_dim // 2
    run_c = jnp.concatenate([cos, jnp.ones((S, rest), f32)], axis=1)
    run_s = jnp.concatenate([sin, jnp.zeros((S, rest), f32)], axis=1)
    return (jnp.tile(run_c, (1, LANES // SUB_RUN)),
            jnp.concatenate([-run_s, -run_s, run_s, run_s], axis=1))


def _inproj_call(x2d, S, g, w_in_bf, tabs_a, tabs_r, tm):
    T = x2d.shape[0]
    nps = S // tm
    row = lambda i: (i, 0)
    posmap = lambda i: (i % nps, 0)
    const = lambda i: (0, 0)
    tab_spec = pl.BlockSpec((tm, LANES), posmap)
    outs = [
        (A_QK_W, bf16), (A_QK_W, bf16), (A_V_W, bf16),
        (R_QK_W, f32), (R_QK_W, f32), (R_V_W, bf16), (R_V_W, bf16),
        (D_MODEL, bf16), (D_MODEL, bf16),
    ]
    nrm_spec = pl.BlockSpec((1, 1, LANES), lambda i: (i // nps, 0, 0))
    nrm_shape = jax.ShapeDtypeStruct((T // S, 1, LANES), f32)
    return pl.pallas_call(
        functools.partial(_inproj_kernel, nps=nps),
        grid=(T // tm,),
        in_specs=[pl.BlockSpec((tm, D_MODEL), row),
                  pl.BlockSpec((1, D_MODEL), const),
                  pl.BlockSpec((D_MODEL, D_IN), const)] + [tab_spec] * (len(tabs_a) + len(tabs_r)),
        out_specs=[pl.BlockSpec((tm, w), row) for w, _ in outs] + [nrm_spec, nrm_spec],
        out_shape=[jax.ShapeDtypeStruct((T, w), dt) for w, dt in outs] + [nrm_shape, nrm_shape],
        compiler_params=_cparams(("arbitrary",)),
        name="inproj",
    )(x2d, g, w_in_bf, *tabs_a, *tabs_r)


def _attn_kernel(kmax_ref, safe_ref, lam_ref, q_ref, k_ref, v_ref, g_ref, o_ref, vaug_ref, *, tq):
    b, h, qi = pl.program_id(0), pl.program_id(1), pl.program_id(2)
    S = v_ref.shape[0]

    @pl.when(qi == 0)
    def _():
        vaug_ref[:, :LANES] = v_ref[...]
        vaug_ref[:, LANES:] = jnp.ones((S, LANES), bf16)

    def stacked_q():
        q = q_ref[...]
        sub0 = _in_sub0(lax.broadcasted_iota(jnp.int32, q.shape, 1))
        zero = jnp.zeros_like(q)
        return jnp.concatenate([jnp.where(sub0, q, zero), jnp.where(sub0, zero, q)], axis=0)

    def scores(qs):
        return lax.dot_general(qs, k_ref[...], (((1,), (1,)), ((), ())), preferred_element_type=f32)

    def finish(e):
        oa = jnp.dot(e.astype(bf16), vaug_ref[...], preferred_element_type=f32)
        o = oa[:, :LANES] / oa[:, LANES:]
        a = o[:tq] - lam_ref[0] * o[tq:]
        y = a * lax.rsqrt(jnp.mean(a * a, axis=-1, keepdims=True) + EPS)
        o_ref[...] = ((y * g_ref[...]) * (1.0 - LAM_INIT)).astype(o_ref.dtype)

    safe = safe_ref[b * A_HEADS + h] == 1

    @pl.when(safe)
    def _():
        qs = stacked_q()
        qf = qs.astype(f32)
        qn = jnp.sqrt(jnp.sum(qf * qf, axis=-1, keepdims=True))
        row = lax.broadcasted_iota(jnp.int32, qn.shape, 0)
        kbase = (b * A_HEADS + h) * 2
        shift = qn * jnp.where(row < tq, kmax_ref[kbase], kmax_ref[kbase + 1])
        finish(jnp.exp(scores(qs) - shift))

    @pl.when(jnp.logical_not(safe))
    def _():
        s = scores(stacked_q())
        finish(jnp.exp(s - jnp.max(s, axis=-1, keepdims=True)))


ATTN_SAFE_BOUND = 40.0


def _attn_call(lam, qa, ka, va, qn2, kn2, subln_g, B, S, tq):
    T = qa.shape[0]
    nq = S // tq
    nsub = 2 * A_HEADS
    qmax = jnp.sqrt(qn2[:, 0, :nsub])
    kmax = jnp.sqrt(kn2[:, 0, :nsub])
    safe = jnp.all((qmax * kmax).reshape(B, A_HEADS, 2) < ATTN_SAFE_BOUND, axis=-1)
    return pl.pallas_call(
        functools.partial(_attn_kernel, tq=tq),
        grid_spec=pltpu.PrefetchScalarGridSpec(
            num_scalar_prefetch=2,
            grid=(B, A_HEADS, nq),
            in_specs=[pl.BlockSpec(memory_space=pltpu.SMEM),
                      pl.BlockSpec((tq, LANES), lambda b, h, i, km, sf: (b * nq + i, h)),
                      pl.BlockSpec((S, LANES), lambda b, h, i, km, sf: (b, h)),
                      pl.BlockSpec((S, LANES), lambda b, h, i, km, sf: (b, h)),
                      pl.BlockSpec((1, LANES), lambda b, h, i, km, sf: (0, 0))],
            out_specs=pl.BlockSpec((tq, LANES), lambda b, h, i, km, sf: (b * nq + i, h)),
            scratch_shapes=[pltpu.VMEM((S, 2 * LANES), bf16)],
        ),
        out_shape=jax.ShapeDtypeStruct((T, A_V_W), bf16),
        compiler_params=_cparams(("arbitrary", "arbitrary", "arbitrary")),
        name="attn",
    )(kmax.reshape(-1), safe.reshape(-1).astype(jnp.int32), lam, qa, ka, va, subln_g)


def _ret_kernel(df_ref, db_ref, q_ref, k_ref, v_ref, gr_ref, o_ref, uf_ref, ub_ref):
    p = pl.program_id(1)
    C = R_CHUNK
    S = q_ref.shape[0]
    n = S // C
    lane = lax.broadcasted_iota(jnp.int32, (C, LANES), 1)
    rowi = lax.broadcasted_iota(jnp.int32, (C, LANES), 0)
    first = lane < R_DK
    ones = jnp.ones((C, LANES), f32)
    lgf_a, lgf_b = -jnp.exp(ones * df_ref[2 * p]), -jnp.exp(ones * df_ref[2 * p + 1])
    lgb_a, lgb_b = -jnp.exp(ones * db_ref[2 * p]), -jnp.exp(ones * db_ref[2 * p + 1])
    lgf = jnp.where(first, lgf_a, lgf_b)
    lgb = jnp.where(first, lgb_a, lgb_b)
    idx = rowi.astype(f32)
    k_f = jnp.exp(lgf * (C - 1 - idx))
    k_b = jnp.exp(lgb * idx)
    q_f = jnp.exp(lgf * (idx + 1))
    q_b = jnp.exp(lgb * (C - idx))
    top = rowi < R_DK
    g_f = jnp.where(top, jnp.exp(lgf_a * C), jnp.exp(lgf_b * C))
    g_b = jnp.where(top, jnp.exp(lgb_a * C), jnp.exp(lgb_b * C))
    g_f = jnp.concatenate([g_f, g_f], axis=1)
    g_b = jnp.concatenate([g_b, g_b], axis=1)
    rel = (rowi - lane).astype(f32)

    def decay_mask(lf, lb):
        return jnp.where(rel >= 0, jnp.exp(lf * jnp.maximum(rel, 0.0)), jnp.exp(lb * jnp.maximum(-rel, 0.0)))

    mask = jnp.concatenate([decay_mask(lgf_a, lgb_a), decay_mask(lgf_b, lgb_b)], axis=0)

    def summaries(c, carry):
        r0 = pl.multiple_of(c * C, C)
        kc = k_ref[pl.ds(r0, C), :]
        vc = v_ref[pl.ds(r0, C), :]
        kt = jnp.concatenate([(kc * k_f).T, (kc * k_b).T], axis=0).astype(bf16)
        uu = jnp.dot(kt, vc, preferred_element_type=f32)
        uf_ref[c] = uu[:LANES]
        ub_ref[c] = uu[LANES:]
        return carry

    lax.fori_loop(0, n, summaries, 0, unroll=RET_UNROLL)

    def scan_f(c, st):
        u = uf_ref[c]
        uf_ref[c] = st
        return g_f * st + u

    lax.fori_loop(0, n, scan_f, jnp.zeros((LANES, 2 * LANES), f32))

    def scan_b(j, st):
        c = n - 1 - j
        u = ub_ref[c]
        ub_ref[c] = st
        return g_b * st + u

    lax.fori_loop(0, n, scan_b, jnp.zeros((LANES, 2 * LANES), f32))

    def outputs(c, carry):
        r0 = pl.multiple_of(c * C, C)
        qc = q_ref[pl.ds(r0, C), :]
        kc = k_ref[pl.ds(r0, C), :]
        vc = v_ref[pl.ds(r0, C), :]
        zero = jnp.zeros_like(qc)
        q2 = jnp.concatenate([jnp.where(first, qc, zero), jnp.where(first, zero, qc)], axis=0)
        s = lax.dot_general(q2.astype(bf16), kc.astype(bf16), (((1,), (1,)), ((), ())),
                            preferred_element_type=f32)
        qf2 = jnp.concatenate([q_f, q_f], axis=0)
        qb2 = jnp.concatenate([q_b, q_b], axis=0)
        state_q = q2 * qf2 + pltpu.roll(q2 * qb2, R_DK, 1)
        lhs = jnp.concatenate([s * mask, state_q], axis=1).astype(bf16)
        uf, ub = uf_ref[c], ub_ref[c]
        states = jnp.concatenate([
            jnp.concatenate([uf[:R_DK, :LANES], ub[R_DK:, LANES:]], axis=1),
            jnp.concatenate([ub[:R_DK, :LANES], uf[R_DK:, LANES:]], axis=1)], axis=0)
        rhs = jnp.concatenate([vc, states.astype(bf16)], axis=0)
        yy = jnp.dot(lhs, rhs, preferred_element_type=f32)
        ya = yy[:C, :LANES]
        yb = yy[C:, LANES:]
        ya = ya * lax.rsqrt(jnp.mean(ya * ya, axis=-1, keepdims=True) + EPS)
        yb = yb * lax.rsqrt(jnp.mean(yb * yb, axis=-1, keepdims=True) + EPS)
        y = jnp.concatenate([ya, yb], axis=1)
        o_ref[pl.ds(r0, C), :] = (gr_ref[pl.ds(r0, C), :].astype(f32) * y).astype(o_ref.dtype)
        return carry

    lax.fori_loop(0, n, outputs, 0, unroll=RET_UNROLL)


def _ret_call(decay_f, decay_b, qr, kr, vr, gr, B, S):
    T = qr.shape[0]
    n = S // R_CHUNK
    blk = lambda w: pl.BlockSpec((S, w), lambda b, p: (b, p))
    return pl.pallas_call(
        _ret_kernel,
        grid=(B, R_HEADS // 2),
        in_specs=[pl.BlockSpec(memory_space=pltpu.SMEM), pl.BlockSpec(memory_space=pltpu.SMEM),
                  blk(LANES), blk(LANES), blk(2 * LANES), blk(2 * LANES)],
        out_specs=blk(2 * LANES),
        out_shape=jax.ShapeDtypeStruct((T, R_V_W), bf16),
        scratch_shapes=[pltpu.VMEM((n, LANES, 2 * LANES), f32), pltpu.VMEM((n, LANES, 2 * LANES), f32)],
        compiler_params=_cparams(("arbitrary", "arbitrary")),
        name="ret",
    )(decay_f, decay_b, qr, kr, vr, gr)


def _outproj_kernel(oa_ref, ob_ref, ga_ref, gb_ref, x_ref, wpa_ref, wpb_ref, wo_ref, g_ref,
                    wr_ref, br_ref, cnt0_ref,
                    x1_ref, row_ref, meta_ref, cnt_ref, tri_ref, carry_ref, *, tm):
    i = pl.program_id(0)

    @pl.when(i == 0)
    def _():
        r = lax.broadcasted_iota(jnp.int32, (tm, tm), 0)
        c = lax.broadcasted_iota(jnp.int32, (tm, tm), 1)
        tri_ref[...] = jnp.where(r < c, 1.0, 0.0).astype(bf16)
        carry_ref[...] = cnt0_ref[...]

    pa = jnp.dot(oa_ref[...], wpa_ref[...], preferred_element_type=f32)
    pb = jnp.dot(ob_ref[...], wpb_ref[...], preferred_element_type=f32)
    merged = ga_ref[...].astype(f32) * pa + gb_ref[...].astype(f32) * pb
    x1 = x_ref[...] + jnp.dot(merged.astype(bf16), wo_ref[...], preferred_element_type=f32)
    x1_ref[...] = x1
    xn = x1 * lax.rsqrt(jnp.mean(x1 * x1, axis=-1, keepdims=True) + EPS)
    xn = xn * g_ref[...]
    row_ref[:, :HALF] = _pack_halves(xn)

    xh = xn.astype(bf16)
    xl = (xn - xh.astype(f32)).astype(bf16)
    parts = jnp.dot(jnp.concatenate([xh, xl], axis=0), wr_ref[...], preferred_element_type=f32)
    logits = parts[:tm, :LANES] + parts[:tm, LANES:] + parts[tm:, :LANES] + br_ref[...]

    lt = logits.T
    row = lax.broadcasted_iota(jnp.int32, (LANES, tm), 0)
    row_f = row.astype(f32)
    ninf = jnp.full((LANES, tm), -jnp.inf, f32)
    big = jnp.full((LANES, tm), float(LANES - 1), f32)

    def first_row(hit):
        return jnp.min(jnp.where(hit, row_f, big), axis=0, keepdims=True).astype(jnp.int32)

    isg = row < N_GROUPS
    lg = jnp.where(isg, lt, ninf)
    mg = jnp.max(lg, axis=0, keepdims=True)
    gsel = first_row(lg == mg)
    pg_top = 1.0 / jnp.sum(jnp.where(isg, jnp.exp(lt - mg), 0.0), axis=0, keepdims=True)
    lo = N_GROUPS + EXPERTS_PER_GROUP * gsel
    le = jnp.where((row >= lo) & (row < lo + EXPERTS_PER_GROUP), lt, ninf)
    m1 = jnp.max(le, axis=0, keepdims=True)
    i1 = first_row(le == m1)
    le2 = jnp.where(row == i1, ninf, le)
    m2 = jnp.max(le2, axis=0, keepdims=True)
    i2 = first_row(le2 == m2)
    t2 = jnp.exp(m2 - m1)
    gate1 = pg_top / (1.0 + t2)
    gate2 = pg_top * t2 / (1.0 + t2)
    e1 = i1 - lo
    e2 = i2 - lo
    ea = jnp.minimum(e1, e2)
    eb = jnp.maximum(e1, e2)
    gate_a = jnp.where(e1 < e2, gate1, gate2)
    gate_b = jnp.where(e1 < e2, gate2, gate1)
    pair = lax.shift_right_logical(ea * (2 * EXPERTS_PER_GROUP - 1 - ea), 1) + (eb - ea - 1)
    cls = gsel * N_PAIRS + pair
    gates_t = jnp.where(row == 0, gate_a, jnp.where(row == 1, gate_b, 0.0))
    row_ref[:, HALF:] = lax.bitcast_convert_type(gates_t.T, u32)

    hit = row == cls
    onehot = jnp.where(hit, 1.0, 0.0)
    carry = jnp.concatenate([carry_ref[...]] * (tm // LANES), axis=1)
    rank = jnp.dot(onehot.astype(bf16), tri_ref[...], preferred_element_type=f32) + carry
    pos = jnp.sum(jnp.where(hit, rank, 0.0), axis=0, keepdims=True).astype(jnp.int32)
    carry_ref[...] = carry_ref[...] + jnp.sum(onehot, axis=1, keepdims=True)
    cnt_ref[...] = carry_ref[...]
    row8 = lax.broadcasted_iota(jnp.int32, (8, tm), 0)
    meta = jnp.where(row8 == 0, cls, jnp.where(row8 == 1, pos, 0))
    for c in range(tm // IDX_TILE):
        meta_ref[c] = meta[:, c * IDX_TILE:(c + 1) * IDX_TILE]


def _outproj_call(oa, ob, ga, gb, x2d, wpa, wpb, wo, g, wr, br, cnt0, tm):
    T = x2d.shape[0]
    row = lambda i: (i, 0)
    const = lambda i: (0, 0)
    return pl.pallas_call(
        functools.partial(_outproj_kernel, tm=tm),
        grid=(T // tm,),
        in_specs=[pl.BlockSpec((tm, A_V_W), row), pl.BlockSpec((tm, R_V_W), row),
                  pl.BlockSpec((tm, D_MODEL), row), pl.BlockSpec((tm, D_MODEL), row),
                  pl.BlockSpec((tm, D_MODEL), row),
                  pl.BlockSpec((A_V_W, D_MODEL), const), pl.BlockSpec((R_V_W, D_MODEL), const),
                  pl.BlockSpec((D_MODEL, D_MODEL), const), pl.BlockSpec((1, D_MODEL), const),
                  pl.BlockSpec((D_MODEL, 2 * LANES), const), pl.BlockSpec((1, LANES), const),
                  pl.BlockSpec((LANES, LANES), const)],
        out_specs=[pl.BlockSpec((tm, D_MODEL), row), pl.BlockSpec((tm, ROW_W), row),
                   pl.BlockSpec((tm // IDX_TILE, 8, IDX_TILE), lambda i: (i, 0, 0)),
                   pl.BlockSpec((LANES, LANES), const)],
        out_shape=[jax.ShapeDtypeStruct((T, D_MODEL), f32), jax.ShapeDtypeStruct((T, ROW_W), u32),
                   jax.ShapeDtypeStruct((T // IDX_TILE, 8, IDX_TILE), jnp.int32),
                   jax.ShapeDtypeStruct((LANES, LANES), f32)],
        scratch_shapes=[pltpu.VMEM((tm, tm), bf16), pltpu.VMEM((LANES, LANES), f32)],
        compiler_params=_cparams(("arbitrary",)),
        name="outproj",
    )(oa, ob, ga, gb, x2d, wpa, wpb, wo, g, wr, br, cnt0)


def _dest_kernel(pstart_ref, meta_ref, o_ref):
    cls = meta_ref[:, 0, :]
    rank = meta_ref[:, 1, :]

    def body(c, acc):
        return acc + jnp.where(cls == c, pstart_ref[c], 0)

    o_ref[:, 0, :] = lax.fori_loop(0, N_CLASSES, body, rank)


def _dest_call(pstart, meta):
    nt = meta.shape[0]
    blk = min(nt, DEST_TILES)
    return pl.pallas_call(
        _dest_kernel,
        grid=(nt // blk,),
        in_specs=[pl.BlockSpec(memory_space=pltpu.SMEM),
                  pl.BlockSpec((blk, 8, IDX_TILE), lambda i: (i, 0, 0))],
        out_specs=pl.BlockSpec((blk, 1, IDX_TILE), lambda i: (i, 0, 0)),
        out_shape=jax.ShapeDtypeStruct((nt, 1, IDX_TILE), jnp.int32),
        compiler_params=_cparams(("arbitrary",)),
        name="dest",
    )(pstart, meta)


def _dispatch_kernel(lastblk_ref, dest_ref, *refs, tiles):
    rows_refs = refs[:len(tiles)]
    xs_ref, buf_ref, zero_ref, sem_in, sem_out, zsem = refs[len(tiles):]
    i = pl.program_id(0)
    n = sum(tiles)
    slot = i % 2

    def tile_copy(rows_ref, t, s):
        r0 = pl.multiple_of(t * IDX_TILE, IDX_TILE)
        return pltpu.make_async_copy(rows_ref.at[pl.ds(r0, IDX_TILE), :], buf_ref.at[s], sem_in.at[s])

    def load_start(t, s):
        t = jnp.asarray(t, jnp.int32)
        first = 0
        for rows_ref, cnt in zip(rows_refs, tiles):
            @pl.when((t >= first) & (t < first + cnt))
            def _(rows_ref=rows_ref, first=first):
                tile_copy(rows_ref, t - first, s).start()
            first += cnt

    def load_wait(s):
        tile_copy(rows_refs[0], 0, s).wait()

    def scatter_wait(s):
        pltpu.make_async_copy(buf_ref.at[s], xs_ref.at[pl.ds(0, IDX_TILE), :], sem_out.at[s]).wait()

    @pl.when(i == 0)
    def _():
        load_start(0, 0)
        zero_ref[...] = jnp.zeros_like(zero_ref)

        def zero_block(j):
            off = pl.multiple_of(j * MOE_BLK, MOE_BLK)
            pltpu.make_async_copy(zero_ref, xs_ref.at[pl.ds(off, MOE_BLK), :], zsem).start()

        def zfill_last(c, carry):
            j = lastblk_ref[c]

            @pl.when(j >= 0)
            def _():
                zero_block(j)

            return carry

        def zfill_unused(j, carry):
            zero_block(j)
            return carry

        def zwait(c, carry):
            pltpu.make_async_copy(zero_ref, xs_ref.at[pl.ds(0, MOE_BLK), :], zsem).wait()
            return carry

        n_used = lastblk_ref[N_CLASSES]
        n_blocks = xs_ref.shape[0] // MOE_BLK
        lax.fori_loop(0, N_CLASSES, zfill_last, 0)
        lax.fori_loop(n_used, n_blocks, zfill_unused, 0)
        lax.fori_loop(0, lastblk_ref[N_CLASSES + 1] + n_blocks - n_used, zwait, 0)

    @pl.when(i >= 1)
    def _():
        scatter_wait(1 - slot)

    @pl.when(i + 1 < n)
    def _():
        load_start(i + 1, 1 - slot)

    load_wait(slot)
    for r in range(IDX_TILE):
        d = dest_ref[0, 0, r]
        pltpu.make_async_copy(buf_ref.at[slot, pl.ds(r, 1), :], xs_ref.at[pl.ds(d, 1), :],
                              sem_out.at[slot]).start(priority=r % 2)

    @pl.when(i == n - 1)
    def _():
        scatter_wait(slot)


def _dispatch_call(lastblk, dests, rows, P):
    tiles = tuple(r.shape[0] // IDX_TILE for r in rows)
    any_spec = pl.BlockSpec(memory_space=pl.ANY)
    return pl.pallas_call(
        functools.partial(_dispatch_kernel, tiles=tiles),
        grid=(sum(tiles),),
        in_specs=[pl.BlockSpec(memory_space=pltpu.SMEM),
                  pl.BlockSpec((1, 1, IDX_TILE), lambda i: (i, 0, 0), memory_space=pltpu.SMEM)]
                 + [any_spec] * len(rows),
        out_specs=any_spec,
        scratch_shapes=[pltpu.VMEM((2, IDX_TILE, ROW_W), u32), pltpu.VMEM((MOE_BLK, ROW_W), u32),
                        pltpu.SemaphoreType.DMA((2,)), pltpu.SemaphoreType.DMA((2,)),
                        pltpu.SemaphoreType.DMA(())],
        out_shape=jax.ShapeDtypeStruct((P, ROW_W), u32),
        compiler_params=_cparams(("arbitrary",), has_side_effects=True),
        name="dispatch",
    )(lastblk, jnp.concatenate(dests, axis=0), *rows)


def _expert_kernel(ea_ref, eb_ref, nb_ref, xs_ref, w1a_ref, w3a_ref, w2a_ref, w1b_ref, w3b_ref, w2b_ref,
                   y_ref):
    j = pl.program_id(0)

    @pl.when(j < nb_ref[0])
    def _():
        x = _unpack_halves(xs_ref[:, :HALF]).astype(bf16)
        gates = lax.bitcast_convert_type(xs_ref[:, HALF:], f32)
        ga = gates[:, 0:1]
        gb = gates[:, 1:2]

        def ffn(w1_ref, w3_ref, w2_ref, gate):
            a = jnp.dot(x, w1_ref[0], preferred_element_type=f32)
            b = jnp.dot(x, w3_ref[0], preferred_element_type=f32)
            h = ((a * _sigmoid(a)) * b) * gate
            return jnp.dot(h.astype(bf16), w2_ref[0], preferred_element_type=f32)

        y_ref[...] = _pack_halves(ffn(w1a_ref, w3a_ref, w2a_ref, ga) + ffn(w1b_ref, w3b_ref, w2b_ref, gb))

    @pl.when(j >= nb_ref[0])
    def _():
        y_ref[...] = jnp.zeros_like(y_ref)


def _expert_call(blk_ea, blk_eb, n_used, xs, w1, w3, w2):
    P = xs.shape[0]
    nb = P // MOE_BLK
    wa = lambda j, ea, eb, n: (ea[j], 0, 0)
    wb = lambda j, ea, eb, n: (eb[j], 0, 0)
    up = (1, D_MODEL, D_EXPERT)
    down = (1, D_EXPERT, D_MODEL)
    return pl.pallas_call(
        _expert_kernel,
        grid_spec=pltpu.PrefetchScalarGridSpec(
            num_scalar_prefetch=3,
            grid=(nb,),
            in_specs=[pl.BlockSpec((MOE_BLK, ROW_W), lambda j, ea, eb, n: (jnp.minimum(j, n[0] - 1), 0)),
                      pl.BlockSpec(up, wa), pl.BlockSpec(up, wa), pl.BlockSpec(down, wa),
                      pl.BlockSpec(up, wb), pl.BlockSpec(up, wb), pl.BlockSpec(down, wb)],
            out_specs=pl.BlockSpec((MOE_BLK, HALF), lambda j, ea, eb, n: (j, 0)),
        ),
        out_shape=jax.ShapeDtypeStruct((P, HALF), u32),
        compiler_params=_cparams(("arbitrary",)),
        name="experts",
    )(blk_ea, blk_eb, n_used, xs, w1, w3, w2, w1, w3, w2)


def _combine_kernel(dest_ref, x1_ref, g_ref, y_ref, o_ref, buf_ref, sems, *, n):
    i = pl.program_id(0)
    slot = i % 2

    @pl.when(i < n)
    def _():
        for r in range(IDX_TILE):
            d = dest_ref[0, 0, r]
            pltpu.make_async_copy(y_ref.at[pl.ds(d, 1), :], buf_ref.at[slot, pl.ds(r, 1), :],
                                  sems.at[slot]).start(priority=r % 2)

    @pl.when(i >= 1)
    def _():
        prev = 1 - slot
        pltpu.make_async_copy(y_ref.at[pl.ds(0, IDX_TILE), :], buf_ref.at[prev], sems.at[prev]).wait()
        x = x1_ref[...] + _unpack_halves(buf_ref[prev])
        y = x * lax.rsqrt(jnp.mean(x * x, axis=-1, keepdims=True) + EPS)
        o_ref[...] = y * g_ref[...]


def _combine_call(dest, x1, g, y):
    T = x1.shape[0]
    n = T // IDX_TILE
    done = lambda i: (jnp.maximum(i - 1, 0), 0)
    return pl.pallas_call(
        functools.partial(_combine_kernel, n=n),
        grid=(n + 1,),
        in_specs=[pl.BlockSpec((1, 1, IDX_TILE), lambda i: (jnp.minimum(i, n - 1), 0, 0), memory_space=pltpu.SMEM),
                  pl.BlockSpec((IDX_TILE, D_MODEL), done),
                  pl.BlockSpec((1, D_MODEL), lambda i: (0, 0)),
                  pl.BlockSpec(memory_space=pl.ANY)],
        out_specs=pl.BlockSpec((IDX_TILE, D_MODEL), done),
        scratch_shapes=[pltpu.VMEM((2, IDX_TILE, HALF), u32), pltpu.SemaphoreType.DMA((2,))],
        out_shape=jax.ShapeDtypeStruct((T, D_MODEL), f32),
        compiler_params=_cparams(("arbitrary",)),
        name="combine",
    )(dest, x1, g, y)


def _pair_table():
    pa, pb = [], []
    for a in range(EXPERTS_PER_GROUP):
        for b in range(a + 1, EXPERTS_PER_GROUP):
            pa.append(a)
            pb.append(b)
    return np.asarray(pa, np.int32), np.asarray(pb, np.int32)


def _route_plan(counts, T):
    nb = T // MOE_BLK + N_CLASSES
    cnt = counts[:N_CLASSES, 0].astype(jnp.int32)
    nblk = (cnt + MOE_BLK - 1) // MOE_BLK
    cum = jnp.cumsum(nblk)
    pstart = jnp.pad((cum - nblk) * MOE_BLK, (0, LANES - N_CLASSES)).astype(jnp.int32)
    blk = jnp.minimum(jnp.arange(nb, dtype=jnp.int32), cum[-1] - 1)
    blk_cls = jnp.minimum(jnp.sum((cum[None, :] <= blk[:, None]).astype(jnp.int32), axis=1), N_CLASSES - 1)
    pa, pb = _pair_table()
    grp = blk_cls // N_PAIRS
    pr = blk_cls % N_PAIRS
    blk_ea = grp * EXPERTS_PER_GROUP + jnp.asarray(pa)[pr]
    blk_eb = grp * EXPERTS_PER_GROUP + jnp.asarray(pb)[pr]
    n_used = cum[-1:].astype(jnp.int32)
    lastblk = jnp.pad(jnp.concatenate([jnp.where(cnt > 0, cum - 1, -1), cum[-1:], jnp.sum(cnt > 0)[None]]),
                      (0, LANES - N_CLASSES - 2)).astype(jnp.int32)
    return pstart, lastblk, blk_ea.astype(jnp.int32), blk_eb.astype(jnp.int32), n_used, nb * MOE_BLK


ATTN_SCORE_ELEMS = 1 << 21


def _mixer(x, lam, wts, tabs, counts0, tm, tq):
    B, S, _ = x.shape
    tq = tq or min(S, ATTN_SCORE_ELEMS // S)
    T = B * S
    x2d = x.reshape(T, D_MODEL)
    tabs_a, tabs_r = tabs
    qa, ka, va, qr, kr, vr, gr, ga, gb, qn2, kn2 = _inproj_call(
        x2d, S, wts['attn_norm_g'], wts['w_in'], [t[:S] for t in tabs_a], [t[:S] for t in tabs_r], tm)
    oa = _attn_call(lam, qa, ka, va, qn2, kn2, wts['subln_g'], B, S, tq)
    ob = _ret_call(wts['ret_decay_f'], wts['ret_decay_b'], qr, kr, vr, gr, B, S)
    return _outproj_call(oa, ob, ga, gb, x2d, wts['w_pa'], wts['w_pb'], wts['w_out'], wts['ffn_norm_g'],
                         wts['wr'], wts['b_r'], counts0, tm)


def _layer(xs_in, lam, wts, tabs, tm=512, tq=None):
    counts = jnp.zeros((LANES, LANES), f32)
    mixed = []
    for x in xs_in:
        x1, rows, meta, counts = _mixer(x, lam, wts, tabs, counts, tm, tq)
        mixed.append((x1, rows, meta))
    total = sum(x.shape[0] * x.shape[1] for x in xs_in)
    pstart, lastblk, blk_ea, blk_eb, n_used, P = _route_plan(counts, total)
    dests = [_dest_call(pstart, meta) for _, _, meta in mixed]
    xs = _dispatch_call(lastblk, dests, [rows for _, rows, _ in mixed], P)
    y = _expert_call(blk_ea, blk_eb, n_used, xs, wts['w1'], wts['w3'], wts['w2'])
    return tuple(_combine_call(dest, x1, wts['final_norm_g'], y).reshape(x.shape)
                 for (x1, _, _), dest, x in zip(mixed, dests, xs_in))


def kernel(x_prompt, x_sample, attn_norm_g, w_in, lam_q1, lam_k1, lam_q2, lam_k2, subln_g,
           ret_decay_f, ret_decay_b, w_pa, w_pb, w_out, ffn_norm_g, w_rg, b_rg, w_re, b_re,
           w1, w3, w2, final_norm_g):
    wr = jnp.concatenate([w_rg[0], jnp.transpose(w_re[0], (1, 0, 2)).reshape(D_MODEL, N_EXPERTS)], axis=1)
    wr = jnp.pad(wr, ((0, 0), (0, LANES - wr.shape[1])))
    wr_hi = wr.astype(bf16)
    b_r = jnp.pad(jnp.concatenate([b_rg[0], b_re[0].reshape(-1)]), (0, LANES - N_GROUPS - N_EXPERTS))
    wts = dict(
        attn_norm_g=attn_norm_g[0][None, :], w_in=_relane_in_columns(w_in[0]).astype(bf16),
        subln_g=subln_g[0][None, :],
        ret_decay_f=ret_decay_f[0], ret_decay_b=ret_decay_b[0],
        w_pa=w_pa[0].astype(bf16), w_pb=w_pb[0].astype(bf16), w_out=w_out[0].astype(bf16),
        ffn_norm_g=ffn_norm_g[0][None, :],
        wr=jnp.concatenate([wr_hi, (wr - wr_hi.astype(f32)).astype(bf16)], axis=1),
        b_r=b_r[None, :], w1=w1[0].astype(bf16), w3=w3[0].astype(bf16), w2=w2[0].astype(bf16),
        final_norm_g=final_norm_g[None, :])
    lam = _lam_call(lam_q1, lam_k1, lam_q2, lam_k2)[0, :1]
    s_max = max(x_prompt.shape[1], x_sample.shape[1])
    tabs = (_rot_tables_half(s_max, A_ROT, A_THETA), _rot_tables(s_max, R_DK, R_THETA, R_DK))
    return _layer((x_prompt, x_sample), lam, wts, tabs)
```
